```python
import math
import jax, jax.numpy as jnp
from jax import lax
import numpy as np

D_MODEL = 2048
BATCH = 4
SEQ = 2048
DEPTH = 4
DEC_BATCH = 128
DEC_SEQ = 1
PAST_LEN = 16384
PAGE_SIZE = 128

BRANCH_WIDTH = D_MODEL // 2
N_BRANCH = 3
HG_HEADS = 8
HG_DK = BRANCH_WIDTH // HG_HEADS
HG_DV = BRANCH_WIDTH // HG_HEADS
HG_CHUNK = 64
HG_LB_FLOOR = 1e-30
RET_HEADS = 4
RET_DK = BRANCH_WIDTH // RET_HEADS
RET_DV = BRANCH_WIDTH // RET_HEADS
RET_CHUNK = 128
ROPE_BASE = 10000.0
ML_HEADS = 4
ML_DK = BRANCH_WIDTH // ML_HEADS
ML_DV = BRANCH_WIDTH // ML_HEADS
ML_CHUNK = 128
ML_FGATE_BIAS_LO = 3.0
ML_FGATE_BIAS_HI = 6.0
NEG_LARGE = -1e30
NORM_EPS = 1e-6
LN_EPS = 1e-5
DEEPNORM_ALPHA = (2 * DEPTH) ** 0.25
DEEPNORM_BETA = (8 * DEPTH) ** -0.25
COL_SIZES = (BRANCH_WIDTH, BRANCH_WIDTH, BRANCH_WIDTH, BRANCH_WIDTH,
             BRANCH_WIDTH, BRANCH_WIDTH, BRANCH_WIDTH, BRANCH_WIDTH,
             BRANCH_WIDTH, BRANCH_WIDTH, BRANCH_WIDTH, BRANCH_WIDTH, BRANCH_WIDTH,
             ML_HEADS, ML_HEADS,
             N_BRANCH * D_MODEL)
D_IN = sum(COL_SIZES)
SPLIT_POINTS = tuple(int(v) for v in np.cumsum(COL_SIZES)[:-1])

kernel_name = 'hybrid_hgrn2_retention_mlstm_step'

F32 = jnp.float32


def _to_chunks(a, c):
    b, l = a.shape[:2]
    return jnp.moveaxis(a.reshape((b, l // c, c) + a.shape[2:]), 1, 0)


def _from_chunks(a):
    a = jnp.moveaxis(a, 0, 1)
    return a.reshape((a.shape[0], a.shape[1] * a.shape[2]) + a.shape[3:])


def _causal_mask(c):
    return jnp.tril(jnp.ones((c, c), dtype=bool))


def _masked_exp(mask, a):
    return jnp.where(mask, jnp.exp(jnp.where(mask, a, 0.0)), 0.0)


def _hgrn2_scan(q, k, v, log_f, s0):
    c = math.gcd(q.shape[1], HG_CHUNK)
    mask = _causal_mask(c)[None, :, :, None, None]

    def step(s, inp):
        qc, kc, vc, gc = inp
        bc = jnp.cumsum(gc, axis=1)
        o = jnp.einsum('bthk,bhkv->bthv', qc * jnp.exp(bc), s)
        dec = _masked_exp(mask, bc[:, :, None] - bc[:, None])
        a = jnp.einsum('bthk,bshk,btshk->btsh', qc, kc, dec)
        o = o + jnp.einsum('btsh,bshv->bthv', a, vc)
        last = bc[:, -1]
        k_dec = kc * jnp.exp(last[:, None] - bc)
        s_new = jnp.exp(last)[..., None] * s + jnp.einsum('bshk,bshv->bhkv', k_dec, vc)
        return s_new, o

    s_fin, o = lax.scan(step, s0, tuple(_to_chunks(t, c) for t in (q, k, v, log_f)))
    return _from_chunks(o), s_fin


def _retention_scan(q, k, v, s0):
    c = math.gcd(q.shape[1], RET_CHUNK)
    log_gamma = jnp.log(1.0 - 2.0 ** (-5.0 - jnp.arange(RET_HEADS, dtype=F32)))
    t_idx = jnp.arange(c, dtype=F32)
    rel = t_idx[:, None] - t_idx[None, :]
    intra = _masked_exp(_causal_mask(c)[..., None], rel[..., None] * log_gamma)
    inter = jnp.exp((t_idx[:, None] + 1.0) * log_gamma)
    to_end = jnp.exp((c - 1.0 - t_idx[:, None]) * log_gamma)
    chunk_dec = jnp.exp(c * log_gamma)

    def step(s, inp):
        qc, kc, vc = inp
        o = jnp.einsum('bthk,bhkv->bthv', qc * inter[None, :, :, None], s)
        a = jnp.einsum('bthk,bshk->btsh', qc, kc) * intra[None]
        o = o + jnp.einsum('btsh,bshv->bthv', a, vc)
        s_new = chunk_dec[:, None, None] * s + jnp.einsum('bshk,bshv->bhkv', kc * to_end[None, :, :, None], vc)
        return s_new, o

    s_fin, o = lax.scan(step, s0, tuple(_to_chunks(t, c) for t in (q, k, v)))
    return _from_chunks(o), s_fin


def _mlstm_scan(q, k, v, i_pre, log_f, c0, n0, m0):
    c = math.gcd(q.shape[1], ML_CHUNK)
    mask = _causal_mask(c)[None, :, :, None]

    def step(carry, inp):
        cs, ns, ms = carry
        qc, kc, vc, ic, fc = inp
        b = jnp.cumsum(fc, axis=1)
        log_w = jnp.where(mask, b[:, :, None] - b[:, None] + ic[:, None], NEG_LARGE)
        log_inter = ms[:, None] + b
        m = jnp.maximum(log_inter, jnp.max(log_w, axis=2))
        w = _masked_exp(mask, log_w - m[:, :, None])
        sc = jnp.exp(log_inter - m)
        qk = jnp.einsum('bthk,bshk->btsh', qc, kc) * w
        num = jnp.einsum('btsh,bshv->bthv', qk, vc) + sc[..., None] * jnp.einsum('bthk,bhkv->bthv', qc, cs)
        den = jnp.sum(qk, axis=2) + sc * jnp.einsum('bthk,bhk->bth', qc, ns)
        h = num / jnp.maximum(jnp.abs(den), jnp.exp(-m))[..., None]
        m_last = m[:, -1]
        dec_state = jnp.exp(ms + b[:, -1] - m_last)
        kw = kc * jnp.exp(b[:, -1:] - b + ic - m_last[:, None])[..., None]
        cs_new = dec_state[..., None, None] * cs + jnp.einsum('bshk,bshv->bhkv', kw, vc)
        ns_new = dec_state[..., None] * ns + jnp.sum(kw, axis=1)
        return (cs_new, ns_new, m_last), h

    (c_fin, n_fin, m_fin), h = lax.scan(step, (c0, n0, m0), tuple(_to_chunks(t, c) for t in (q, k, v, i_pre, log_f)))
    return _from_chunks(h), c_fin, n_fin, m_fin


def _rotary(x, pos):
    d = x.shape[-1]
    theta = 1.0 / (ROPE_BASE ** jnp.linspace(0.0, 1.0, d // 2, dtype=F32))
    ang = pos.astype(F32)[:, None] * theta[None]
    cos = jnp.cos(ang)[None, :, None]
    sin = jnp.sin(ang)[None, :, None]
    x1 = x[..., 0::2]
    x2 = x[..., 1::2]
    return jnp.stack([x1 * cos - x2 * sin, x1 * sin + x2 * cos], axis=-1).reshape(x.shape)


def _head_rmsnorm(o, gain):
    b, l, h, d = o.shape
    y = o * lax.rsqrt(jnp.mean(o * o, axis=-1, keepdims=True) + NORM_EPS)
    return y.reshape(b, l, h * d) * gain.astype(F32)


def _head_layernorm(o, gain):
    b, l, h, d = o.shape
    mu = jnp.mean(o, axis=-1, keepdims=True)
    var = jnp.mean(jnp.square(o - mu), axis=-1, keepdims=True)
    y = (o - mu) * lax.rsqrt(var + NORM_EPS)
    return y.reshape(b, l, h * d) * gain.astype(F32)


def _layer(x, pos, state, params, lower_bound):
    hg_s, ret_s, ml_c, ml_n, ml_m = state
    w_in, hg_gain, ret_gain, ml_gain, b_i, b_f, w_br, w_o, ln_g, ln_b = params
    bsz, seq_len, _ = x.shape
    (hq, hf, hi, hz, rq, rk, rv, rz, mq, mk, mv, mz, mo, mi, mf, gates) = jnp.split(x @ w_in, SPLIT_POINTS, axis=-1)

    def heads(a, n):
        return a.astype(F32).reshape(bsz, seq_len, n, -1)

    lb = lower_bound.astype(F32)
    log_f = jnp.logaddexp(jnp.log(jnp.maximum(lb, HG_LB_FLOOR)),
                          jnp.log1p(-lb) + jax.nn.log_sigmoid(hf.astype(F32)))
    o_a, hg_new = _hgrn2_scan(heads(jax.nn.silu(hq.astype(F32)), HG_HEADS), heads(-jnp.expm1(log_f), HG_HEADS),
                              heads(hi, HG_HEADS), heads(log_f, HG_HEADS), hg_s.astype(F32))
    y_a = _head_rmsnorm(o_a, hg_gain) * jax.nn.silu(hz.astype(F32))

    q_r = _rotary(heads(rq, RET_HEADS), pos)
    k_r = _rotary(heads(rk, RET_HEADS), pos) * (RET_DK ** -0.5)
    o_b, ret_new = _retention_scan(q_r, k_r, heads(rv, RET_HEADS), ret_s.astype(F32))
    y_b = _head_layernorm(o_b, ret_gain) * jax.nn.silu(rz.astype(F32))

    i_pre = mi.astype(F32) + b_i.astype(F32)
    log_fg = jax.nn.log_sigmoid(mf.astype(F32) + b_f.astype(F32))
    h_c, mc_new, mn_new, mm_new = _mlstm_scan(heads(mq, ML_HEADS), heads(mk, ML_HEADS) * (ML_DK ** -0.5),
                                              heads(mv, ML_HEADS), i_pre, log_fg,
                                              ml_c.astype(F32), ml_n.astype(F32), ml_m.astype(F32))
    h_c = jax.nn.sigmoid(heads(mo, ML_HEADS)) * h_c
    y_c = _head_layernorm(h_c, ml_gain) * jax.nn.silu(mz.astype(F32))

    ys = jnp.stack([y_a, y_b, y_c], axis=2).astype(x.dtype)
    branch = jnp.einsum('blnc,ncd->blnd', ys, w_br)
    merged = jnp.sum(jax.nn.sigmoid(gates.reshape(bsz, seq_len, N_BRANCH, D_MODEL)) * branch, axis=2)
    out = merged @ w_o

    h = DEEPNORM_ALPHA * x.astype(F32) + out.astype(F32)
    mu = jnp.mean(h, axis=-1, keepdims=True)
    var = jnp.mean(jnp.square(h - mu), axis=-1, keepdims=True)
    x_new = (h - mu) * lax.rsqrt(var + LN_EPS) * ln_g.astype(F32) + ln_b.astype(F32)
    return x_new.astype(x.dtype), (hg_new, ret_new, mc_new, mn_new, mm_new)


def _run_group(x, pos, states, lower_bounds, params):
    new_states = []
    for layer in range(DEPTH):
        layer_params = tuple(p[layer] for p in params)
        layer_state = tuple(s[layer] for s in states)
        x, st = _layer(x, pos, layer_state, layer_params, lower_bounds[layer])
        new_states.append(st)
    stacked = tuple(jnp.stack(group) for group in zip(*new_states))
    return x, stacked


def setup_inputs(seed: int = 0) -> dict:
    key = jax.random.key(seed)
    ks = jax.random.split(key, 20)
    nrm = jax.random.normal
    x_prompt = nrm(ks[0], (BATCH, SEQ, D_MODEL), F32)
    x_sample = nrm(ks[1], (DEC_BATCH, DEC_SEQ, D_MODEL), F32)
    state_hgrn = 0.5 * nrm(ks[2], (DEPTH, DEC_BATCH, HG_HEADS, HG_DK, HG_DV), F32)
    state_ret = 0.5 * nrm(ks[3], (DEPTH, DEC_BATCH, RET_HEADS, RET_DK, RET_DV), F32)
    state_mlstm_c = 0.5 * nrm(ks[4], (DEPTH, DEC_BATCH, ML_HEADS, ML_DK, ML_DV), F32)
    state_mlstm_n = 0.5 * nrm(ks[5], (DEPTH, DEC_BATCH, ML_HEADS, ML_DK), F32)
    state_mlstm_m = 4.0 * jax.random.uniform(ks[6], (DEPTH, DEC_BATCH, ML_HEADS), F32)
    w_in = nrm(ks[7], (DEPTH, D_MODEL, D_IN), F32) * (D_MODEL ** -0.5)
    hgrn_lb_logits = 1.0 + 0.1 * nrm(ks[8], (DEPTH, BRANCH_WIDTH), F32)
    hgrn_norm = 1.0 + 0.02 * nrm(ks[9], (DEPTH, BRANCH_WIDTH), F32)
    ret_norm = 1.0 + 0.02 * nrm(ks[10], (DEPTH, BRANCH_WIDTH), F32)
    mlstm_norm = 1.0 + 0.02 * nrm(ks[11], (DEPTH, BRANCH_WIDTH), F32)
    mlstm_b_i = 0.1 * nrm(ks[12], (DEPTH, ML_HEADS), F32)
    mlstm_b_f = (jnp.linspace(ML_FGATE_BIAS_LO, ML_FGATE_BIAS_HI, ML_HEADS, dtype=F32)[None]
                 + 0.1 * nrm(ks[13], (DEPTH, ML_HEADS), F32))
    w_branch = nrm(ks[14], (DEPTH, N_BRANCH, BRANCH_WIDTH, D_MODEL), F32) * (BRANCH_WIDTH ** -0.5) * DEEPNORM_BETA
    w_out = nrm(ks[15], (DEPTH, D_MODEL, D_MODEL), F32) * (D_MODEL ** -0.5) * DEEPNORM_BETA
    ln_g = 1.0 + 0.02 * nrm(ks[16], (DEPTH, D_MODEL), F32)
    ln_b = 0.02 * nrm(ks[17], (DEPTH, D_MODEL), F32)
    return {'x_prompt': x_prompt, 'x_sample': x_sample,
            'state_hgrn': state_hgrn, 'state_ret': state_ret, 'state_mlstm_c': state_mlstm_c,
            'state_mlstm_n': state_mlstm_n, 'state_mlstm_m': state_mlstm_m,
            'w_in': w_in, 'hgrn_lb_logits': hgrn_lb_logits, 'hgrn_norm': hgrn_norm, 'ret_norm': ret_norm,
            'mlstm_norm': mlstm_norm, 'mlstm_b_i': mlstm_b_i, 'mlstm_b_f': mlstm_b_f,
            'w_branch': w_branch, 'w_out': w_out, 'ln_g': ln_g, 'ln_b': ln_b}


def reference(x_prompt, x_sample, state_hgrn, state_ret, state_mlstm_c, state_mlstm_n, state_mlstm_m,
              w_in, hgrn_lb_logits, hgrn_norm, ret_norm, mlstm_norm, mlstm_b_i, mlstm_b_f,
              w_branch, w_out, ln_g, ln_b):
    lb_sm = jax.nn.softmax(hgrn_lb_logits.astype(F32), axis=0)
    lower_bounds = jnp.cumsum(lb_sm, axis=0) - lb_sm[0]
    params = (w_in, hgrn_norm, ret_norm, mlstm_norm, mlstm_b_i, mlstm_b_f, w_branch, w_out, ln_g, ln_b)

    bp, lp = x_prompt.shape[0], x_prompt.shape[1]
    zero_states = (jnp.zeros((DEPTH, bp, HG_HEADS, HG_DK, HG_DV), F32),
                   jnp.zeros((DEPTH, bp, RET_HEADS, RET_DK, RET_DV), F32),
                   jnp.zeros((DEPTH, bp, ML_HEADS, ML_DK, ML_DV), F32),
                   jnp.zeros((DEPTH, bp, ML_HEADS, ML_DK), F32),
                   jnp.zeros((DEPTH, bp, ML_HEADS), F32))
    y_prompt, (hg_p, ret_p, mc_p, mn_p, mm_p) = _run_group(
        x_prompt, jnp.arange(lp), zero_states, lower_bounds, params)

    pos_sample = PAST_LEN + jnp.arange(x_sample.shape[1])
    y_sample, (hg_s, ret_s, mc_s, mn_s, mm_s) = _run_group(
        x_sample, pos_sample, (state_hgrn, state_ret, state_mlstm_c, state_mlstm_n, state_mlstm_m),
        lower_bounds, params)

    return (y_prompt, y_sample, hg_p, ret_p, mc_p, mn_p, mm_p, hg_s, ret_s, mc_s, mn_s, mm_s)
```

```python
import functools
import math

import jax
import jax.numpy as jnp
import numpy as np
from jax import lax
from jax.experimental import pallas as pl
from jax.experimental.pallas import tpu as pltpu

F32 = jnp.float32
BF16 = jnp.bfloat16
HIGHEST = lax.Precision.HIGHEST

D_MODEL = 2048
DEPTH = 4
PAST_LEN = 16384
BW = D_MODEL // 2
HG_HEADS, HG_D = 8, 128
RET_HEADS, RET_D = 4, 256
ML_HEADS, ML_D = 4, 256
HG_LB_FLOOR = 1e-30
ROPE_BASE = 10000.0
NEG_LARGE = -1e30
NORM_EPS = 1e-6
LN_EPS = 1e-5
DEEPNORM_ALPHA = (2 * DEPTH) ** 0.25

_HQ, _HF, _HI, _HZ, _RQ, _RK, _RV, _RZ, _MQ, _MK, _MV, _MZ, _MO = range(13)
_PRE_ORDER = (_HF, _HQ, _HZ, _RZ, _MZ, _MO, _RQ, _RK, _HI, _RV, _MQ, _MK, _MV)
PB_HQ, PB_HZ, PB_RZ, PB_MZ, PB_MO, PB_RQ, PB_RK, PB_HI, PB_RV, PB_MQ, PB_MK, PB_MV = range(12)
N_PB = 12
IF_COL0 = 13 * BW
GATE_COL0 = 13 * BW + 2 * ML_HEADS

V7X_VMEM_LIMIT = 56 * 1024 * 1024


def _cparams(sem):
    return pltpu.CompilerParams(dimension_semantics=sem, vmem_limit_bytes=V7X_VMEM_LIMIT)


def _sigmoid(x):
    return 1.0 / (1.0 + jnp.exp(-x))


def _silu(x):
    return x * _sigmoid(x)


def _log_sigmoid(x):
    return jnp.minimum(x, 0.0) - jnp.log1p(jnp.exp(-jnp.abs(x)))


def _iota(shape, dim):
    return lax.broadcasted_iota(jnp.int32, shape, dim)


def _dot(a, b):
    return jnp.dot(a, b, preferred_element_type=F32)


def _dot_t(a, b, precision=None):
    return lax.dot_general(a, b, (((1,), (1,)), ((), ())), preferred_element_type=F32, precision=precision)


def _lb_kernel(logit_ref, lb_ref):
    z = logit_ref[...]
    rows = [z[l:l + 1] for l in range(DEPTH)]
    mx = functools.reduce(jnp.maximum, rows)
    ex = [jnp.exp(r - mx) for r in rows]
    tot = functools.reduce(lambda a, b: a + b, ex)
    sm = [e / tot for e in ex]
    run = sm[0]
    lb_ref[0:1, :] = run - sm[0]
    for l in range(1, DEPTH):
        run = run + sm[l]
        lb_ref[l:l + 1, :] = run - sm[0]


def _lower_bounds(logits):
    return pl.pallas_call(_lb_kernel, out_shape=jax.ShapeDtypeStruct(logits.shape, F32), name="lower_bounds")(logits)


def _pre_kernel(x_ref, w_ref, wif_ref, bif_ref, lb_ref, cos_ref, sin_ref, pb_ref, g_ref, gt_ref, xb_ref, *, tpg):
    j = pl.program_id(1)
    tn = w_ref.shape[1]

    @pl.when(j == 0)
    def _():
        xb = x_ref[...].astype(BF16)
        xb_ref[...] = xb
        pt = _dot(xb, wif_ref[...]).T[0:2 * ML_HEADS, :] + bif_ref[...]
        row = _iota(pt.shape, 0)
        gt_ref[...] = jnp.where(row < ML_HEADS, pt, _log_sigmoid(pt))

    acc = _dot(xb_ref[...], w_ref[...])
    grp = j // tpg

    @pl.when(grp == 0)
    def _():
        lb = lb_ref[...]
        a = jnp.log(jnp.maximum(lb, HG_LB_FLOOR))
        b = jnp.log1p(-lb) + _log_sigmoid(acc)
        g_ref[...] = jnp.maximum(a, b) + jnp.log1p(jnp.exp(-jnp.abs(a - b)))

    @pl.when((grp >= 1 + PB_HQ) & (grp <= 1 + PB_MZ))
    def _():
        pb_ref[...] = _silu(acc).astype(BF16)

    @pl.when(grp == 1 + PB_MO)
    def _():
        pb_ref[...] = _sigmoid(acc).astype(BF16)

    @pl.when((grp == 1 + PB_RQ) | (grp == 1 + PB_RK))
    def _():
        even = (_iota(acc.shape, 1) % 2) == 0
        swapped = jnp.where(even, pltpu.roll(acc, tn - 1, 1), pltpu.roll(acc, 1, 1))
        rot = acc * cos_ref[...] + swapped * sin_ref[...]
        scale = jnp.where(grp == 1 + PB_RK, RET_D ** -0.5, 1.0).astype(F32)
        pb_ref[...] = (rot * scale).astype(BF16)

    @pl.when(grp >= 1 + PB_HI)
    def _():
        scale = jnp.where(grp == 1 + PB_MK, ML_D ** -0.5, 1.0).astype(F32)
        pb_ref[...] = (acc * scale).astype(BF16)


def _pre(x, w_main, w_if, b_if, lb, cos_t, sin_t, *, tm, tn):
    t = x.shape[0]
    tpg = BW // tn
    nj = 13 * tpg
    n_pos_blocks = cos_t.shape[0] // tm
    kern = functools.partial(_pre_kernel, tpg=tpg)
    return pl.pallas_call(
        kern,
        grid=(t // tm, nj),
        in_specs=[
            pl.BlockSpec((tm, D_MODEL), lambda i, j: (i, 0)),
            pl.BlockSpec((D_MODEL, tn), lambda i, j: (0, j)),
            pl.BlockSpec((D_MODEL, 128), lambda i, j: (0, 0)),
            pl.BlockSpec((2 * ML_HEADS, 1), lambda i, j: (0, 0)),
            pl.BlockSpec((1, tn), lambda i, j: (0, jnp.minimum(j, tpg - 1))),
            pl.BlockSpec((tm, tn), lambda i, j: (i % n_pos_blocks, 0)),
            pl.BlockSpec((tm, tn), lambda i, j: (i % n_pos_blocks, 0)),
        ],
        out_specs=[
            pl.BlockSpec((tm, tn), lambda i, j: (i, jnp.maximum(j - tpg, 0))),
            pl.BlockSpec((tm, tn), lambda i, j: (i, jnp.minimum(j, tpg - 1))),
            pl.BlockSpec((2 * ML_HEADS, tm), lambda i, j: (0, i)),
        ],
        out_shape=[
            jax.ShapeDtypeStruct((t, N_PB * BW), BF16),
            jax.ShapeDtypeStruct((t, BW), F32),
            jax.ShapeDtypeStruct((2 * ML_HEADS, t), F32),
        ],
        scratch_shapes=[pltpu.VMEM((tm, D_MODEL), BF16)],
        compiler_params=_cparams(("arbitrary", "arbitrary")),
        name="pre",
    )(x, w_main, w_if, b_if, lb, cos_t, sin_t)


def _rms_gate(o, gain, zs):
    y = o * lax.rsqrt(jnp.mean(o * o, axis=-1, keepdims=True) + NORM_EPS)
    return (y * gain * zs).astype(BF16)


def _ln_gate(o, gain, zs):
    mu = jnp.mean(o, axis=-1, keepdims=True)
    d = o - mu
    var = jnp.mean(d * d, axis=-1, keepdims=True)
    return (d * lax.rsqrt(var + NORM_EPS) * gain * zs).astype(BF16)


HG_SUB = 16


def _hgrn_kernel(q_ref, g_ref, v_ref, z_ref, gain_ref, y_ref, s_ref):
    c = pl.program_id(1)
    cs = q_ref.shape[0]

    @pl.when(c == 0)
    def _():
        s_ref[...] = jnp.zeros_like(s_ref)

    tril = (_iota((cs, cs), 0) >= _iota((cs, cs), 1)).astype(F32)
    eye_d = _iota((HG_D, HG_D), 0) == _iota((HG_D, HG_D), 1)
    sub_row = _iota((HG_SUB, 1), 0)
    half_row = _iota((HG_SUB // 2, 1), 0) + HG_SUB // 2
    for h in range(HG_HEADS):
        sl = slice(h * HG_D, (h + 1) * HG_D)
        q = q_ref[:, sl].astype(F32)
        g = g_ref[:, sl]
        v = v_ref[:, sl]
        vf = v.astype(F32)
        kk = 1.0 - jnp.exp(g)
        bc = jnp.dot(tril, g, preferred_element_type=F32, precision=HIGHEST)
        s_old = s_ref[0, h]
        o_inter = _dot((q * jnp.exp(bc)).astype(BF16), s_old.astype(BF16))
        outs = []
        for i in range(cs // HG_SUB):
            r0 = i * HG_SUB
            bc_i = bc[r0:r0 + HG_SUB]
            q_i = q[r0:r0 + HG_SUB]
            acc = o_inter[r0:r0 + HG_SUB]
            if i > 0:
                ref_i = bc[r0 - 1:r0]
                qs = (q_i * jnp.exp(bc_i - ref_i)).astype(BF16)
                ks = (kk[:r0] * jnp.exp(ref_i - bc[:r0])).astype(BF16)
                a_off = _dot_t(qs, ks)
                acc = acc + _dot(a_off.astype(BF16), v[:r0])
            lo = [acc[:HG_SUB // 2], acc[HG_SUB // 2:]]
            for s in range(HG_SUB):
                b_s = bc[r0 + s:r0 + s + 1]
                k_s = kk[r0 + s:r0 + s + 1]
                v_s = vf[r0 + s:r0 + s + 1]
                if s < HG_SUB // 2:
                    rows, qq, bb, dst = sub_row, q_i, bc_i, None
                else:
                    rows, qq, bb, dst = half_row, q_i[HG_SUB // 2:], bc_i[HG_SUB // 2:], 1
                m = rows >= s
                e = jnp.exp(jnp.where(m, bb - b_s, 0.0))
                p = jnp.where(m, qq * (k_s * e), 0.0)
                contrib = jnp.sum(p, axis=1, keepdims=True) * v_s
                if dst is None:
                    lo = [lo[0] + contrib[:HG_SUB // 2], lo[1] + contrib[HG_SUB // 2:]]
                else:
                    lo[1] = lo[1] + contrib
            outs.extend(lo)
        o = jnp.concatenate(outs, axis=0)
        y_ref[:, sl] = _rms_gate(o, gain_ref[:, sl], z_ref[:, sl].astype(F32))
        last = bc[cs - 1:cs]
        k_dec = (kk * jnp.exp(last - bc)).astype(F32)
        upd = _dot(k_dec.T.astype(BF16), v)
        dec_col = jnp.sum(jnp.where(eye_d, jnp.exp(last), 0.0), axis=1, keepdims=True)
        s_ref[0, h] = dec_col * s_old + upd


def _hgrn_prompt(pb, g, gain, *, bsz, seq, cs):
    nc = seq // cs
    tok = lambda b, c: b * nc + c
    return pl.pallas_call(
        _hgrn_kernel,
        grid=(bsz, nc),
        in_specs=[
            pl.BlockSpec((cs, BW), lambda b, c: (tok(b, c), PB_HQ)),
            pl.BlockSpec((cs, BW), lambda b, c: (tok(b, c), 0)),
            pl.BlockSpec((cs, BW), lambda b, c: (tok(b, c), PB_HI)),
            pl.BlockSpec((cs, BW), lambda b, c: (tok(b, c), PB_HZ)),
            pl.BlockSpec((1, BW), lambda b, c: (0, 0)),
        ],
        out_specs=[
            pl.BlockSpec((cs, BW), lambda b, c: (tok(b, c), 0)),
            pl.BlockSpec((1, HG_HEADS, HG_D, HG_D), lambda b, c: (b, 0, 0, 0)),
        ],
        out_shape=[
            jax.ShapeDtypeStruct((bsz * seq, BW), BF16),
            jax.ShapeDtypeStruct((bsz, HG_HEADS, HG_D, HG_D), F32),
        ],
        compiler_params=_cparams(("arbitrary", "arbitrary")),
        name="hgrn_prompt",
    )(pb, g, pb, pb, gain)


def _ret_log_gamma(h):
    return jnp.log(jnp.full((1, 1), 1.0 - 2.0 ** (-5.0 - h), F32))


def _ret_kernel(q_ref, k_ref, v_ref, z_ref, gain_ref, y_ref, s_ref):
    c = pl.program_id(1)
    cs = q_ref.shape[0]

    @pl.when(c == 0)
    def _():
        s_ref[...] = jnp.zeros_like(s_ref)

    t_col = _iota((cs, 1), 0).astype(F32)
    rel = (_iota((cs, cs), 0) - _iota((cs, cs), 1)).astype(F32)
    causal = rel >= 0.0
    for h in range(RET_HEADS):
        sl = slice(h * RET_D, (h + 1) * RET_D)
        lg = _ret_log_gamma(h)
        q = q_ref[:, sl]
        k = k_ref[:, sl]
        v = v_ref[:, sl]
        s_old = s_ref[0, h]
        intra = jnp.where(causal, jnp.exp(jnp.where(causal, rel * lg, 0.0)), 0.0)
        inter = jnp.exp((t_col + 1.0) * lg)
        to_end = jnp.exp((cs - 1.0 - t_col) * lg)
        a = _dot_t(q, k) * intra
        o = inter * _dot(q, s_old.astype(BF16)) + _dot(a.astype(BF16), v)
        y_ref[:, sl] = _ln_gate(o, gain_ref[:, sl], z_ref[:, sl].astype(F32))
        k_dec = k.astype(F32) * to_end
        s_ref[0, h] = jnp.exp(cs * lg) * s_old + _dot(k_dec.T.astype(BF16), v)


def _ret_prompt(pb, gain, *, bsz, seq, cs):
    nc = seq // cs
    tok = lambda b, c: b * nc + c
    return pl.pallas_call(
        _ret_kernel,
        grid=(bsz, nc),
        in_specs=[
            pl.BlockSpec((cs, BW), lambda b, c: (tok(b, c), PB_RQ)),
            pl.BlockSpec((cs, BW), lambda b, c: (tok(b, c), PB_RK)),
            pl.BlockSpec((cs, BW), lambda b, c: (tok(b, c), PB_RV)),
            pl.BlockSpec((cs, BW), lambda b, c: (tok(b, c), PB_RZ)),
            pl.BlockSpec((1, BW), lambda b, c: (0, 0)),
        ],
        out_specs=[
            pl.BlockSpec((cs, BW), lambda b, c: (tok(b, c), 0)),
            pl.BlockSpec((1, RET_HEADS, RET_D, RET_D), lambda b, c: (b, 0, 0, 0)),
        ],
        out_shape=[
            jax.ShapeDtypeStruct((bsz * seq, BW), BF16),
            jax.ShapeDtypeStruct((bsz, RET_HEADS, RET_D, RET_D), F32),
        ],
        compiler_params=_cparams(("arbitrary", "arbitrary")),
        name="ret_prompt",
    )(pb, pb, pb, pb, gain)


def _mlstm_kernel(q_ref, k_ref, v_ref, z_ref, og_ref, gt_ref, gain_ref, y_ref, c_ref, n_ref, m_ref):
    c = pl.program_id(1)
    cs = q_ref.shape[0]

    @pl.when(c == 0)
    def _():
        c_ref[...] = jnp.zeros_like(c_ref)
        n_ref[...] = jnp.zeros_like(n_ref)
        m_ref[...] = jnp.zeros_like(m_ref)

    row = _iota((cs, cs), 0)
    col = _iota((cs, cs), 1)
    causal = row >= col
    tril = causal.astype(F32)
    triu = (row <= col).astype(F32)
    eye = (row == col).astype(F32)
    gi = gt_ref[0:ML_HEADS, :]
    gf = gt_ref[ML_HEADS:2 * ML_HEADS, :]
    b_rows = jnp.dot(gf, triu, preferred_element_type=F32, precision=HIGHEST)
    b_cols = _dot_t(tril, gf, precision=HIGHEST)
    i_cols = _dot_t(eye, gi, precision=HIGHEST)
    for h in range(ML_HEADS):
        sl = slice(h * ML_D, (h + 1) * ML_D)
        q = q_ref[:, sl]
        k = k_ref[:, sl]
        v = v_ref[:, sl]
        c_old = c_ref[0, h]
        n_old = n_ref[0, :, sl]
        m_old = m_ref[0, :, h * 128:h * 128 + 1]
        b_col = b_cols[:, h:h + 1]
        i_col = i_cols[:, h:h + 1]
        b_row = b_rows[h:h + 1, :]
        i_row = gi[h:h + 1, :]
        log_w = jnp.where(causal, b_col - b_row + i_row, NEG_LARGE)
        log_inter = m_old + b_col
        m = jnp.maximum(log_inter, jnp.max(log_w, axis=1, keepdims=True))
        w = jnp.where(causal, jnp.exp(jnp.where(causal, log_w - m, 0.0)), 0.0)
        sc = jnp.exp(log_inter - m)
        qk = _dot_t(q, k) * w
        num = _dot(qk.astype(BF16), v) + sc * _dot(q, c_old.astype(BF16))
        den = jnp.sum(qk, axis=1, keepdims=True) + sc * jnp.sum(q.astype(F32) * n_old, axis=1, keepdims=True)
        hid = num / jnp.maximum(jnp.abs(den), jnp.exp(-m))
        hid = og_ref[:, sl].astype(F32) * hid
        y_ref[:, sl] = _ln_gate(hid, gain_ref[:, sl], z_ref[:, sl].astype(F32))
        m_last = m[cs - 1:cs]
        b_last = b_col[cs - 1:cs]
        dec = jnp.exp(m_old + b_last - m_last)
        kw = k.astype(F32) * jnp.exp(b_last - b_col + i_col - m_last)
        c_ref[0, h] = dec * c_old + _dot(kw.T.astype(BF16), v)
        n_ref[0, :, sl] = dec * n_old + jnp.sum(kw, axis=0, keepdims=True)
        m_ref[0, :, h * 128:(h + 1) * 128] = jnp.broadcast_to(m_last, (1, 128))


def _mlstm_prompt(pb, gt, gain, *, bsz, seq, cs):
    nc = seq // cs
    tok = lambda b, c: b * nc + c
    return pl.pallas_call(
        _mlstm_kernel,
        grid=(bsz, nc),
        in_specs=[
            pl.BlockSpec((cs, BW), lambda b, c: (tok(b, c), PB_MQ)),
            pl.BlockSpec((cs, BW), lambda b, c: (tok(b, c), PB_MK)),
            pl.BlockSpec((cs, BW), lambda b, c: (tok(b, c), PB_MV)),
            pl.BlockSpec((cs, BW), lambda b, c: (tok(b, c), PB_MZ)),
            pl.BlockSpec((cs, BW), lambda b, c: (tok(b, c), PB_MO)),
            pl.BlockSpec((2 * ML_HEADS, cs), lambda b, c: (0, tok(b, c))),
            pl.BlockSpec((1, BW), lambda b, c: (0, 0)),
        ],
        out_specs=[
            pl.BlockSpec((cs, BW), lambda b, c: (tok(b, c), 0)),
            pl.BlockSpec((1, ML_HEADS, ML_D, ML_D), lambda b, c: (b, 0, 0, 0)),
            pl.BlockSpec((1, 1, BW), lambda b, c: (b, 0, 0)),
            pl.BlockSpec((1, 1, ML_HEADS * 128), lambda b, c: (b, 0, 0)),
        ],
        out_shape=[
            jax.ShapeDtypeStruct((bsz * seq, BW), BF16),
            jax.ShapeDtypeStruct((bsz, ML_HEADS, ML_D, ML_D), F32),
            jax.ShapeDtypeStruct((bsz, 1, BW), F32),
            jax.ShapeDtypeStruct((bsz, 1, ML_HEADS * 128), F32),
        ],
        compiler_params=_cparams(("arbitrary", "arbitrary")),
        name="mlstm_prompt",
    )(pb, pb, pb, pb, pb, gt, gain)


def _post_kernel(x_ref, ya_ref, yb_ref, yc_ref, wg_ref, wbr_ref, wo_ref, lng_ref, lnb_ref, o_ref, acc_ref, xb_ref):
    j = pl.program_id(1)

    @pl.when(j == 0)
    def _():
        xb_ref[...] = x_ref[...].astype(BF16)
        acc_ref[...] = jnp.zeros_like(acc_ref)

    xb = xb_ref[...]
    merged = None
    for n, y_ref in enumerate((ya_ref, yb_ref, yc_ref)):
        term = _sigmoid(_dot(xb, wg_ref[n])) * _dot(y_ref[...], wbr_ref[n])
        merged = term if merged is None else merged + term
    acc_ref[...] += _dot(merged.astype(BF16), wo_ref[...])

    @pl.when(j == pl.num_programs(1) - 1)
    def _():
        hres = DEEPNORM_ALPHA * x_ref[...] + acc_ref[...]
        mu = jnp.mean(hres, axis=-1, keepdims=True)
        d = hres - mu
        var = jnp.mean(d * d, axis=-1, keepdims=True)
        o_ref[...] = d * lax.rsqrt(var + LN_EPS) * lng_ref[...] + lnb_ref[...]


def _post(x, ya, yb, yc, wg, wbr, wo, lng, lnb, *, tm, tn):
    t = x.shape[0]
    return pl.pallas_call(
        _post_kernel,
        grid=(t // tm, D_MODEL // tn),
        in_specs=[
            pl.BlockSpec((tm, D_MODEL), lambda i, j: (i, 0)),
            pl.BlockSpec((tm, BW), lambda i, j: (i, 0)),
            pl.BlockSpec((tm, BW), lambda i, j: (i, 0)),
            pl.BlockSpec((tm, BW), lambda i, j: (i, 0)),
            pl.BlockSpec((3, D_MODEL, tn), lambda i, j: (0, 0, j)),
            pl.BlockSpec((3, BW, tn), lambda i, j: (0, 0, j)),
            pl.BlockSpec((tn, D_MODEL), lambda i, j: (j, 0)),
            pl.BlockSpec((1, D_MODEL), lambda i, j: (0, 0)),
            pl.BlockSpec((1, D_MODEL), lambda i, j: (0, 0)),
        ],
        out_specs=pl.BlockSpec((tm, D_MODEL), lambda i, j: (i, 0)),
        out_shape=jax.ShapeDtypeStruct((t, D_MODEL), F32),
        scratch_shapes=[pltpu.VMEM((tm, D_MODEL), F32), pltpu.VMEM((tm, D_MODEL), BF16)],
        compiler_params=_cparams(("arbitrary", "arbitrary")),
        name="post",
    )(x, ya, yb, yc, wg, wbr, wo, lng, lnb)


DEC_NB = 8


def _columns(x):
    d = x.shape[1]
    eye = (_iota((d, d), 0) == _iota((d, d), 1)).astype(F32)
    return _dot_t(eye, x, precision=HIGHEST)


def _rank1_update(s_old, decay, k_cols, q_cols, v_rows, j):
    s_new = decay * s_old + k_cols[:, j:j + 1] * v_rows[j:j + 1, :]
    o = jnp.sum(q_cols[:, j:j + 1] * s_new, axis=0, keepdims=True)
    return s_new, o


def _hgrn_dec_kernel(q_ref, g_ref, v_ref, z_ref, gain_ref, s_ref, y_ref, so_ref, o_scr):
    g = g_ref[...]
    q_cols = _columns(q_ref[...].astype(F32))
    f = jnp.exp(g)
    f_cols = _columns(f)
    k_cols = _columns(1.0 - f)
    v_rows = v_ref[...].astype(F32)
    for j in range(DEC_NB):
        s_new, o = _rank1_update(s_ref[0, j, 0], f_cols[:, j:j + 1], k_cols, q_cols, v_rows, j)
        so_ref[0, j, 0] = s_new
        o_scr[j:j + 1, :] = o
    y_ref[...] = _rms_gate(o_scr[...], gain_ref[...], z_ref[...].astype(F32))


def _ret_dec_kernel(q_ref, k_ref, v_ref, z_ref, gain_ref, s_ref, y_ref, so_ref, o_scr):
    h = pl.program_id(1)
    gamma = jnp.float32(1.0 - 2.0 ** -5.0)
    for hh in range(1, RET_HEADS):
        gamma = jnp.where(h == hh, jnp.float32(1.0 - 2.0 ** (-5.0 - hh)), gamma)
    q_cols = _columns(q_ref[...].astype(F32))
    k_cols = _columns(k_ref[...].astype(F32))
    v_rows = v_ref[...].astype(F32)
    for j in range(DEC_NB):
        s_new, o = _rank1_update(s_ref[0, j, 0], gamma, k_cols, q_cols, v_rows, j)
        so_ref[0, j, 0] = s_new
        o_scr[j:j + 1, :] = o
    y_ref[...] = _ln_gate(o_scr[...], gain_ref[...], z_ref[...].astype(F32))


def _mlstm_gate_dec_kernel(gt_ref, m_ref, mo_ref, w_ref, sc_ref, eps_ref):
    gi = gt_ref[0:ML_HEADS, :]
    gf = gt_ref[ML_HEADS:2 * ML_HEADS, :]
    m_old = m_ref[...]
    log_inter = m_old + gf
    m_new = jnp.maximum(log_inter, gi)
    mo_ref[...] = m_new
    w_ref[...] = jnp.exp(gi - m_new)
    sc_ref[...] = jnp.exp(log_inter - m_new)
    eps_ref[...] = jnp.exp(-m_new)


def _mlstm_gate_dec(gt, m_t):
    shp = jax.ShapeDtypeStruct(m_t.shape, F32)
    return pl.pallas_call(_mlstm_gate_dec_kernel, out_shape=[shp, shp, shp, shp], name="mlstm_gate_dec")(gt, m_t)


def _mlstm_dec_kernel(w_sm, sc_sm, eps_sm, q_ref, k_ref, v_ref, z_ref, og_ref, gain_ref, n_ref, c_ref,
                      y_ref, no_ref, co_ref, o_scr, kw_scr, sc_scr, eps_scr, *, bsz):
    i = pl.program_id(0)
    h = pl.program_id(1)
    qf = q_ref[...].astype(F32)
    q_cols = _columns(qf)
    k_rows = k_ref[...].astype(F32)
    for j in range(DEC_NB):
        idx = h * bsz + i * DEC_NB + j
        kw_scr[j:j + 1, :] = k_rows[j:j + 1, :] * w_sm[idx]
        sc_scr[j:j + 1, :] = jnp.full((1, 128), sc_sm[idx], F32)
        eps_scr[j:j + 1, :] = jnp.full((1, 128), eps_sm[idx], F32)
    kw_rows = kw_scr[...]
    kw_cols = _columns(kw_rows)
    v_rows = v_ref[...].astype(F32)
    for j in range(DEC_NB):
        sc = sc_sm[h * bsz + i * DEC_NB + j]
        c_new, o = _rank1_update(c_ref[0, j, 0], sc, kw_cols, q_cols, v_rows, j)
        co_ref[0, j, 0] = c_new
        o_scr[j:j + 1, :] = o
    n_new = sc_scr[:, 0:1] * n_ref[...] + kw_rows
    no_ref[...] = n_new
    den = jnp.sum(qf * n_new, axis=1, keepdims=True)
    hid = o_scr[...] / jnp.maximum(jnp.abs(den), eps_scr[:, 0:1])
    hid = og_ref[...].astype(F32) * hid
    y_ref[...] = _ln_gate(hid, gain_ref[...], z_ref[...].astype(F32))


def _hgrn_dec(pb, g, gain, state, layer, prev_out):
    bsz = pb.shape[0]
    d, nh = HG_D, HG_HEADS
    state_spec = pl.BlockSpec((1, DEC_NB, 1, d, d), lambda i, h: (layer, i, h, 0, 0))
    in_specs = ([pl.BlockSpec((DEC_NB, d), lambda i, h: (i, PB_HQ * nh + h)),
                 pl.BlockSpec((DEC_NB, d), lambda i, h: (i, h)),
                 pl.BlockSpec((DEC_NB, d), lambda i, h: (i, PB_HI * nh + h)),
                 pl.BlockSpec((DEC_NB, d), lambda i, h: (i, PB_HZ * nh + h)),
                 pl.BlockSpec((1, d), lambda i, h: (0, h)),
                 state_spec])
    args = [pb, g, pb, pb, gain, state]
    aliases = {}
    if prev_out is not None:
        in_specs.append(pl.BlockSpec(memory_space=pl.ANY))
        args.append(prev_out)
        aliases = {len(args) - 1: 1}
    kern = _hgrn_dec_kernel if prev_out is None else (lambda *r: _hgrn_dec_kernel(*r[:6], *r[7:]))
    return pl.pallas_call(
        kern,
        grid=(bsz // DEC_NB, nh),
        in_specs=in_specs,
        out_specs=[pl.BlockSpec((DEC_NB, d), lambda i, h: (i, h)), state_spec],
        out_shape=[jax.ShapeDtypeStruct((bsz, BW), BF16), jax.ShapeDtypeStruct(state.shape, F32)],
        scratch_shapes=[pltpu.VMEM((DEC_NB, d), F32)],
        input_output_aliases=aliases,
        compiler_params=_cparams(("arbitrary", "arbitrary")),
        name="hgrn_dec",
    )(*args)


def _ret_dec(pb, gain, state, layer, prev_out):
    bsz = pb.shape[0]
    d, nh = RET_D, RET_HEADS
    state_spec = pl.BlockSpec((1, DEC_NB, 1, d, d), lambda i, h: (layer, i, h, 0, 0))
    in_specs = ([pl.BlockSpec((DEC_NB, d), lambda i, h: (i, PB_RQ * nh + h)),
                 pl.BlockSpec((DEC_NB, d), lambda i, h: (i, PB_RK * nh + h)),
                 pl.BlockSpec((DEC_NB, d), lambda i, h: (i, PB_RV * nh + h)),
                 pl.BlockSpec((DEC_NB, d), lambda i, h: (i, PB_RZ * nh + h)),
                 pl.BlockSpec((1, d), lambda i, h: (0, h)),
                 state_spec])
    args = [pb, pb, pb, pb, gain, state]
    aliases = {}
    if prev_out is not None:
        in_specs.append(pl.BlockSpec(memory_space=pl.ANY))
        args.append(prev_out)
        aliases = {len(args) - 1: 1}
    kern = _ret_dec_kernel if prev_out is None else (lambda *r: _ret_dec_kernel(*r[:6], *r[7:]))
    return pl.pallas_call(
        kern,
        grid=(bsz // DEC_NB, nh),
        in_specs=in_specs,
        out_specs=[pl.BlockSpec((DEC_NB, d), lambda i, h: (i, h)), state_spec],
        out_shape=[jax.ShapeDtypeStruct((bsz, BW), BF16), jax.ShapeDtypeStruct(state.shape, F32)],
        scratch_shapes=[pltpu.VMEM((DEC_NB, d), F32)],
        input_output_aliases=aliases,
        compiler_params=_cparams(("arbitrary", "arbitrary")),
        name="ret_dec",
    )(*args)


def _mlstm_dec(pb, gain, w_flat, sc_flat, eps_flat, n_state, c_state, layer, prev_out):
    bsz = pb.shape[0]
    d, nh = ML_D, ML_HEADS
    state_spec = pl.BlockSpec((1, DEC_NB, 1, d, d), lambda i, h, *_: (layer, i, h, 0, 0))
    row = lambda c: pl.BlockSpec((DEC_NB, d), lambda i, h, *_: (i, c * nh + h))
    in_specs = [row(PB_MQ), row(PB_MK), row(PB_MV), row(PB_MZ), row(PB_MO),
                pl.BlockSpec((1, d), lambda i, h, *_: (0, h)),
                pl.BlockSpec((DEC_NB, d), lambda i, h, *_: (i, h)),
                state_spec]
    args = [pb, pb, pb, pb, pb, gain, n_state, c_state]
    aliases = {}
    n_in = 8
    if prev_out is not None:
        in_specs.append(pl.BlockSpec(memory_space=pl.ANY))
        args.append(prev_out)
        aliases = {3 + n_in: 2}
    base = functools.partial(_mlstm_dec_kernel, bsz=bsz)
    kern = base if prev_out is None else (lambda *r: base(*r[:3 + n_in], *r[3 + n_in + 1:]))
    grid_spec = pltpu.PrefetchScalarGridSpec(
        num_scalar_prefetch=3,
        grid=(bsz // DEC_NB, nh),
        in_specs=in_specs,
        out_specs=[pl.BlockSpec((DEC_NB, d), lambda i, h, *_: (i, h)),
                   pl.BlockSpec((DEC_NB, d), lambda i, h, *_: (i, h)),
                   state_spec],
        scratch_shapes=[pltpu.VMEM((DEC_NB, d), F32), pltpu.VMEM((DEC_NB, d), F32),
                        pltpu.VMEM((DEC_NB, 128), F32), pltpu.VMEM((DEC_NB, 128), F32)],
    )
    return pl.pallas_call(
        kern,
        grid_spec=grid_spec,
        out_shape=[jax.ShapeDtypeStruct((bsz, BW), BF16), jax.ShapeDtypeStruct((bsz, BW), F32),
                   jax.ShapeDtypeStruct(c_state.shape, F32)],
        input_output_aliases=aliases,
        compiler_params=_cparams(("arbitrary", "arbitrary")),
        name="mlstm_dec",
    )(w_flat, sc_flat, eps_flat, *args)


def _rotary_tables(pos, width):
    theta = 1.0 / (ROPE_BASE ** jnp.linspace(0.0, 1.0, RET_D // 2, dtype=F32))
    ang = pos.astype(F32)[:, None] * theta[None]
    cos = jnp.repeat(jnp.cos(ang), 2, axis=1)
    sin = jnp.repeat(jnp.sin(ang), 2, axis=1)
    sign = jnp.tile(jnp.asarray([-1.0, 1.0], F32), RET_D // 2)
    reps = width // RET_D
    return jnp.tile(cos, (1, reps)), jnp.tile(sin * sign[None], (1, reps))


def _prep_weights(w_in, w_branch, w_out, mlstm_b_i, mlstm_b_f):
    w_main = jnp.concatenate([w_in[:, :, o * BW:(o + 1) * BW] for o in _PRE_ORDER], axis=-1).astype(BF16)
    w_if = jnp.pad(w_in[:, :, IF_COL0:GATE_COL0], ((0, 0), (0, 0), (0, 128 - 2 * ML_HEADS))).astype(BF16)
    b_if = jnp.concatenate([mlstm_b_i, mlstm_b_f], axis=-1).astype(F32)[:, :, None]
    w_gate = w_in[:, :, GATE_COL0:].reshape(DEPTH, D_MODEL, 3, D_MODEL).transpose(0, 2, 1, 3).astype(BF16)
    return w_main, w_if, b_if, w_gate, w_branch.astype(BF16), w_out.astype(BF16)


def _row(a):
    return a.astype(F32)[None, :]


def kernel(x_prompt, x_sample, state_hgrn, state_ret, state_mlstm_c, state_mlstm_n, state_mlstm_m,
           w_in, hgrn_lb_logits, hgrn_norm, ret_norm, mlstm_norm, mlstm_b_i, mlstm_b_f,
           w_branch, w_out, ln_g, ln_b):
    bp, lp, _ = x_prompt.shape
    bs, ls, _ = x_sample.shape
    assert ls == 1
    lbs = _lower_bounds(hgrn_lb_logits.astype(F32))
    w_main, w_if, b_if, w_gate, w_br, w_o = _prep_weights(w_in, w_branch, w_out, mlstm_b_i, mlstm_b_f)

    pre_tn = 512
    tm_p = min(1024, lp)
    cos_p, sin_p = _rotary_tables(jnp.arange(lp), pre_tn)
    cos_s, sin_s = _rotary_tables(jnp.full((bs,), PAST_LEN), pre_tn)

    x = x_prompt.reshape(bp * lp, D_MODEL)
    hg_l, ret_l, mc_l, mn_l, mm_l = [], [], [], [], []
    for l in range(DEPTH):
        pb, g, gt = _pre(x, w_main[l], w_if[l], b_if[l], lbs[l:l + 1], cos_p, sin_p, tm=tm_p, tn=pre_tn)
        ya, hg = _hgrn_prompt(pb, g, _row(hgrn_norm[l]), bsz=bp, seq=lp, cs=min(64, lp))
        yb, rs = _ret_prompt(pb, _row(ret_norm[l]), bsz=bp, seq=lp, cs=min(128, lp))
        yc, mc, mn, mm = _mlstm_prompt(pb, gt, _row(mlstm_norm[l]), bsz=bp, seq=lp, cs=min(128, lp))
        x = _post(x, ya, yb, yc, w_gate[l], w_br[l], w_o[l], _row(ln_g[l]), _row(ln_b[l]), tm=min(512, bp * lp), tn=256)
        hg_l.append(hg)
        ret_l.append(rs)
        mc_l.append(mc)
        mn_l.append(mn.reshape(bp, ML_HEADS, ML_D))
        mm_l.append(mm.reshape(bp, ML_HEADS, 128)[:, :, 0])
    y_prompt = x.reshape(bp, lp, D_MODEL)

    x = x_sample.reshape(bs, D_MODEL)
    hg_s = ret_s = mc_s = None
    mn_sl, mm_sl = [], []
    for l in range(DEPTH):
        pb, g, gt = _pre(x, w_main[l], w_if[l], b_if[l], lbs[l:l + 1], cos_s, sin_s, tm=bs, tn=pre_tn)
        ya, hg_s = _hgrn_dec(pb, g, _row(hgrn_norm[l]), state_hgrn, l, hg_s)
        yb, ret_s = _ret_dec(pb, _row(ret_norm[l]), state_ret, l, ret_s)
        m_new, w_t, sc_t, eps_t = _mlstm_gate_dec(gt, state_mlstm_m[l].T)
        yc, mn, mc_s = _mlstm_dec(pb, _row(mlstm_norm[l]), w_t.reshape(-1), sc_t.reshape(-1), eps_t.reshape(-1),
                                  state_mlstm_n[l].reshape(bs, BW), state_mlstm_c, l, mc_s)
        x = _post(x, ya, yb, yc, w_gate[l], w_br[l], w_o[l], _row(ln_g[l]), _row(ln_b[l]), tm=bs, tn=256)
        mn_sl.append(mn.reshape(bs, ML_HEADS, ML_D))
        mm_sl.append(m_new.T)
    y_sample = x.reshape(bs, 1, D_MODEL)

    return (y_prompt, y_sample,
            jnp.stack(hg_l), jnp.stack(ret_l), jnp.stack(mc_l), jnp.stack(mn_l), jnp.stack(mm_l),
            hg_s, ret_s, mc_s, jnp.stack(mn_sl), jnp.stack(mm_sl))
```

```python
import functools

import jax
import jax.numpy as jnp
from jax import lax
from jax.experimental import pallas as pl
from jax.experimental.pallas import tpu as pltpu

F32 = jnp.float32
BF16 = jnp.bfloat16
HIGHEST = lax.Precision.HIGHEST

D_MODEL = 2048
DEPTH = 4
PAST_LEN = 16384
BW = D_MODEL // 2
HG_HEADS, HG_D = 8, 128
RET_HEADS, RET_D = 4, 256
ML_HEADS, ML_D = 4, 256
HG_LB_FLOOR = 1e-30
ROPE_BASE = 10000.0
NEG_LARGE = -1e30
NORM_EPS = 1e-6
LN_EPS = 1e-5
DEEPNORM_ALPHA = (2 * DEPTH) ** 0.25

_HQ, _HF, _HI, _HZ, _RQ, _RK, _RV, _RZ, _MQ, _MK, _MV, _MZ, _MO = range(13)
SILU_BLOCKS = (_HQ, _HZ, _RZ, _MZ)
ROT_BLOCKS = (_RQ, _RK)
PLAIN_BLOCKS = (_HI, _RV, _MQ, _MK, _MV)
IF_COL0 = 13 * BW
GATE_COL0 = 13 * BW + 2 * ML_HEADS

V7X_VMEM_LIMIT = 56 * 1024 * 1024
V7X_MXU_COLS = 256


def _cparams(n_grid):
    return pltpu.CompilerParams(dimension_semantics=("arbitrary",) * n_grid, vmem_limit_bytes=V7X_VMEM_LIMIT)


def _sigmoid(x):
    return 1.0 / (1.0 + jnp.exp(-x))


def _silu(x):
    return x * _sigmoid(x)


def _log_sigmoid(x):
    return jnp.minimum(x, 0.0) - jnp.log1p(jnp.exp(-jnp.abs(x)))


def _iota(shape, dim):
    return lax.broadcasted_iota(jnp.int32, shape, dim)


def _dot(a, b):
    return jnp.dot(a, b, preferred_element_type=F32)


def _dot_t(a, b, precision=None):
    return lax.dot_general(a, b, (((1,), (1,)), ((), ())), preferred_element_type=F32, precision=precision)


def _lookup(j, table):
    out = table[0]
    for k in range(1, len(table)):
        out = jnp.where(j == k, table[k], out)
    return out


def _lb_kernel(logit_ref, lb_ref):
    z = logit_ref[...]
    rows = [z[l:l + 1] for l in range(DEPTH)]
    mx = functools.reduce(jnp.maximum, rows)
    ex = [jnp.exp(r - mx) for r in rows]
    tot = functools.reduce(lambda a, b: a + b, ex)
    sm = [e / tot for e in ex]
    run = sm[0]
    lb_ref[0:1, :] = run - sm[0]
    for l in range(1, DEPTH):
        run = run + sm[l]
        lb_ref[l:l + 1, :] = run - sm[0]


def _lower_bounds(logits):
    return pl.pallas_call(_lb_kernel, out_shape=jax.ShapeDtypeStruct(logits.shape, F32), name="lower_bounds")(logits)


def _proj_body(xb_ref, wb_ref, out_ref, epilogue):
    for s in range(wb_ref.shape[1] // V7X_MXU_COLS):
        cols = slice(s * V7X_MXU_COLS, (s + 1) * V7X_MXU_COLS)
        out_ref[:, cols] = epilogue(_dot(xb_ref[...], wb_ref[:, cols]), cols).astype(out_ref.dtype)


def _cast_weights(w_ref, wb_ref):
    @pl.when(pl.program_id(1) == 0)
    def _():
        wb_ref[...] = w_ref[...].astype(BF16)


def _proj_silu_kernel(xb_ref, w_ref, out_ref, wb_ref):
    _cast_weights(w_ref, wb_ref)
    _proj_body(xb_ref, wb_ref, out_ref, lambda acc, cols: _silu(acc))


def _proj_sigmoid_kernel(xb_ref, w_ref, out_ref, wb_ref):
    _cast_weights(w_ref, wb_ref)
    _proj_body(xb_ref, wb_ref, out_ref, lambda acc, cols: _sigmoid(acc))


def _proj_plain_kernel(xb_ref, w_ref, out_ref, wb_ref, *, scaled_block, scale):
    _cast_weights(w_ref, wb_ref)
    sc = jnp.where(pl.program_id(0) == scaled_block, scale, 1.0).astype(F32)
    _proj_body(xb_ref, wb_ref, out_ref, lambda acc, cols: acc * sc)


def _proj_rotary_kernel(xb_ref, w_ref, cos_ref, sin_ref, out_ref, wb_ref, *, scaled_block, scale):
    _cast_weights(w_ref, wb_ref)
    sc = jnp.where(pl.program_id(0) == scaled_block, scale, 1.0).astype(F32)
    assert V7X_MXU_COLS == RET_D

    def epilogue(acc, cols):
        even = (_iota(acc.shape, 1) % 2) == 0
        swapped = jnp.where(even, pltpu.roll(acc, RET_D - 1, 1), pltpu.roll(acc, 1, 1))
        return (acc * cos_ref[...] + swapped * sin_ref[...]) * sc

    _proj_body(xb_ref, wb_ref, out_ref, epilogue)


def _proj_logf_kernel(xb_ref, w_ref, lb_ref, wif_ref, bif_ref, g_ref, gt_ref, wb_ref):
    _cast_weights(w_ref, wb_ref)
    pt = _dot(xb_ref[...], wif_ref[...]).T[0:2 * ML_HEADS, :] + bif_ref[...]
    gt_ref[...] = jnp.where(_iota(pt.shape, 0) < ML_HEADS, pt, _log_sigmoid(pt))

    def epilogue(acc, cols):
        lb = lb_ref[:, cols]
        a = jnp.log(jnp.maximum(lb, HG_LB_FLOOR))
        b = jnp.log1p(-lb) + _log_sigmoid(acc)
        return jnp.maximum(a, b) + jnp.log1p(jnp.exp(-jnp.abs(a - b)))

    _proj_body(xb_ref, wb_ref, g_ref, epilogue)


def _proj(kern, xb, w_in, layer, blocks, extra_in, extra_specs, out_dtype, *, tm, extra_out=(), extra_out_specs=()):
    t = xb.shape[0]
    out_shape = [jax.ShapeDtypeStruct((t, len(blocks) * BW), out_dtype)] + list(extra_out)
    out_specs = [pl.BlockSpec((tm, BW), lambda j, i: (i, j))] + list(extra_out_specs)
    res = pl.pallas_call(
        kern,
        grid=(len(blocks), t // tm),
        in_specs=[pl.BlockSpec((tm, D_MODEL), lambda j, i: (i, 0)),
                  pl.BlockSpec((None, D_MODEL, BW), lambda j, i: (layer, 0, _lookup(j, blocks)))] + list(extra_specs),
        out_specs=out_specs,
        out_shape=out_shape,
        scratch_shapes=[pltpu.VMEM((D_MODEL, BW), BF16)],
        compiler_params=_cparams(2),
        name="proj",
    )(xb, w_in, *extra_in)
    return res


def _project_all(xb, w_in, layer, lbs3, w_if, b_if, cos_t, sin_t, *, tm):
    t = xb.shape[0]
    n_pos = cos_t.shape[0] // tm
    g, gt = _proj(
        _proj_logf_kernel, xb, w_in, layer, (_HF,),
        (lbs3, w_if, b_if),
        (pl.BlockSpec((None, 1, BW), lambda j, i: (layer, 0, 0)),
         pl.BlockSpec((None, D_MODEL, 128), lambda j, i: (layer, 0, 0)),
         pl.BlockSpec((None, 2 * ML_HEADS, 1), lambda j, i: (layer, 0, 0))),
        F32, tm=tm,
        extra_out=(jax.ShapeDtypeStruct((2 * ML_HEADS, t), F32),),
        extra_out_specs=(pl.BlockSpec((2 * ML_HEADS, tm), lambda j, i: (0, i)),))
    ps, = _proj(_proj_silu_kernel, xb, w_in, layer, SILU_BLOCKS, (), (), BF16, tm=tm)
    po, = _proj(_proj_sigmoid_kernel, xb, w_in, layer, (_MO,), (), (), BF16, tm=tm)
    pr, = _proj(functools.partial(_proj_rotary_kernel, scaled_block=ROT_BLOCKS.index(_RK), scale=RET_D ** -0.5),
                xb, w_in, layer, ROT_BLOCKS, (cos_t, sin_t),
                (pl.BlockSpec((tm, RET_D), lambda j, i: (i % n_pos, 0)),
                 pl.BlockSpec((tm, RET_D), lambda j, i: (i % n_pos, 0))), BF16, tm=tm)
    pp, = _proj(functools.partial(_proj_plain_kernel, scaled_block=PLAIN_BLOCKS.index(_MK), scale=ML_D ** -0.5),
                xb, w_in, layer, PLAIN_BLOCKS, (), (), BF16, tm=tm)
    return g, gt, ps, po, pr, pp


def _rms_gate(o, gain, zs):
    y = o * lax.rsqrt(jnp.mean(o * o, axis=-1, keepdims=True) + NORM_EPS)
    return (y * gain * zs).astype(BF16)


def _ln_gate(o, gain, zs):
    mu = jnp.mean(o, axis=-1, keepdims=True)
    d = o - mu
    var = jnp.mean(d * d, axis=-1, keepdims=True)
    return (d * lax.rsqrt(var + NORM_EPS) * gain * zs).astype(BF16)


def _state_chain(state_shape, block, index_map, prev_out, n_in, out_pos):
    spec = pl.BlockSpec(block, index_map)
    shape = jax.ShapeDtypeStruct(state_shape, F32)
    if prev_out is None:
        return spec, shape, [], [], {}
    return spec, shape, [pl.BlockSpec(memory_space=pl.ANY)], [prev_out], {n_in: out_pos}


def _hgrn_level_ref(bc, m):
    cs, d = bc.shape
    if m >= 8:
        parts = [jnp.broadcast_to(bc[b * 2 * m + m - 1:b * 2 * m + m], (2 * m, d)) for b in range(cs // (2 * m))]
        return parts[0] if len(parts) == 1 else jnp.concatenate(parts, axis=0)
    x3 = bc.reshape(cs // 8, 8, d)
    sub = _iota((cs // 8, 8, d), 1)
    pick = lambda r: jnp.broadcast_to(x3[:, r:r + 1, :], x3.shape)
    if m == 4:
        r3 = pick(3)
    elif m == 2:
        r3 = jnp.where(sub < 4, pick(1), pick(5))
    else:
        r3 = jnp.where(sub < 2, pick(0), jnp.where(sub < 4, pick(2), jnp.where(sub < 6, pick(4), pick(6))))
    return r3.reshape(cs, d)


def _hgrn_kernel(q_ref, g_ref, v_ref, z_ref, gain_ref, *rest):
    y_ref, s_ref = rest[-2:]
    c = pl.program_id(1)
    cs = q_ref.shape[0]

    @pl.when(c == 0)
    def _():
        s_ref[...] = jnp.zeros_like(s_ref)

    row = _iota((cs, cs), 0)
    col = _iota((cs, cs), 1)
    tril = (row >= col).astype(F32)
    diff = jnp.where(row > col, row ^ col, 0)
    eye_c = row == col
    eye_d = _iota((HG_D, HG_D), 0) == _iota((HG_D, HG_D), 1)
    levels = [cs >> (k + 1) for k in range(cs.bit_length() - 1)]
    for h in range(HG_HEADS):
        sl = slice(h * HG_D, (h + 1) * HG_D)
        q = q_ref[:, sl].astype(F32)
        g = g_ref[:, sl]
        v = v_ref[:, sl]
        kk = 1.0 - jnp.exp(g)
        bc = jnp.dot(tril, g, preferred_element_type=F32, precision=HIGHEST)
        s_old = s_ref[0, 0, h]
        a = jnp.where(eye_c, jnp.sum(q * kk, axis=1, keepdims=True), 0.0)
        for m in levels:
            e = jnp.exp(-jnp.abs(bc - _hgrn_level_ref(bc, m)))
            a_m = _dot_t((q * e).astype(BF16), (kk * e).astype(BF16))
            a = jnp.where((diff >> (m.bit_length() - 1)) == 1, a_m, a)
        o = _dot((q * jnp.exp(bc)).astype(BF16), s_old.astype(BF16)) + _dot(a.astype(BF16), v)
        y_ref[:, sl] = _rms_gate(o, gain_ref[:, sl], z_ref[:, sl].astype(F32))
        last = bc[cs - 1:cs]
        k_dec = kk * jnp.exp(last - bc)
        dec_col = jnp.sum(jnp.where(eye_d, jnp.exp(last), 0.0), axis=1, keepdims=True)
        s_ref[0, 0, h] = dec_col * s_old + _dot(k_dec.T.astype(BF16), v)


def _hgrn_prompt(ps, g, pp, gain3, layer, prev_out, *, bsz, seq, cs):
    nc = seq // cs
    tok = lambda b, c: b * nc + c
    in_specs = [
        pl.BlockSpec((cs, BW), lambda b, c: (tok(b, c), SILU_BLOCKS.index(_HQ))),
        pl.BlockSpec((cs, BW), lambda b, c: (tok(b, c), 0)),
        pl.BlockSpec((cs, BW), lambda b, c: (tok(b, c), PLAIN_BLOCKS.index(_HI))),
        pl.BlockSpec((cs, BW), lambda b, c: (tok(b, c), SILU_BLOCKS.index(_HZ))),
        pl.BlockSpec((None, 1, BW), lambda b, c: (layer, 0, 0)),
    ]
    args = [ps, g, pp, ps, gain3]
    s_spec, s_shape, x_specs, x_args, aliases = _state_chain(
        (DEPTH, bsz, HG_HEADS, HG_D, HG_D), (1, 1, HG_HEADS, HG_D, HG_D), lambda b, c: (layer, b, 0, 0, 0),
        prev_out, len(args), 1)
    return pl.pallas_call(
        _hgrn_kernel,
        grid=(bsz, nc),
        in_specs=in_specs + x_specs,
        out_specs=[pl.BlockSpec((cs, BW), lambda b, c: (tok(b, c), 0)), s_spec],
        out_shape=[jax.ShapeDtypeStruct((bsz * seq, BW), BF16), s_shape],
        input_output_aliases=aliases,
        compiler_params=_cparams(2),
        name="hgrn_prompt",
    )(*args, *x_args)


def _ret_log_gamma(h):
    return jnp.log(jnp.full((1, 1), 1.0 - 2.0 ** (-5.0 - h), F32))


def _ret_kernel(q_ref, k_ref, v_ref, z_ref, gain_ref, *rest):
    y_ref, s_ref = rest[-2:]
    c = pl.program_id(1)
    cs = q_ref.shape[0]

    @pl.when(c == 0)
    def _():
        s_ref[...] = jnp.zeros_like(s_ref)

    t_col = _iota((cs, 1), 0).astype(F32)
    rel = (_iota((cs, cs), 0) - _iota((cs, cs), 1)).astype(F32)
    causal = rel >= 0.0
    for h in range(RET_HEADS):
        sl = slice(h * RET_D, (h + 1) * RET_D)
        lg = _ret_log_gamma(h)
        q = q_ref[:, sl]
        k = k_ref[:, sl]
        v = v_ref[:, sl]
        s_old = s_ref[0, 0, h]
        intra = jnp.where(causal, jnp.exp(jnp.where(causal, rel * lg, 0.0)), 0.0)
        inter = jnp.exp((t_col + 1.0) * lg)
        to_end = jnp.exp((cs - 1.0 - t_col) * lg)
        a = _dot_t(q, k) * intra
        o = inter * _dot(q, s_old.astype(BF16)) + _dot(a.astype(BF16), v)
        y_ref[:, sl] = _ln_gate(o, gain_ref[:, sl], z_ref[:, sl].astype(F32))
        k_dec = k.astype(F32) * to_end
        s_ref[0, 0, h] = jnp.exp(cs * lg) * s_old + _dot(k_dec.T.astype(BF16), v)


def _ret_prompt(ps, pr, pp, gain3, layer, prev_out, *, bsz, seq, cs):
    nc = seq // cs
    tok = lambda b, c: b * nc + c
    in_specs = [
        pl.BlockSpec((cs, BW), lambda b, c: (tok(b, c), ROT_BLOCKS.index(_RQ))),
        pl.BlockSpec((cs, BW), lambda b, c: (tok(b, c), ROT_BLOCKS.index(_RK))),
        pl.BlockSpec((cs, BW), lambda b, c: (tok(b, c), PLAIN_BLOCKS.index(_RV))),
        pl.BlockSpec((cs, BW), lambda b, c: (tok(b, c), SILU_BLOCKS.index(_RZ))),
        pl.BlockSpec((None, 1, BW), lambda b, c: (layer, 0, 0)),
    ]
    args = [pr, pr, pp, ps, gain3]
    s_spec, s_shape, x_specs, x_args, aliases = _state_chain(
        (DEPTH, bsz, RET_HEADS, RET_D, RET_D), (1, 1, RET_HEADS, RET_D, RET_D), lambda b, c: (layer, b, 0, 0, 0),
        prev_out, len(args), 1)
    return pl.pallas_call(
        _ret_kernel,
        grid=(bsz, nc),
        in_specs=in_specs + x_specs,
        out_specs=[pl.BlockSpec((cs, BW), lambda b, c: (tok(b, c), 0)), s_spec],
        out_shape=[jax.ShapeDtypeStruct((bsz * seq, BW), BF16), s_shape],
        input_output_aliases=aliases,
        compiler_params=_cparams(2),
        name="ret_prompt",
    )(*args, *x_args)


def _mlstm_kernel(q_ref, k_ref, v_ref, z_ref, og_ref, gt_ref, gain_ref, *rest):
    y_ref, c_ref, n_ref, m_ref = rest[-4:]
    c = pl.program_id(1)
    cs = q_ref.shape[0]

    @pl.when(c == 0)
    def _():
        c_ref[...] = jnp.zeros_like(c_ref)
        n_ref[...] = jnp.zeros_like(n_ref)
        m_ref[...] = jnp.zeros_like(m_ref)

    row = _iota((cs, cs), 0)
    col = _iota((cs, cs), 1)
    causal = row >= col
    tril = causal.astype(F32)
    triu = (row <= col).astype(F32)
    eye = (row == col).astype(F32)
    gi = gt_ref[0:ML_HEADS, :]
    gf = gt_ref[ML_HEADS:2 * ML_HEADS, :]
    b_rows = jnp.dot(gf, triu, preferred_element_type=F32, precision=HIGHEST)
    b_cols = _dot_t(tril, gf, precision=HIGHEST)
    i_cols = _dot_t(eye, gi, precision=HIGHEST)
    for h in range(ML_HEADS):
        sl = slice(h * ML_D, (h + 1) * ML_D)
        q = q_ref[:, sl]
        k = k_ref[:, sl]
        v = v_ref[:, sl]
        c_old = c_ref[0, 0, h]
        n_old = n_ref[0, :, sl]
        m_old = m_ref[0, :, h * 128:h * 128 + 1]
        b_col = b_cols[:, h:h + 1]
        i_col = i_cols[:, h:h + 1]
        b_row = b_rows[h:h + 1, :]
        i_row = gi[h:h + 1, :]
        log_w = jnp.where(causal, b_col - b_row + i_row, NEG_LARGE)
        log_inter = m_old + b_col
        m = jnp.maximum(log_inter, jnp.max(log_w, axis=1, keepdims=True))
        w = jnp.where(causal, jnp.exp(jnp.where(causal, log_w - m, 0.0)), 0.0)
        sc = jnp.exp(log_inter - m)
        qk = _dot_t(q, k) * w
        num = _dot(qk.astype(BF16), v) + sc * _dot(q, c_old.astype(BF16))
        den = jnp.sum(qk, axis=1, keepdims=True) + sc * jnp.sum(q.astype(F32) * n_old, axis=1, keepdims=True)
        hid = num / jnp.maximum(jnp.abs(den), jnp.exp(-m))
        hid = og_ref[:, sl].astype(F32) * hid
        y_ref[:, sl] = _ln_gate(hid, gain_ref[:, sl], z_ref[:, sl].astype(F32))
        m_last = m[cs - 1:cs]
        b_last = b_col[cs - 1:cs]
        dec = jnp.exp(m_old + b_last - m_last)
        kw = k.astype(F32) * jnp.exp(b_last - b_col + i_col - m_last)
        c_ref[0, 0, h] = dec * c_old + _dot(kw.T.astype(BF16), v)
        n_ref[0, :, sl] = dec * n_old + jnp.sum(kw, axis=0, keepdims=True)
        m_ref[0, :, h * 128:(h + 1) * 128] = jnp.broadcast_to(m_last, (1, 128))


def _mlstm_prompt(ps, po, pp, gt, gain3, layer, prev_out, *, bsz, seq, cs):
    nc = seq // cs
    tok = lambda b, c: b * nc + c
    in_specs = [
        pl.BlockSpec((cs, BW), lambda b, c: (tok(b, c), PLAIN_BLOCKS.index(_MQ))),
        pl.BlockSpec((cs, BW), lambda b, c: (tok(b, c), PLAIN_BLOCKS.index(_MK))),
        pl.BlockSpec((cs, BW), lambda b, c: (tok(b, c), PLAIN_BLOCKS.index(_MV))),
        pl.BlockSpec((cs, BW), lambda b, c: (tok(b, c), SILU_BLOCKS.index(_MZ))),
        pl.BlockSpec((cs, BW), lambda b, c: (tok(b, c), 0)),
        pl.BlockSpec((2 * ML_HEADS, cs), lambda b, c: (0, tok(b, c))),
        pl.BlockSpec((None, 1, BW), lambda b, c: (layer, 0, 0)),
    ]
    args = [pp, pp, pp, ps, po, gt, gain3]
    c_spec, c_shape, x_specs, x_args, aliases = _state_chain(
        (DEPTH, bsz, ML_HEADS, ML_D, ML_D), (1, 1, ML_HEADS, ML_D, ML_D), lambda b, c: (layer, b, 0, 0, 0),
        prev_out, len(args), 1)
    return pl.pallas_call(
        _mlstm_kernel,
        grid=(bsz, nc),
        in_specs=in_specs + x_specs,
        out_specs=[
            pl.BlockSpec((cs, BW), lambda b, c: (tok(b, c), 0)),
            c_spec,
            pl.BlockSpec((1, 1, BW), lambda b, c: (b, 0, 0)),
            pl.BlockSpec((1, 1, ML_HEADS * 128), lambda b, c: (b, 0, 0)),
        ],
        out_shape=[
            jax.ShapeDtypeStruct((bsz * seq, BW), BF16),
            c_shape,
            jax.ShapeDtypeStruct((bsz, 1, BW), F32),
            jax.ShapeDtypeStruct((bsz, 1, ML_HEADS * 128), F32),
        ],
        input_output_aliases=aliases,
        compiler_params=_cparams(2),
        name="mlstm_prompt",
    )(*args, *x_args)


def _merge_kernel(xb_ref, ya_ref, yb_ref, yc_ref, wg0_ref, wg1_ref, wg2_ref, wbr_ref, out_ref):
    xb = xb_ref[...]
    for s in range(out_ref.shape[1] // V7X_MXU_COLS):
        cols = slice(s * V7X_MXU_COLS, (s + 1) * V7X_MXU_COLS)
        merged = None
        for n, (y_ref, wg_ref) in enumerate(((ya_ref, wg0_ref), (yb_ref, wg1_ref), (yc_ref, wg2_ref))):
            term = _sigmoid(_dot(xb, wg_ref[:, cols])) * _dot(y_ref[...], wbr_ref[n, :, cols])
            merged = term if merged is None else merged + term
        out_ref[:, cols] = merged.astype(BF16)


def _merge(xb, ya, yb, yc, w_gate, w_br, layer, *, tm, tn):
    t = xb.shape[0]
    nj = D_MODEL // tn
    gate_spec = lambda n: pl.BlockSpec((None, D_MODEL, tn), lambda i, j: (layer, 0, n * nj + j))
    return pl.pallas_call(
        _merge_kernel,
        grid=(t // tm, nj),
        in_specs=[
            pl.BlockSpec((tm, D_MODEL), lambda i, j: (i, 0)),
            pl.BlockSpec((tm, BW), lambda i, j: (i, 0)),
            pl.BlockSpec((tm, BW), lambda i, j: (i, 0)),
            pl.BlockSpec((tm, BW), lambda i, j: (i, 0)),
            gate_spec(0), gate_spec(1), gate_spec(2),
            pl.BlockSpec((None, 3, BW, tn), lambda i, j: (layer, 0, 0, j)),
        ],
        out_specs=pl.BlockSpec((tm, tn), lambda i, j: (i, j)),
        out_shape=jax.ShapeDtypeStruct((t, D_MODEL), BF16),
        compiler_params=_cparams(2),
        name="merge",
    )(xb, ya, yb, yc, w_gate, w_gate, w_gate, w_br)


def _outproj_kernel(m_ref, x_ref, wo_ref, lng_ref, lnb_ref, xo_ref, xbo_ref):
    for s in range(D_MODEL // V7X_MXU_COLS):
        cols = slice(s * V7X_MXU_COLS, (s + 1) * V7X_MXU_COLS)
        xo_ref[:, cols] = DEEPNORM_ALPHA * x_ref[:, cols] + _dot(m_ref[...], wo_ref[:, cols])
    hres = xo_ref[...]
    mu = jnp.mean(hres, axis=-1, keepdims=True)
    d = hres - mu
    var = jnp.mean(d * d, axis=-1, keepdims=True)
    x_new = d * lax.rsqrt(var + LN_EPS) * lng_ref[...] + lnb_ref[...]
    xo_ref[...] = x_new
    xbo_ref[...] = x_new.astype(BF16)


def _outproj(merged, x, w_o, lng3, lnb3, layer, *, tm):
    t = x.shape[0]
    return pl.pallas_call(
        _outproj_kernel,
        grid=(t // tm,),
        in_specs=[
            pl.BlockSpec((tm, D_MODEL), lambda i: (i, 0)),
            pl.BlockSpec((tm, D_MODEL), lambda i: (i, 0)),
            pl.BlockSpec((None, D_MODEL, D_MODEL), lambda i: (layer, 0, 0)),
            pl.BlockSpec((None, 1, D_MODEL), lambda i: (layer, 0, 0)),
            pl.BlockSpec((None, 1, D_MODEL), lambda i: (layer, 0, 0)),
        ],
        out_specs=[pl.BlockSpec((tm, D_MODEL), lambda i: (i, 0)), pl.BlockSpec((tm, D_MODEL), lambda i: (i, 0))],
        out_shape=[jax.ShapeDtypeStruct((t, D_MODEL), F32), jax.ShapeDtypeStruct((t, D_MODEL), BF16)],
        compiler_params=_cparams(1),
        name="outproj",
    )(merged, x, w_o, lng3, lnb3)


DEC_NB = 8


def _columns(x):
    d = x.shape[1]
    eye = (_iota((d, d), 0) == _iota((d, d), 1)).astype(F32)
    return _dot_t(eye, x, precision=HIGHEST)


def _rank1_update(s_old, decay, k_cols, q_cols, v_rows, j):
    s_new = decay * s_old + k_cols[:, j:j + 1] * v_rows[j:j + 1, :]
    o = jnp.sum(q_cols[:, j:j + 1] * s_new, axis=0, keepdims=True)
    return s_new, o


def _hgrn_dec_kernel(q_ref, g_ref, v_ref, z_ref, gain_ref, s_ref, *rest):
    y_ref, so_ref, o_scr = rest[-3:]
    for h in range(HG_HEADS):
        sl = slice(h * HG_D, (h + 1) * HG_D)
        f = jnp.exp(g_ref[:, sl])
        q_cols = _columns(q_ref[:, sl].astype(F32))
        f_cols = _columns(f)
        k_cols = _columns(1.0 - f)
        v_rows = v_ref[:, sl].astype(F32)
        for j in range(DEC_NB):
            s_new, o = _rank1_update(s_ref[0, j, h], f_cols[:, j:j + 1], k_cols, q_cols, v_rows, j)
            so_ref[0, j, h] = s_new
            o_scr[j:j + 1, sl] = o
        y_ref[:, sl] = _rms_gate(o_scr[:, sl], gain_ref[:, sl], z_ref[:, sl].astype(F32))


def _ret_dec_kernel(q_ref, k_ref, v_ref, z_ref, gain_ref, s_ref, *rest):
    y_ref, so_ref, o_scr = rest[-3:]
    h = pl.program_id(1)
    gamma = _lookup(h, tuple(jnp.float32(1.0 - 2.0 ** (-5.0 - hh)) for hh in range(RET_HEADS)))
    q_cols = _columns(q_ref[...].astype(F32))
    k_cols = _columns(k_ref[...].astype(F32))
    v_rows = v_ref[...].astype(F32)
    for j in range(DEC_NB):
        s_new, o = _rank1_update(s_ref[0, j, 0], gamma, k_cols, q_cols, v_rows, j)
        so_ref[0, j, 0] = s_new
        o_scr[j:j + 1, :] = o
    y_ref[...] = _ln_gate(o_scr[...], gain_ref[...], z_ref[...].astype(F32))


def _mlstm_gate_dec_kernel(gt_ref, m_ref, mo_ref, w_ref, sc_ref, eps_ref):
    gi = gt_ref[0:ML_HEADS, :]
    gf = gt_ref[ML_HEADS:2 * ML_HEADS, :]
    log_inter = m_ref[...] + gf
    m_new = jnp.maximum(log_inter, gi)
    mo_ref[...] = m_new
    w_ref[...] = jnp.exp(gi - m_new)
    sc_ref[...] = jnp.exp(log_inter - m_new)
    eps_ref[...] = jnp.exp(-m_new)


def _mlstm_gate_dec(gt, m_t):
    shp = jax.ShapeDtypeStruct(m_t.shape, F32)
    return pl.pallas_call(_mlstm_gate_dec_kernel, out_shape=[shp, shp, shp, shp], name="mlstm_gate_dec")(gt, m_t)


def _mlstm_dec_kernel(w_sm, sc_sm, eps_sm, q_ref, k_ref, v_ref, z_ref, og_ref, gain_ref, n_ref, c_ref, *rest, bsz):
    y_ref, no_ref, co_ref, o_scr, kw_scr, sc_scr, eps_scr = rest[-7:]
    i = pl.program_id(0)
    h = pl.program_id(1)
    qf = q_ref[...].astype(F32)
    q_cols = _columns(qf)
    k_rows = k_ref[...].astype(F32)
    for j in range(DEC_NB):
        idx = h * bsz + i * DEC_NB + j
        kw_scr[j:j + 1, :] = k_rows[j:j + 1, :] * w_sm[idx]
        sc_scr[j:j + 1, :] = jnp.full((1, 128), sc_sm[idx], F32)
        eps_scr[j:j + 1, :] = jnp.full((1, 128), eps_sm[idx], F32)
    kw_rows = kw_scr[...]
    kw_cols = _columns(kw_rows)
    v_rows = v_ref[...].astype(F32)
    for j in range(DEC_NB):
        sc = sc_sm[h * bsz + i * DEC_NB + j]
        c_new, o = _rank1_update(c_ref[0, j, 0], sc, kw_cols, q_cols, v_rows, j)
        co_ref[0, j, 0] = c_new
        o_scr[j:j + 1, :] = o
    n_new = sc_scr[:, 0:1] * n_ref[...] + kw_rows
    no_ref[...] = n_new
    den = jnp.sum(qf * n_new, axis=1, keepdims=True)
    hid = o_scr[...] / jnp.maximum(jnp.abs(den), eps_scr[:, 0:1])
    hid = og_ref[...].astype(F32) * hid
    y_ref[...] = _ln_gate(hid, gain_ref[...], z_ref[...].astype(F32))


def _hgrn_dec(ps, g, pp, gain3, state, layer, prev_out):
    bsz = ps.shape[0]
    in_specs = [
        pl.BlockSpec((DEC_NB, BW), lambda i: (i, SILU_BLOCKS.index(_HQ))),
        pl.BlockSpec((DEC_NB, BW), lambda i: (i, 0)),
        pl.BlockSpec((DEC_NB, BW), lambda i: (i, PLAIN_BLOCKS.index(_HI))),
        pl.BlockSpec((DEC_NB, BW), lambda i: (i, SILU_BLOCKS.index(_HZ))),
        pl.BlockSpec((None, 1, BW), lambda i: (layer, 0, 0)),
        pl.BlockSpec((1, DEC_NB, HG_HEADS, HG_D, HG_D), lambda i: (layer, i, 0, 0, 0)),
    ]
    args = [ps, g, pp, ps, gain3, state]
    s_spec, s_shape, x_specs, x_args, aliases = _state_chain(
        state.shape, (1, DEC_NB, HG_HEADS, HG_D, HG_D), lambda i: (layer, i, 0, 0, 0), prev_out, len(args), 1)
    return pl.pallas_call(
        _hgrn_dec_kernel,
        grid=(bsz // DEC_NB,),
        in_specs=in_specs + x_specs,
        out_specs=[pl.BlockSpec((DEC_NB, BW), lambda i: (i, 0)), s_spec],
        out_shape=[jax.ShapeDtypeStruct((bsz, BW), BF16), s_shape],
        scratch_shapes=[pltpu.VMEM((DEC_NB, BW), F32)],
        input_output_aliases=aliases,
        compiler_params=_cparams(1),
        name="hgrn_dec",
    )(*args, *x_args)


def _ret_dec(ps, pr, pp, gain3, state, layer, prev_out):
    bsz = ps.shape[0]
    d, nh = RET_D, RET_HEADS
    in_specs = [
        pl.BlockSpec((DEC_NB, d), lambda i, h: (i, ROT_BLOCKS.index(_RQ) * nh + h)),
        pl.BlockSpec((DEC_NB, d), lambda i, h: (i, ROT_BLOCKS.index(_RK) * nh + h)),
        pl.BlockSpec((DEC_NB, d), lambda i, h: (i, PLAIN_BLOCKS.index(_RV) * nh + h)),
        pl.BlockSpec((DEC_NB, d), lambda i, h: (i, SILU_BLOCKS.index(_RZ) * nh + h)),
        pl.BlockSpec((None, 1, d), lambda i, h: (layer, 0, h)),
        pl.BlockSpec((1, DEC_NB, 1, d, d), lambda i, h: (layer, i, h, 0, 0)),
    ]
    args = [pr, pr, pp, ps, gain3, state]
    s_spec, s_shape, x_specs, x_args, aliases = _state_chain(
        state.shape, (1, DEC_NB, 1, d, d), lambda i, h: (layer, i, h, 0, 0), prev_out, len(args), 1)
    return pl.pallas_call(
        _ret_dec_kernel,
        grid=(bsz // DEC_NB, nh),
        in_specs=in_specs + x_specs,
        out_specs=[pl.BlockSpec((DEC_NB, d), lambda i, h: (i, h)), s_spec],
        out_shape=[jax.ShapeDtypeStruct((bsz, BW), BF16), s_shape],
        scratch_shapes=[pltpu.VMEM((DEC_NB, d), F32)],
        input_output_aliases=aliases,
        compiler_params=_cparams(2),
        name="ret_dec",
    )(*args, *x_args)


def _mlstm_dec(ps, po, pp, gain3, w_flat, sc_flat, eps_flat, n_state, c_state, layer, prev_out):
    bsz = ps.shape[0]
    d, nh = ML_D, ML_HEADS
    row = lambda blocks, blk: pl.BlockSpec((DEC_NB, d), lambda i, h, *_: (i, blocks.index(blk) * nh + h))
    in_specs = [
        row(PLAIN_BLOCKS, _MQ), row(PLAIN_BLOCKS, _MK), row(PLAIN_BLOCKS, _MV), row(SILU_BLOCKS, _MZ),
        pl.BlockSpec((DEC_NB, d), lambda i, h, *_: (i, h)),
        pl.BlockSpec((None, 1, d), lambda i, h, *_: (layer, 0, h)),
        pl.BlockSpec((DEC_NB, d), lambda i, h, *_: (i, h)),
        pl.BlockSpec((1, DEC_NB, 1, d, d), lambda i, h, *_: (layer, i, h, 0, 0)),
    ]
    args = [pp, pp, pp, ps, po, gain3, n_state, c_state]
    n_prefetch = 3
    c_spec, c_shape, x_specs, x_args, aliases = _state_chain(
        c_state.shape, (1, DEC_NB, 1, d, d), lambda i, h, *_: (layer, i, h, 0, 0), prev_out,
        n_prefetch + len(args), 2)
    grid_spec = pltpu.PrefetchScalarGridSpec(
        num_scalar_prefetch=n_prefetch,
        grid=(bsz // DEC_NB, nh),
        in_specs=in_specs + x_specs,
        out_specs=[pl.BlockSpec((DEC_NB, d), lambda i, h, *_: (i, h)),
                   pl.BlockSpec((DEC_NB, d), lambda i, h, *_: (i, h)),
                   c_spec],
        scratch_shapes=[pltpu.VMEM((DEC_NB, d), F32), pltpu.VMEM((DEC_NB, d), F32),
                        pltpu.VMEM((DEC_NB, 128), F32), pltpu.VMEM((DEC_NB, 128), F32)],
    )
    return pl.pallas_call(
        functools.partial(_mlstm_dec_kernel, bsz=bsz),
        grid_spec=grid_spec,
        out_shape=[jax.ShapeDtypeStruct((bsz, BW), BF16), jax.ShapeDtypeStruct((bsz, BW), F32), c_shape],
        input_output_aliases=aliases,
        compiler_params=_cparams(2),
        name="mlstm_dec",
    )(w_flat, sc_flat, eps_flat, *args, *x_args)


def _rotary_tables(pos):
    theta = 1.0 / (ROPE_BASE ** jnp.linspace(0.0, 1.0, RET_D // 2, dtype=F32))
    ang = pos.astype(F32)[:, None] * theta[None]
    cos = jnp.repeat(jnp.cos(ang), 2, axis=1)
    sin = jnp.repeat(jnp.sin(ang), 2, axis=1)
    sign = jnp.tile(jnp.asarray([-1.0, 1.0], F32), RET_D // 2)
    return cos, sin * sign[None]


def _rows3(a):
    return a.astype(F32)[:, None, :]


def _tile_sizes(n_tokens, seq):
    return min(1024, seq), min(1024, n_tokens), 512, min(512, n_tokens)


def kernel(x_prompt, x_sample, state_hgrn, state_ret, state_mlstm_c, state_mlstm_n, state_mlstm_m,
           w_in, hgrn_lb_logits, hgrn_norm, ret_norm, mlstm_norm, mlstm_b_i, mlstm_b_f,
           w_branch, w_out, ln_g, ln_b):
    bp, lp, _ = x_prompt.shape
    bs, ls, _ = x_sample.shape
    assert ls == 1 and w_in.dtype == F32
    lbs3 = _lower_bounds(hgrn_lb_logits.astype(F32))[:, None, :]
    w_if = jnp.pad(w_in[:, :, IF_COL0:GATE_COL0], ((0, 0), (0, 0), (0, 128 - 2 * ML_HEADS))).astype(BF16)
    b_if = jnp.concatenate([mlstm_b_i, mlstm_b_f], axis=-1).astype(F32)[:, :, None]
    w_gate = w_in[:, :, GATE_COL0:].astype(BF16)
    w_br = w_branch.astype(BF16)
    w_o = w_out.astype(BF16)
    hg_gain, ret_gain, ml_gain = _rows3(hgrn_norm), _rows3(ret_norm), _rows3(mlstm_norm)
    lng3, lnb3 = _rows3(ln_g), _rows3(ln_b)
    cos_p, sin_p = _rotary_tables(jnp.arange(lp))
    cos_s, sin_s = _rotary_tables(jnp.full((bs,), PAST_LEN))

    tp, tmm, tnm, tmo = _tile_sizes(bp * lp, lp)
    x = x_prompt.reshape(bp * lp, D_MODEL)
    xb = x.astype(BF16)
    hg_p = ret_p = mc_p = None
    mn_l, mm_l = [], []
    for l in range(DEPTH):
        g, gt, ps, po, pr, pp = _project_all(xb, w_in, l, lbs3, w_if, b_if, cos_p, sin_p, tm=tp)
        ya, hg_p = _hgrn_prompt(ps, g, pp, hg_gain, l, hg_p, bsz=bp, seq=lp, cs=min(128, lp))
        yb, ret_p = _ret_prompt(ps, pr, pp, ret_gain, l, ret_p, bsz=bp, seq=lp, cs=min(128, lp))
        yc, mc_p, mn, mm = _mlstm_prompt(ps, po, pp, gt, ml_gain, l, mc_p, bsz=bp, seq=lp, cs=min(128, lp))
        merged = _merge(xb, ya, yb, yc, w_gate, w_br, l, tm=tmm, tn=tnm)
        x, xb = _outproj(merged, x, w_o, lng3, lnb3, l, tm=tmo)
        mn_l.append(mn.reshape(bp, ML_HEADS, ML_D))
        mm_l.append(mm.reshape(bp, ML_HEADS, 128)[:, :, 0])
    y_prompt = x.reshape(bp, lp, D_MODEL)

    ts, tmm, tnm, tmo = _tile_sizes(bs, bs)
    x = x_sample.reshape(bs, D_MODEL)
    xb = x.astype(BF16)
    hg_s = ret_s = mc_s = None
    mn_sl, mm_sl = [], []
    for l in range(DEPTH):
        g, gt, ps, po, pr, pp = _project_all(xb, w_in, l, lbs3, w_if, b_if, cos_s, sin_s, tm=ts)
        ya, hg_s = _hgrn_dec(ps, g, pp, hg_gain, state_hgrn, l, hg_s)
        yb, ret_s = _ret_dec(ps, pr, pp, ret_gain, state_ret, l, ret_s)
        m_new, w_t, sc_t, eps_t = _mlstm_gate_dec(gt, state_mlstm_m[l].T)
        yc, mn, mc_s = _mlstm_dec(ps, po, pp, ml_gain, w_t.reshape(-1), sc_t.reshape(-1), eps_t.reshape(-1),
                                  state_mlstm_n[l].reshape(bs, BW), state_mlstm_c, l, mc_s)
        merged = _merge(xb, ya, yb, yc, w_gate, w_br, l, tm=tmm, tn=tnm)
        x, xb = _outproj(merged, x, w_o, lng3, lnb3, l, tm=tmo)
        mn_sl.append(mn.reshape(bs, ML_HEADS, ML_D))
        mm_sl.append(m_new.T)
    y_sample = x.reshape(bs, 1, D_MODEL)

    return (y_prompt, y_sample, hg_p, ret_p, mc_p, jnp.stack(mn_l), jnp.stack(mm_l),
            hg_s, ret_s, mc_s, jnp.stack(mn_sl), jnp.stack(mm_sl))
```

```python
import functools

import jax
import jax.numpy as jnp
from jax import lax
from jax.experimental import pallas as pl
from jax.experimental.pallas import tpu as pltpu

F32 = jnp.float32
BF16 = jnp.bfloat16
HIGHEST = lax.Precision.HIGHEST

D_MODEL = 2048
DEPTH = 4
PAST_LEN = 16384
BW = D_MODEL // 2
HG_HEADS, HG_D = 8, 128
RET_HEADS, RET_D = 4, 256
ML_HEADS, ML_D = 4, 256
HG_LB_FLOOR = 1e-30
ROPE_BASE = 10000.0
NEG_LARGE = -1e30
NORM_EPS = 1e-6
LN_EPS = 1e-5
DEEPNORM_ALPHA = (2 * DEPTH) ** 0.25

_HQ, _HF, _HI, _HZ, _RQ, _RK, _RV, _RZ, _MQ, _MK, _MV, _MZ, _MO = range(13)
SILU_BLOCKS = (_HQ, _HZ, _RZ, _MZ, _MO)
ROT_BLOCKS = (_RQ, _RK)
PLAIN_BLOCKS = (_HI, _RV, _MQ, _MK, _MV)
IF_COL0 = 13 * BW
GATE_COL0 = 13 * BW + 2 * ML_HEADS

V7X_VMEM_LIMIT = 56 * 1024 * 1024
V7X_MXU_COLS = 256


def _cparams(n_grid):
    return pltpu.CompilerParams(dimension_semantics=("arbitrary",) * n_grid, vmem_limit_bytes=V7X_VMEM_LIMIT)


def _sigmoid(x):
    return 1.0 / (1.0 + jnp.exp(-x))


def _silu(x):
    return x * _sigmoid(x)


def _log_sigmoid(x):
    return jnp.minimum(x, 0.0) - jnp.log1p(jnp.exp(-jnp.abs(x)))


def _iota(shape, dim):
    return lax.broadcasted_iota(jnp.int32, shape, dim)


def _dot(a, b):
    return jnp.dot(a, b, preferred_element_type=F32)


def _dot_t(a, b, precision=None):
    return lax.dot_general(a, b, (((1,), (1,)), ((), ())), preferred_element_type=F32, precision=precision)


def _lookup(j, table):
    out = table[0]
    for k in range(1, len(table)):
        out = jnp.where(j == k, table[k], out)
    return out


def _lb_kernel(logit_ref, lb_ref):
    z = logit_ref[...]
    rows = [z[l:l + 1] for l in range(DEPTH)]
    mx = functools.reduce(jnp.maximum, rows)
    ex = [jnp.exp(r - mx) for r in rows]
    tot = functools.reduce(lambda a, b: a + b, ex)
    sm = [e / tot for e in ex]
    run = sm[0]
    lb_ref[0:1, :] = run - sm[0]
    for l in range(1, DEPTH):
        run = run + sm[l]
        lb_ref[l:l + 1, :] = run - sm[0]


def _lower_bounds(logits):
    return pl.pallas_call(_lb_kernel, out_shape=jax.ShapeDtypeStruct(logits.shape, F32), name="lower_bounds")(logits)


GATE_PREP_TN = 512
GATE_PREP_ROWS = 256


def _gate_prep_kernel(main_ref, nxt_ref, wg_ref, wif_ref):
    tn = main_ref.shape[1]
    shift = GATE_COL0 - IF_COL0
    for r in range(D_MODEL // GATE_PREP_ROWS):
        rows = slice(r * GATE_PREP_ROWS, (r + 1) * GATE_PREP_ROWS)
        cat = jnp.concatenate([main_ref[rows, :], nxt_ref[rows, :]], axis=1)
        wg_ref[rows, :] = pltpu.roll(cat, cat.shape[1] - shift, 1)[:, :tn].astype(BF16)

    @pl.when(pl.program_id(1) == 0)
    def _():
        head = main_ref[:, :128]
        wif_ref[...] = jnp.where(_iota(head.shape, 1) < shift, head, 0.0).astype(BF16)


def _gate_prep(w_in):
    tn = GATE_PREP_TN
    return pl.pallas_call(
        _gate_prep_kernel,
        grid=(DEPTH, 3 * D_MODEL // tn),
        in_specs=[pl.BlockSpec((None, D_MODEL, tn), lambda l, j: (l, 0, IF_COL0 // tn + j)),
                  pl.BlockSpec((None, D_MODEL, 128), lambda l, j: (l, 0, IF_COL0 // 128 + (j + 1) * (tn // 128)))],
        out_specs=[pl.BlockSpec((None, D_MODEL, tn), lambda l, j: (l, 0, j)),
                   pl.BlockSpec((None, D_MODEL, 128), lambda l, j: (l, 0, 0))],
        out_shape=[jax.ShapeDtypeStruct((DEPTH, D_MODEL, 3 * D_MODEL), BF16),
                   jax.ShapeDtypeStruct((DEPTH, D_MODEL, 128), BF16)],
        compiler_params=_cparams(2),
        name="gate_prep",
    )(w_in, w_in)


def _proj_body(xb_ref, wb_ref, out_ref, epilogue):
    for s in range(wb_ref.shape[1] // V7X_MXU_COLS):
        cols = slice(s * V7X_MXU_COLS, (s + 1) * V7X_MXU_COLS)
        out_ref[:, cols] = epilogue(_dot(xb_ref[...], wb_ref[:, cols]), cols).astype(out_ref.dtype)


def _cast_weights(w_ref, wb_ref):
    @pl.when(pl.program_id(1) == 0)
    def _():
        wb_ref[...] = w_ref[...].astype(BF16)


def _proj_silu_kernel(xb_ref, w_ref, out_ref, wb_ref, *, sigmoid_block):
    _cast_weights(w_ref, wb_ref)
    plain_sigmoid = pl.program_id(0) == sigmoid_block

    def epilogue(acc, cols):
        s = _sigmoid(acc)
        return jnp.where(plain_sigmoid, s, acc * s)

    _proj_body(xb_ref, wb_ref, out_ref, epilogue)


def _proj_plain_kernel(xb_ref, w_ref, out_ref, wb_ref, *, scaled_block, scale):
    _cast_weights(w_ref, wb_ref)
    sc = jnp.where(pl.program_id(0) == scaled_block, scale, 1.0).astype(F32)
    _proj_body(xb_ref, wb_ref, out_ref, lambda acc, cols: acc * sc)


def _proj_rotary_kernel(xb_ref, w_ref, cos_ref, sin_ref, out_ref, wb_ref, *, scaled_block, scale):
    _cast_weights(w_ref, wb_ref)
    sc = jnp.where(pl.program_id(0) == scaled_block, scale, 1.0).astype(F32)
    assert V7X_MXU_COLS == RET_D

    def epilogue(acc, cols):
        even = (_iota(acc.shape, 1) % 2) == 0
        swapped = jnp.where(even, pltpu.roll(acc, RET_D - 1, 1), pltpu.roll(acc, 1, 1))
        return (acc * cos_ref[...] + swapped * sin_ref[...]) * sc

    _proj_body(xb_ref, wb_ref, out_ref, epilogue)


def _proj_logf_kernel(xb_ref, w_ref, lb_ref, wif_ref, bif_ref, g_ref, gt_ref, wb_ref):
    _cast_weights(w_ref, wb_ref)
    pt = _dot(xb_ref[...], wif_ref[...]).T[0:2 * ML_HEADS, :] + bif_ref[...]
    gt_ref[...] = jnp.where(_iota(pt.shape, 0) < ML_HEADS, pt, _log_sigmoid(pt))

    def epilogue(acc, cols):
        lb = lb_ref[:, cols]
        return jnp.log(jnp.maximum(lb, HG_LB_FLOOR) + (1.0 - lb) * _sigmoid(acc))

    _proj_body(xb_ref, wb_ref, g_ref, epilogue)


def _proj(kern, xb, w_in, layer, blocks, extra_in, extra_specs, out_dtype, *, tm, extra_out=(), extra_out_specs=()):
    t = xb.shape[0]
    out_shape = [jax.ShapeDtypeStruct((t, len(blocks) * BW), out_dtype)] + list(extra_out)
    out_specs = [pl.BlockSpec((tm, BW), lambda j, i: (i, j))] + list(extra_out_specs)
    res = pl.pallas_call(
        kern,
        grid=(len(blocks), t // tm),
        in_specs=[pl.BlockSpec((tm, D_MODEL), lambda j, i: (i, 0)),
                  pl.BlockSpec((None, D_MODEL, BW), lambda j, i: (layer, 0, _lookup(j, blocks)))] + list(extra_specs),
        out_specs=out_specs,
        out_shape=out_shape,
        scratch_shapes=[pltpu.VMEM((D_MODEL, BW), BF16)],
        compiler_params=_cparams(2),
        name="proj",
    )(xb, w_in, *extra_in)
    return res


def _project_all(xb, w_in, layer, lbs3, w_if, b_if, cos_t, sin_t, *, tm):
    t = xb.shape[0]
    n_pos = cos_t.shape[0] // tm
    g, gt = _proj(
        _proj_logf_kernel, xb, w_in, layer, (_HF,),
        (lbs3, w_if, b_if),
        (pl.BlockSpec((None, 1, BW), lambda j, i: (layer, 0, 0)),
         pl.BlockSpec((None, D_MODEL, 128), lambda j, i: (layer, 0, 0)),
         pl.BlockSpec((None, 2 * ML_HEADS, 1), lambda j, i: (layer, 0, 0))),
        F32, tm=tm,
        extra_out=(jax.ShapeDtypeStruct((2 * ML_HEADS, t), F32),),
        extra_out_specs=(pl.BlockSpec((2 * ML_HEADS, tm), lambda j, i: (0, i)),))
    ps, = _proj(functools.partial(_proj_silu_kernel, sigmoid_block=SILU_BLOCKS.index(_MO)),
                xb, w_in, layer, SILU_BLOCKS, (), (), BF16, tm=tm)
    pr, = _proj(functools.partial(_proj_rotary_kernel, scaled_block=ROT_BLOCKS.index(_RK), scale=RET_D ** -0.5),
                xb, w_in, layer, ROT_BLOCKS, (cos_t, sin_t),
                (pl.BlockSpec((tm, RET_D), lambda j, i: (i % n_pos, 0)),
                 pl.BlockSpec((tm, RET_D), lambda j, i: (i % n_pos, 0))), BF16, tm=tm)
    pp, = _proj(functools.partial(_proj_plain_kernel, scaled_block=PLAIN_BLOCKS.index(_MK), scale=ML_D ** -0.5),
                xb, w_in, layer, PLAIN_BLOCKS, (), (), BF16, tm=tm)
    return g, gt, ps, pr, pp


def _rms_gate(o, gain, zs):
    y = o * lax.rsqrt(jnp.mean(o * o, axis=-1, keepdims=True) + NORM_EPS)
    return (y * gain * zs).astype(BF16)


def _ln_gate(o, gain, zs):
    mu = jnp.mean(o, axis=-1, keepdims=True)
    d = o - mu
    var = jnp.mean(d * d, axis=-1, keepdims=True)
    return (d * lax.rsqrt(var + NORM_EPS) * gain * zs).astype(BF16)


def _state_chain(state_shape, block, index_map, prev_out, n_in, out_pos):
    spec = pl.BlockSpec(block, index_map)
    shape = jax.ShapeDtypeStruct(state_shape, F32)
    if prev_out is None:
        return spec, shape, [], [], {}
    return spec, shape, [pl.BlockSpec(memory_space=pl.ANY)], [prev_out], {n_in: out_pos}


def _hgrn_level_ref(bc, m):
    cs, d = bc.shape
    if m >= 8:
        parts = [jnp.broadcast_to(bc[b * 2 * m + m - 1:b * 2 * m + m], (2 * m, d)) for b in range(cs // (2 * m))]
        return parts[0] if len(parts) == 1 else jnp.concatenate(parts, axis=0)
    x3 = bc.reshape(cs // 8, 8, d)
    sub = _iota((cs // 8, 8, d), 1)
    pick = lambda r: jnp.broadcast_to(x3[:, r:r + 1, :], x3.shape)
    if m == 4:
        r3 = pick(3)
    elif m == 2:
        r3 = jnp.where(sub < 4, pick(1), pick(5))
    else:
        r3 = jnp.where(sub < 2, pick(0), jnp.where(sub < 4, pick(2), jnp.where(sub < 6, pick(4), pick(6))))
    return r3.reshape(cs, d)


def _hgrn_kernel(q_ref, g_ref, v_ref, z_ref, gain_ref, *rest):
    y_ref, s_ref = rest[-2:]
    c = pl.program_id(1)
    cs = q_ref.shape[0]

    @pl.when(c == 0)
    def _():
        s_ref[...] = jnp.zeros_like(s_ref)

    row = _iota((cs, cs), 0)
    col = _iota((cs, cs), 1)
    tril = (row >= col).astype(F32)
    level = jnp.where(row > col, 31 - lax.clz(row ^ col), -1)
    eye_c = row == col
    eye_d = _iota((HG_D, HG_D), 0) == _iota((HG_D, HG_D), 1)
    levels = [cs >> (k + 1) for k in range(cs.bit_length() - 1)]
    heads = [slice(h * HG_D, (h + 1) * HG_D) for h in range(HG_HEADS)]
    q = q_ref[...].astype(F32)
    g = g_ref[...]
    kk = 1.0 - jnp.exp(g)
    bc = jnp.dot(tril, g, preferred_element_type=F32, precision=HIGHEST)
    qk = q * kk
    a = [jnp.where(eye_c, jnp.sum(qk[:, sl], axis=1, keepdims=True), 0.0) for sl in heads]
    for m in levels:
        e = jnp.exp(-jnp.abs(bc - _hgrn_level_ref(bc, m)))
        qs = (q * e).astype(BF16)
        ks = (kk * e).astype(BF16)
        sel = level == m.bit_length() - 1
        a = [jnp.where(sel, _dot_t(qs[:, sl], ks[:, sl]), a_h) for sl, a_h in zip(heads, a)]
    qe = (q * jnp.exp(bc)).astype(BF16)
    last = bc[cs - 1:cs]
    k_dec_t = (kk * jnp.exp(last - bc)).T.astype(BF16)
    dec_row = jnp.exp(last)
    for h, sl in enumerate(heads):
        v = v_ref[:, sl]
        s_old = s_ref[0, 0, h]
        o = _dot(qe[:, sl], s_old.astype(BF16)) + _dot(a[h].astype(BF16), v)
        y_ref[:, sl] = _rms_gate(o, gain_ref[:, sl], z_ref[:, sl].astype(F32))
        dec_col = jnp.sum(jnp.where(eye_d, dec_row[:, sl], 0.0), axis=1, keepdims=True)
        s_ref[0, 0, h] = dec_col * s_old + _dot(k_dec_t[sl, :], v)


def _hgrn_prompt(ps, g, pp, gain3, layer, prev_out, *, bsz, seq, cs):
    nc = seq // cs
    tok = lambda b, c: b * nc + c
    in_specs = [
        pl.BlockSpec((cs, BW), lambda b, c: (tok(b, c), SILU_BLOCKS.index(_HQ))),
        pl.BlockSpec((cs, BW), lambda b, c: (tok(b, c), 0)),
        pl.BlockSpec((cs, BW), lambda b, c: (tok(b, c), PLAIN_BLOCKS.index(_HI))),
        pl.BlockSpec((cs, BW), lambda b, c: (tok(b, c), SILU_BLOCKS.index(_HZ))),
        pl.BlockSpec((None, 1, BW), lambda b, c: (layer, 0, 0)),
    ]
    args = [ps, g, pp, ps, gain3]
    s_spec, s_shape, x_specs, x_args, aliases = _state_chain(
        (DEPTH, bsz, HG_HEADS, HG_D, HG_D), (1, 1, HG_HEADS, HG_D, HG_D), lambda b, c: (layer, b, 0, 0, 0),
        prev_out, len(args), 1)
    return pl.pallas_call(
        _hgrn_kernel,
        grid=(bsz, nc),
        in_specs=in_specs + x_specs,
        out_specs=[pl.BlockSpec((cs, BW), lambda b, c: (tok(b, c), 0)), s_spec],
        out_shape=[jax.ShapeDtypeStruct((bsz * seq, BW), BF16), s_shape],
        input_output_aliases=aliases,
        compiler_params=_cparams(2),
        name="hgrn_prompt",
    )(*args, *x_args)


def _ret_log_gamma(h):
    return jnp.log(jnp.full((1, 1), 1.0 - 2.0 ** (-5.0 - h), F32))


def _ret_kernel(q_ref, k_ref, v_ref, z_ref, gain_ref, *rest):
    y_ref, s_ref = rest[-2:]
    c = pl.program_id(1)
    cs = q_ref.shape[0]

    @pl.when(c == 0)
    def _():
        s_ref[...] = jnp.zeros_like(s_ref)

    t_col = _iota((cs, 1), 0).astype(F32)
    rel = (_iota((cs, cs), 0) - _iota((cs, cs), 1)).astype(F32)
    causal = rel >= 0.0
    heads = [slice(h * RET_D, (h + 1) * RET_D) for h in range(RET_HEADS)]
    hr = range(RET_HEADS)
    lg = [_ret_log_gamma(h) for h in hr]
    s_qk = [_dot_t(q_ref[:, sl], k_ref[:, sl]) for sl in heads]
    s_old = [s_ref[0, 0, h] for h in hr]
    q_s = [_dot(q_ref[:, sl], s_old[h].astype(BF16)) for h, sl in enumerate(heads)]
    k_dec_t = [(k_ref[:, sl].astype(F32) * jnp.exp((cs - 1.0 - t_col) * lg[h])).T.astype(BF16)
               for h, sl in enumerate(heads)]
    upd = [_dot(k_dec_t[h], v_ref[:, sl]) for h, sl in enumerate(heads)]
    a = [s_qk[h] * jnp.where(causal, jnp.exp(jnp.where(causal, rel * lg[h], 0.0)), 0.0) for h in hr]
    a_v = [_dot(a[h].astype(BF16), v_ref[:, sl]) for h, sl in enumerate(heads)]
    for h, sl in enumerate(heads):
        o = jnp.exp((t_col + 1.0) * lg[h]) * q_s[h] + a_v[h]
        y_ref[:, sl] = _ln_gate(o, gain_ref[:, sl], z_ref[:, sl].astype(F32))
        s_ref[0, 0, h] = jnp.exp(cs * lg[h]) * s_old[h] + upd[h]


def _ret_prompt(ps, pr, pp, gain3, layer, prev_out, *, bsz, seq, cs):
    nc = seq // cs
    tok = lambda b, c: b * nc + c
    in_specs = [
        pl.BlockSpec((cs, BW), lambda b, c: (tok(b, c), ROT_BLOCKS.index(_RQ))),
        pl.BlockSpec((cs, BW), lambda b, c: (tok(b, c), ROT_BLOCKS.index(_RK))),
        pl.BlockSpec((cs, BW), lambda b, c: (tok(b, c), PLAIN_BLOCKS.index(_RV))),
        pl.BlockSpec((cs, BW), lambda b, c: (tok(b, c), SILU_BLOCKS.index(_RZ))),
        pl.BlockSpec((None, 1, BW), lambda b, c: (layer, 0, 0)),
    ]
    args = [pr, pr, pp, ps, gain3]
    s_spec, s_shape, x_specs, x_args, aliases = _state_chain(
        (DEPTH, bsz, RET_HEADS, RET_D, RET_D), (1, 1, RET_HEADS, RET_D, RET_D), lambda b, c: (layer, b, 0, 0, 0),
        prev_out, len(args), 1)
    return pl.pallas_call(
        _ret_kernel,
        grid=(bsz, nc),
        in_specs=in_specs + x_specs,
        out_specs=[pl.BlockSpec((cs, BW), lambda b, c: (tok(b, c), 0)), s_spec],
        out_shape=[jax.ShapeDtypeStruct((bsz * seq, BW), BF16), s_shape],
        input_output_aliases=aliases,
        compiler_params=_cparams(2),
        name="ret_prompt",
    )(*args, *x_args)


def _mlstm_kernel(q_ref, k_ref, v_ref, z_ref, og_ref, gt_ref, gain_ref, *rest):
    y_ref, c_ref, n_ref, m_ref = rest[-4:]
    c = pl.program_id(1)
    cs = q_ref.shape[0]

    @pl.when(c == 0)
    def _():
        c_ref[...] = jnp.zeros_like(c_ref)
        n_ref[...] = jnp.zeros_like(n_ref)
        m_ref[...] = jnp.zeros_like(m_ref)

    row = _iota((cs, cs), 0)
    col = _iota((cs, cs), 1)
    causal = row >= col
    tril = causal.astype(F32)
    triu = (row <= col).astype(F32)
    eye = (row == col).astype(F32)
    gi = gt_ref[0:ML_HEADS, :]
    gf = gt_ref[ML_HEADS:2 * ML_HEADS, :]
    b_rows = jnp.dot(gf, triu, preferred_element_type=F32, precision=HIGHEST)
    b_cols = _dot_t(tril, gf, precision=HIGHEST)
    i_cols = _dot_t(eye, gi, precision=HIGHEST)
    heads = [slice(h * ML_D, (h + 1) * ML_D) for h in range(ML_HEADS)]
    hr = range(ML_HEADS)
    s_qk = [_dot_t(q_ref[:, sl], k_ref[:, sl]) for sl in heads]
    c_old = [c_ref[0, 0, h] for h in hr]
    q_c = [_dot(q_ref[:, sl], c_old[h].astype(BF16)) for h, sl in enumerate(heads)]
    m_old = [m_ref[0, :, h * 128:h * 128 + 1] for h in hr]
    b_col = [b_cols[:, h:h + 1] for h in hr]
    log_inter = [m_old[h] + b_col[h] for h in hr]
    log_w = [jnp.where(causal, b_col[h] - b_rows[h:h + 1, :] + gi[h:h + 1, :], NEG_LARGE) for h in hr]
    m = [jnp.maximum(log_inter[h], jnp.max(log_w[h], axis=1, keepdims=True)) for h in hr]
    qk = [s_qk[h] * jnp.where(causal, jnp.exp(jnp.where(causal, log_w[h] - m[h], 0.0)), 0.0) for h in hr]
    sc = [jnp.exp(log_inter[h] - m[h]) for h in hr]
    num = [_dot(qk[h].astype(BF16), v_ref[:, sl]) + sc[h] * q_c[h] for h, sl in enumerate(heads)]
    m_last = [m[h][cs - 1:cs] for h in hr]
    b_last = [b_col[h][cs - 1:cs] for h in hr]
    kw = [k_ref[:, sl].astype(F32) * jnp.exp(b_last[h] - b_col[h] + i_cols[:, h:h + 1] - m_last[h])
          for h, sl in enumerate(heads)]
    upd = [_dot(kw[h].T.astype(BF16), v_ref[:, sl]) for h, sl in enumerate(heads)]
    for h, sl in enumerate(heads):
        n_old = n_ref[0, :, sl]
        den = (jnp.sum(qk[h], axis=1, keepdims=True)
               + sc[h] * jnp.sum(q_ref[:, sl].astype(F32) * n_old, axis=1, keepdims=True))
        hid = num[h] / jnp.maximum(jnp.abs(den), jnp.exp(-m[h]))
        hid = og_ref[:, sl].astype(F32) * hid
        y_ref[:, sl] = _ln_gate(hid, gain_ref[:, sl], z_ref[:, sl].astype(F32))
        dec = jnp.exp(m_old[h] + b_last[h] - m_last[h])
        c_ref[0, 0, h] = dec * c_old[h] + upd[h]
        n_ref[0, :, sl] = dec * n_old + jnp.sum(kw[h], axis=0, keepdims=True)
        m_ref[0, :, h * 128:(h + 1) * 128] = jnp.broadcast_to(m_last[h], (1, 128))


def _mlstm_prompt(ps, pp, gt, gain3, layer, prev_out, *, bsz, seq, cs):
    nc = seq // cs
    tok = lambda b, c: b * nc + c
    in_specs = [
        pl.BlockSpec((cs, BW), lambda b, c: (tok(b, c), PLAIN_BLOCKS.index(_MQ))),
        pl.BlockSpec((cs, BW), lambda b, c: (tok(b, c), PLAIN_BLOCKS.index(_MK))),
        pl.BlockSpec((cs, BW), lambda b, c: (tok(b, c), PLAIN_BLOCKS.index(_MV))),
        pl.BlockSpec((cs, BW), lambda b, c: (tok(b, c), SILU_BLOCKS.index(_MZ))),
        pl.BlockSpec((cs, BW), lambda b, c: (tok(b, c), SILU_BLOCKS.index(_MO))),
        pl.BlockSpec((2 * ML_HEADS, cs), lambda b, c: (0, tok(b, c))),
        pl.BlockSpec((None, 1, BW), lambda b, c: (layer, 0, 0)),
    ]
    args = [pp, pp, pp, ps, ps, gt, gain3]
    c_spec, c_shape, x_specs, x_args, aliases = _state_chain(
        (DEPTH, bsz, ML_HEADS, ML_D, ML_D), (1, 1, ML_HEADS, ML_D, ML_D), lambda b, c: (layer, b, 0, 0, 0),
        prev_out, len(args), 1)
    return pl.pallas_call(
        _mlstm_kernel,
        grid=(bsz, nc),
        in_specs=in_specs + x_specs,
        out_specs=[
            pl.BlockSpec((cs, BW), lambda b, c: (tok(b, c), 0)),
            c_spec,
            pl.BlockSpec((1, 1, BW), lambda b, c: (b, 0, 0)),
            pl.BlockSpec((1, 1, ML_HEADS * 128), lambda b, c: (b, 0, 0)),
        ],
        out_shape=[
            jax.ShapeDtypeStruct((bsz * seq, BW), BF16),
            c_shape,
            jax.ShapeDtypeStruct((bsz, 1, BW), F32),
            jax.ShapeDtypeStruct((bsz, 1, ML_HEADS * 128), F32),
        ],
        input_output_aliases=aliases,
        compiler_params=_cparams(2),
        name="mlstm_prompt",
    )(*args, *x_args)


def _merge_kernel(xb_ref, ya_ref, yb_ref, yc_ref, wg0_ref, wg1_ref, wg2_ref, wbr_ref, out_ref):
    xb = xb_ref[...]
    for s in range(out_ref.shape[1] // V7X_MXU_COLS):
        cols = slice(s * V7X_MXU_COLS, (s + 1) * V7X_MXU_COLS)
        merged = None
        for n, (y_ref, wg_ref) in enumerate(((ya_ref, wg0_ref), (yb_ref, wg1_ref), (yc_ref, wg2_ref))):
            term = _sigmoid(_dot(xb, wg_ref[:, cols])) * _dot(y_ref[...], wbr_ref[n, :, cols])
            merged = term if merged is None else merged + term
        out_ref[:, cols] = merged.astype(BF16)


def _merge(xb, ya, yb, yc, w_gate, w_br, layer, *, tm, tn):
    t = xb.shape[0]
    nj = D_MODEL // tn
    gate_spec = lambda n: pl.BlockSpec((None, D_MODEL, tn), lambda i, j: (layer, 0, n * nj + j))
    return pl.pallas_call(
        _merge_kernel,
        grid=(t // tm, nj),
        in_specs=[
            pl.BlockSpec((tm, D_MODEL), lambda i, j: (i, 0)),
            pl.BlockSpec((tm, BW), lambda i, j: (i, 0)),
            pl.BlockSpec((tm, BW), lambda i, j: (i, 0)),
            pl.BlockSpec((tm, BW), lambda i, j: (i, 0)),
            gate_spec(0), gate_spec(1), gate_spec(2),
            pl.BlockSpec((None, 3, BW, tn), lambda i, j: (layer, 0, 0, j)),
        ],
        out_specs=pl.BlockSpec((tm, tn), lambda i, j: (i, j)),
        out_shape=jax.ShapeDtypeStruct((t, D_MODEL), BF16),
        compiler_params=_cparams(2),
        name="merge",
    )(xb, ya, yb, yc, w_gate, w_gate, w_gate, w_br)


def _outproj_kernel(m_ref, x_ref, wo_ref, lng_ref, lnb_ref, xo_ref, xbo_ref):
    for s in range(D_MODEL // V7X_MXU_COLS):
        cols = slice(s * V7X_MXU_COLS, (s + 1) * V7X_MXU_COLS)
        xo_ref[:, cols] = DEEPNORM_ALPHA * x_ref[:, cols] + _dot(m_ref[...], wo_ref[:, cols])
    hres = xo_ref[...]
    mu = jnp.mean(hres, axis=-1, keepdims=True)
    d = hres - mu
    var = jnp.mean(d * d, axis=-1, keepdims=True)
    x_new = d * lax.rsqrt(var + LN_EPS) * lng_ref[...] + lnb_ref[...]
    xo_ref[...] = x_new
    xbo_ref[...] = x_new.astype(BF16)


def _outproj(merged, x, w_o, lng3, lnb3, layer, *, tm):
    t = x.shape[0]
    return pl.pallas_call(
        _outproj_kernel,
        grid=(t // tm,),
        in_specs=[
            pl.BlockSpec((tm, D_MODEL), lambda i: (i, 0)),
            pl.BlockSpec((tm, D_MODEL), lambda i: (i, 0)),
            pl.BlockSpec((None, D_MODEL, D_MODEL), lambda i: (layer, 0, 0)),
            pl.BlockSpec((None, 1, D_MODEL), lambda i: (layer, 0, 0)),
            pl.BlockSpec((None, 1, D_MODEL), lambda i: (layer, 0, 0)),
        ],
        out_specs=[pl.BlockSpec((tm, D_MODEL), lambda i: (i, 0)), pl.BlockSpec((tm, D_MODEL), lambda i: (i, 0))],
        out_shape=[jax.ShapeDtypeStruct((t, D_MODEL), F32), jax.ShapeDtypeStruct((t, D_MODEL), BF16)],
        compiler_params=_cparams(1),
        name="outproj",
    )(merged, x, w_o, lng3, lnb3)


DEC_NB = 8


def _columns(x):
    d = x.shape[1]
    eye = (_iota((d, d), 0) == _iota((d, d), 1)).astype(F32)
    return _dot_t(eye, x, precision=HIGHEST)


def _pad_rows_bf16(x):
    return jnp.concatenate([x, jnp.zeros_like(x)], axis=0).astype(BF16)


def _bf16_columns(x16):
    d = x16.shape[1]
    eye = jnp.where(_iota((d, d), 0) == _iota((d, d), 1), 1.0, 0.0).astype(BF16)
    return _dot_t(eye, x16).astype(BF16)


def _rank1_updates(s_ref, so_ref, o_scr, h, sl, decays, k_cols16, v16, q16):
    d = k_cols16.shape[0]
    lane = _iota(k_cols16.shape, 1)
    k_stack = jnp.concatenate(
        [jnp.where(lane == j, k_cols16, jnp.zeros_like(k_cols16)) for j in range(DEC_NB)], axis=0)
    upd = _dot(k_stack, v16)
    s_bf = []
    for j in range(DEC_NB):
        s_new = decays[j] * s_ref[0, j, h] + upd[j * d:(j + 1) * d]
        so_ref[0, j, h] = s_new
        s_bf.append(s_new.astype(BF16))
    for j in range(DEC_NB):
        o_scr[j:j + 1, sl] = _dot(q16, s_bf[j])[j:j + 1, :]


def _hgrn_dec_kernel(q_ref, g_ref, v_ref, z_ref, gain_ref, s_ref, *rest):
    y_ref, so_ref, o_scr = rest[-3:]
    for h in range(HG_HEADS):
        sl = slice(h * HG_D, (h + 1) * HG_D)
        f = jnp.exp(g_ref[:, sl])
        f_cols = _columns(f)
        k_cols = _columns(1.0 - f)
        q_cols = _columns(q_ref[:, sl].astype(F32))
        v_rows = v_ref[:, sl].astype(F32)
        for j in range(DEC_NB):
            s_new = f_cols[:, j:j + 1] * s_ref[0, j, h] + k_cols[:, j:j + 1] * v_rows[j:j + 1, :]
            so_ref[0, j, h] = s_new
            o_scr[j:j + 1, sl] = jnp.sum(q_cols[:, j:j + 1] * s_new, axis=0, keepdims=True)
        y_ref[:, sl] = _rms_gate(o_scr[:, sl], gain_ref[:, sl], z_ref[:, sl].astype(F32))


def _ret_dec_kernel(q_ref, k_ref, v_ref, z_ref, gain_ref, s_ref, *rest):
    y_ref, so_ref, o_scr = rest[-3:]
    for h in range(RET_HEADS):
        sl = slice(h * RET_D, (h + 1) * RET_D)
        gamma = 1.0 - 2.0 ** (-5.0 - h)
        k_cols16 = _bf16_columns(_pad_rows_bf16(k_ref[:, sl].astype(F32)))
        q16 = _pad_rows_bf16(q_ref[:, sl].astype(F32))
        v16 = _pad_rows_bf16(v_ref[:, sl].astype(F32))
        _rank1_updates(s_ref, so_ref, o_scr, h, sl, [gamma] * DEC_NB, k_cols16, v16, q16)
        y_ref[:, sl] = _ln_gate(o_scr[:, sl], gain_ref[:, sl], z_ref[:, sl].astype(F32))


def _mlstm_gate_dec_kernel(gt_ref, m_ref, mo_ref, w_ref, sc_ref, eps_ref):
    gi = gt_ref[0:ML_HEADS, :]
    gf = gt_ref[ML_HEADS:2 * ML_HEADS, :]
    log_inter = m_ref[...] + gf
    m_new = jnp.maximum(log_inter, gi)
    mo_ref[...] = m_new
    w_ref[...] = jnp.exp(gi - m_new)
    sc_ref[...] = jnp.exp(log_inter - m_new)
    eps_ref[...] = jnp.exp(-m_new)


def _mlstm_gate_dec(gt, m_t):
    shp = jax.ShapeDtypeStruct(m_t.shape, F32)
    return pl.pallas_call(_mlstm_gate_dec_kernel, out_shape=[shp, shp, shp, shp], name="mlstm_gate_dec")(gt, m_t)


def _mlstm_dec_kernel(w_sm, sc_sm, eps_sm, q_ref, k_ref, v_ref, z_ref, og_ref, gain_ref, n_ref, c_ref, *rest, bsz):
    y_ref, no_ref, co_ref, o_scr, kw_scr, sc_scr, eps_scr = rest[-7:]
    i = pl.program_id(0)
    for h in range(ML_HEADS):
        sl = slice(h * ML_D, (h + 1) * ML_D)
        base = h * bsz + i * DEC_NB
        qf = q_ref[:, sl].astype(F32)
        k_rows = k_ref[:, sl].astype(F32)
        for j in range(DEC_NB):
            kw_scr[j:j + 1, :] = k_rows[j:j + 1, :] * w_sm[base + j]
            sc_scr[j:j + 1, :] = jnp.full((1, 128), sc_sm[base + j], F32)
            eps_scr[j:j + 1, :] = jnp.full((1, 128), eps_sm[base + j], F32)
        kw_rows = kw_scr[...]
        kw_cols16 = _bf16_columns(_pad_rows_bf16(kw_rows))
        q16 = _pad_rows_bf16(qf)
        v16 = _pad_rows_bf16(v_ref[:, sl].astype(F32))
        _rank1_updates(c_ref, co_ref, o_scr, h, sl, [sc_sm[base + j] for j in range(DEC_NB)], kw_cols16, v16, q16)
        n_new = sc_scr[:, 0:1] * n_ref[:, sl] + kw_rows
        no_ref[:, sl] = n_new
        den = jnp.sum(qf * n_new, axis=1, keepdims=True)
        hid = o_scr[:, sl] / jnp.maximum(jnp.abs(den), eps_scr[:, 0:1])
        hid = og_ref[:, sl].astype(F32) * hid
        y_ref[:, sl] = _ln_gate(hid, gain_ref[:, sl], z_ref[:, sl].astype(F32))


def _hgrn_dec(ps, g, pp, gain3, state, layer, prev_out):
    bsz = ps.shape[0]
    in_specs = [
        pl.BlockSpec((DEC_NB, BW), lambda i: (i, SILU_BLOCKS.index(_HQ))),
        pl.BlockSpec((DEC_NB, BW), lambda i: (i, 0)),
        pl.BlockSpec((DEC_NB, BW), lambda i: (i, PLAIN_BLOCKS.index(_HI))),
        pl.BlockSpec((DEC_NB, BW), lambda i: (i, SILU_BLOCKS.index(_HZ))),
        pl.BlockSpec((None, 1, BW), lambda i: (layer, 0, 0)),
        pl.BlockSpec((1, DEC_NB, HG_HEADS, HG_D, HG_D), lambda i: (layer, i, 0, 0, 0)),
    ]
    args = [ps, g, pp, ps, gain3, state]
    s_spec, s_shape, x_specs, x_args, aliases = _state_chain(
        state.shape, (1, DEC_NB, HG_HEADS, HG_D, HG_D), lambda i: (layer, i, 0, 0, 0), prev_out, len(args), 1)
    return pl.pallas_call(
        _hgrn_dec_kernel,
        grid=(bsz // DEC_NB,),
        in_specs=in_specs + x_specs,
        out_specs=[pl.BlockSpec((DEC_NB, BW), lambda i: (i, 0)), s_spec],
        out_shape=[jax.ShapeDtypeStruct((bsz, BW), BF16), s_shape],
        scratch_shapes=[pltpu.VMEM((DEC_NB, BW), F32)],
        input_output_aliases=aliases,
        compiler_params=_cparams(1),
        name="hgrn_dec",
    )(*args, *x_args)


def _ret_dec(ps, pr, pp, gain3, state, layer, prev_out):
    bsz = ps.shape[0]
    d, nh = RET_D, RET_HEADS
    in_specs = [
        pl.BlockSpec((DEC_NB, BW), lambda i: (i, ROT_BLOCKS.index(_RQ))),
        pl.BlockSpec((DEC_NB, BW), lambda i: (i, ROT_BLOCKS.index(_RK))),
        pl.BlockSpec((DEC_NB, BW), lambda i: (i, PLAIN_BLOCKS.index(_RV))),
        pl.BlockSpec((DEC_NB, BW), lambda i: (i, SILU_BLOCKS.index(_RZ))),
        pl.BlockSpec((None, 1, BW), lambda i: (layer, 0, 0)),
        pl.BlockSpec((1, DEC_NB, nh, d, d), lambda i: (layer, i, 0, 0, 0)),
    ]
    args = [pr, pr, pp, ps, gain3, state]
    s_spec, s_shape, x_specs, x_args, aliases = _state_chain(
        state.shape, (1, DEC_NB, nh, d, d), lambda i: (layer, i, 0, 0, 0), prev_out, len(args), 1)
    return pl.pallas_call(
        _ret_dec_kernel,
        grid=(bsz // DEC_NB,),
        in_specs=in_specs + x_specs,
        out_specs=[pl.BlockSpec((DEC_NB, BW), lambda i: (i, 0)), s_spec],
        out_shape=[jax.ShapeDtypeStruct((bsz, BW), BF16), s_shape],
        scratch_shapes=[pltpu.VMEM((DEC_NB, BW), F32)],
        input_output_aliases=aliases,
        compiler_params=_cparams(1),
        name="ret_dec",
    )(*args, *x_args)


def _mlstm_dec(ps, pp, gain3, w_flat, sc_flat, eps_flat, n_state, c_state, layer, prev_out):
    bsz = ps.shape[0]
    d, nh = ML_D, ML_HEADS
    row = lambda blocks, blk: pl.BlockSpec((DEC_NB, BW), lambda i, *_: (i, blocks.index(blk)))
    in_specs = [
        row(PLAIN_BLOCKS, _MQ), row(PLAIN_BLOCKS, _MK), row(PLAIN_BLOCKS, _MV), row(SILU_BLOCKS, _MZ),
        row(SILU_BLOCKS, _MO),
        pl.BlockSpec((None, 1, BW), lambda i, *_: (layer, 0, 0)),
        pl.BlockSpec((DEC_NB, BW), lambda i, *_: (i, 0)),
        pl.BlockSpec((1, DEC_NB, nh, d, d), lambda i, *_: (layer, i, 0, 0, 0)),
    ]
    args = [pp, pp, pp, ps, ps, gain3, n_state, c_state]
    n_prefetch = 3
    c_spec, c_shape, x_specs, x_args, aliases = _state_chain(
        c_state.shape, (1, DEC_NB, nh, d, d), lambda i, *_: (layer, i, 0, 0, 0), prev_out,
        n_prefetch + len(args), 2)
    grid_spec = pltpu.PrefetchScalarGridSpec(
        num_scalar_prefetch=n_prefetch,
        grid=(bsz // DEC_NB,),
        in_specs=in_specs + x_specs,
        out_specs=[pl.BlockSpec((DEC_NB, BW), lambda i, *_: (i, 0)),
                   pl.BlockSpec((DEC_NB, BW), lambda i, *_: (i, 0)),
                   c_spec],
        scratch_shapes=[pltpu.VMEM((DEC_NB, BW), F32), pltpu.VMEM((DEC_NB, d), F32),
                        pltpu.VMEM((DEC_NB, 128), F32), pltpu.VMEM((DEC_NB, 128), F32)],
    )
    return pl.pallas_call(
        functools.partial(_mlstm_dec_kernel, bsz=bsz),
        grid_spec=grid_spec,
        out_shape=[jax.ShapeDtypeStruct((bsz, BW), BF16), jax.ShapeDtypeStruct((bsz, BW), F32), c_shape],
        input_output_aliases=aliases,
        compiler_params=_cparams(1),
        name="mlstm_dec",
    )(w_flat, sc_flat, eps_flat, *args, *x_args)


def _rotary_tables(pos):
    theta = 1.0 / (ROPE_BASE ** jnp.linspace(0.0, 1.0, RET_D // 2, dtype=F32))
    ang = pos.astype(F32)[:, None] * theta[None]
    cos = jnp.repeat(jnp.cos(ang), 2, axis=1)
    sin = jnp.repeat(jnp.sin(ang), 2, axis=1)
    sign = jnp.tile(jnp.asarray([-1.0, 1.0], F32), RET_D // 2)
    return cos, sin * sign[None]


def _rows3(a):
    return a.astype(F32)[:, None, :]


def _tile_sizes(n_tokens, seq):
    return min(1024, seq), min(1024, n_tokens), 512, min(512, n_tokens)


def kernel(x_prompt, x_sample, state_hgrn, state_ret, state_mlstm_c, state_mlstm_n, state_mlstm_m,
           w_in, hgrn_lb_logits, hgrn_norm, ret_norm, mlstm_norm, mlstm_b_i, mlstm_b_f,
           w_branch, w_out, ln_g, ln_b):
    bp, lp, _ = x_prompt.shape
    bs, ls, _ = x_sample.shape
    assert ls == 1 and w_in.dtype == F32
    lbs3 = _lower_bounds(hgrn_lb_logits.astype(F32))[:, None, :]
    assert w_in.shape[-1] == GATE_COL0 + 3 * D_MODEL
    w_gate, w_if = _gate_prep(w_in)
    b_if = jnp.concatenate([mlstm_b_i, mlstm_b_f], axis=-1).astype(F32)[:, :, None]
    w_br = w_branch.astype(BF16)
    w_o = w_out.astype(BF16)
    hg_gain, ret_gain, ml_gain = _rows3(hgrn_norm), _rows3(ret_norm), _rows3(mlstm_norm)
    lng3, lnb3 = _rows3(ln_g), _rows3(ln_b)
    cos_p, sin_p = _rotary_tables(jnp.arange(lp))
    cos_s, sin_s = _rotary_tables(jnp.full((bs,), PAST_LEN))

    tp, tmm, tnm, tmo = _tile_sizes(bp * lp, lp)
    x = x_prompt.reshape(bp * lp, D_MODEL)
    xb = x.astype(BF16)
    hg_p = ret_p = mc_p = None
    mn_l, mm_l = [], []
    for l in range(DEPTH):
        g, gt, ps, pr, pp = _project_all(xb, w_in, l, lbs3, w_if, b_if, cos_p, sin_p, tm=tp)
        ya, hg_p = _hgrn_prompt(ps, g, pp, hg_gain, l, hg_p, bsz=bp, seq=lp, cs=min(128, lp))
        yb, ret_p = _ret_prompt(ps, pr, pp, ret_gain, l, ret_p, bsz=bp, seq=lp, cs=min(128, lp))
        yc, mc_p, mn, mm = _mlstm_prompt(ps, pp, gt, ml_gain, l, mc_p, bsz=bp, seq=lp, cs=min(128, lp))
        merged = _merge(xb, ya, yb, yc, w_gate, w_br, l, tm=tmm, tn=tnm)
        x, xb = _outproj(merged, x, w_o, lng3, lnb3, l, tm=tmo)
        mn_l.append(mn.reshape(bp, ML_HEADS, ML_D))
        mm_l.append(mm.reshape(bp, ML_HEADS, 128)[:, :, 0])
    y_prompt = x.reshape(bp, lp, D_MODEL)

    ts, tmm, tnm, tmo = _tile_sizes(bs, bs)
    x = x_sample.reshape(bs, D_MODEL)
    xb = x.astype(BF16)
    hg_s = ret_s = mc_s = None
    mn_sl, mm_sl = [], []
    for l in range(DEPTH):
        g, gt, ps, pr, pp = _project_all(xb, w_in, l, lbs3, w_if, b_if, cos_s, sin_s, tm=ts)
        ya, hg_s = _hgrn_dec(ps, g, pp, hg_gain, state_hgrn, l, hg_s)
        yb, ret_s = _ret_dec(ps, pr, pp, ret_gain, state_ret, l, ret_s)
        m_new, w_t, sc_t, eps_t = _mlstm_gate_dec(gt, state_mlstm_m[l].T)
        yc, mn, mc_s = _mlstm_dec(ps, pp, ml_gain, w_t.reshape(-1), sc_t.reshape(-1), eps_t.reshape(-1),
                                  state_mlstm_n[l].reshape(bs, BW), state_mlstm_c, l, mc_s)
        merged = _merge(xb, ya, yb, yc, w_gate, w_br, l, tm=tmm, tn=tnm)
        x, xb = _outproj(merged, x, w_o, lng3, lnb3, l, tm=tmo)
        mn_sl.append(mn.reshape(bs, ML_HEADS, ML_D))
        mm_sl.append(m_new.T)
    y_sample = x.reshape(bs, 1, D_MODEL)

    return (y_prompt, y_sample, hg_p, ret_p, mc_p, jnp.stack(mn_l), jnp.stack(mm_l),
            hg_s, ret_s, mc_s, jnp.stack(mn_sl), jnp.stack(mm_sl))
```

```python
import functools

import jax
import jax.numpy as jnp
from jax import lax
from jax.experimental import pallas as pl
from jax.experimental.pallas import tpu as pltpu

F32 = jnp.float32
BF16 = jnp.bfloat16
HIGHEST = lax.Precision.HIGHEST

D_MODEL = 2048
DEPTH = 4
PAST_LEN = 16384
BW = D_MODEL // 2
HG_HEADS, HG_D = 8, 128
RET_HEADS, RET_D = 4, 256
ML_HEADS, ML_D = 4, 256
HG_LB_FLOOR = 1e-30
ROPE_BASE = 10000.0
NEG_LARGE = -1e30
NORM_EPS = 1e-6
LN_EPS = 1e-5
DEEPNORM_ALPHA = (2 * DEPTH) ** 0.25

_HQ, _HF, _HI, _HZ, _RQ, _RK, _RV, _RZ, _MQ, _MK, _MV, _MZ, _MO = range(13)
SILU_BLOCKS = (_HQ, _HZ, _RZ, _MZ, _MO)
ROT_BLOCKS = (_RQ, _RK)
PLAIN_BLOCKS = (_HI, _RV, _MQ, _MK, _MV)
IF_COL0 = 13 * BW
GATE_COL0 = 13 * BW + 2 * ML_HEADS

V7X_VMEM_LIMIT = 56 * 1024 * 1024
V7X_MXU_COLS = 256


def _cparams(n_grid):
    return pltpu.CompilerParams(dimension_semantics=("arbitrary",) * n_grid, vmem_limit_bytes=V7X_VMEM_LIMIT)


def _sigmoid(x):
    return 1.0 / (1.0 + jnp.exp(-x))


def _silu(x):
    return x * _sigmoid(x)


def _log_sigmoid(x):
    return jnp.minimum(x, 0.0) - jnp.log1p(jnp.exp(-jnp.abs(x)))


def _iota(shape, dim):
    return lax.broadcasted_iota(jnp.int32, shape, dim)


def _dot(a, b):
    return jnp.dot(a, b, preferred_element_type=F32)


def _dot_t(a, b, precision=None):
    return lax.dot_general(a, b, (((1,), (1,)), ((), ())), preferred_element_type=F32, precision=precision)


def _lookup(j, table):
    out = table[0]
    for k in range(1, len(table)):
        out = jnp.where(j == k, table[k], out)
    return out


def _lb_kernel(logit_ref, lb_ref):
    z = logit_ref[...]
    rows = [z[l:l + 1] for l in range(DEPTH)]
    mx = functools.reduce(jnp.maximum, rows)
    ex = [jnp.exp(r - mx) for r in rows]
    tot = functools.reduce(lambda a, b: a + b, ex)
    sm = [e / tot for e in ex]
    run = sm[0]
    lb_ref[0:1, :] = run - sm[0]
    for l in range(1, DEPTH):
        run = run + sm[l]
        lb_ref[l:l + 1, :] = run - sm[0]


def _lower_bounds(logits):
    return pl.pallas_call(_lb_kernel, out_shape=jax.ShapeDtypeStruct(logits.shape, F32), name="lower_bounds")(logits)


GATE_PREP_TN = 512
XPOSE_CHUNK = 256


def _transpose_cast(w_ref, wb_ref):
    for r in range(w_ref.shape[1] // XPOSE_CHUNK):
        ks = slice(r * XPOSE_CHUNK, (r + 1) * XPOSE_CHUNK)
        wb_ref[ks, :] = w_ref[:, ks].T.astype(BF16)


def _gate_prep_kernel(wt_ref, wif_rows_ref, wg_ref, wif_ref):
    _transpose_cast(wt_ref, wg_ref)

    @pl.when(pl.program_id(1) == 0)
    def _():
        rows = wif_rows_ref[...]
        padded = jnp.concatenate([rows, jnp.zeros((128 - rows.shape[0], D_MODEL), F32)], axis=0)
        wif_ref[...] = padded.T.astype(BF16)


def _gate_prep(w_t):
    tn = GATE_PREP_TN
    assert GATE_COL0 % 8 == 0 and IF_COL0 % 8 == 0
    return pl.pallas_call(
        _gate_prep_kernel,
        grid=(DEPTH, 3 * D_MODEL // tn),
        in_specs=[pl.BlockSpec((None, pl.Element(tn), pl.Element(D_MODEL)), lambda l, j: (l, pl.multiple_of(GATE_COL0 + j * tn, 8), 0)),
                  pl.BlockSpec((None, 2 * ML_HEADS, D_MODEL), lambda l, j: (l, IF_COL0 // (2 * ML_HEADS), 0))],
        out_specs=[pl.BlockSpec((None, D_MODEL, tn), lambda l, j: (l, 0, j)),
                   pl.BlockSpec((None, D_MODEL, 128), lambda l, j: (l, 0, 0))],
        out_shape=[jax.ShapeDtypeStruct((DEPTH, D_MODEL, 3 * D_MODEL), BF16),
                   jax.ShapeDtypeStruct((DEPTH, D_MODEL, 128), BF16)],
        compiler_params=_cparams(2),
        name="gate_prep",
    )(w_t, w_t)


def _proj_body(xb_ref, wb_ref, out_ref, epilogue):
    for s in range(wb_ref.shape[1] // V7X_MXU_COLS):
        cols = slice(s * V7X_MXU_COLS, (s + 1) * V7X_MXU_COLS)
        out_ref[:, cols] = epilogue(_dot(xb_ref[...], wb_ref[:, cols]), cols).astype(out_ref.dtype)


def _cast_weights(w_ref, wb_ref):
    @pl.when(pl.program_id(1) == 0)
    def _():
        _transpose_cast(w_ref, wb_ref)


def _proj_silu_kernel(xb_ref, w_ref, out_ref, wb_ref, *, sigmoid_block):
    _cast_weights(w_ref, wb_ref)
    plain_sigmoid = pl.program_id(0) == sigmoid_block

    def epilogue(acc, cols):
        s = _sigmoid(acc)
        return jnp.where(plain_sigmoid, s, acc * s)

    _proj_body(xb_ref, wb_ref, out_ref, epilogue)


def _proj_plain_kernel(xb_ref, w_ref, out_ref, wb_ref, *, scaled_block, scale):
    _cast_weights(w_ref, wb_ref)
    sc = jnp.where(pl.program_id(0) == scaled_block, scale, 1.0).astype(F32)
    _proj_body(xb_ref, wb_ref, out_ref, lambda acc, cols: acc * sc)


def _proj_rotary_kernel(xb_ref, w_ref, cos_ref, sin_ref, out_ref, wb_ref, *, scaled_block, scale):
    _cast_weights(w_ref, wb_ref)
    sc = jnp.where(pl.program_id(0) == scaled_block, scale, 1.0).astype(F32)
    assert V7X_MXU_COLS == RET_D

    def epilogue(acc, cols):
        even = (_iota(acc.shape, 1) % 2) == 0
        swapped = jnp.where(even, pltpu.roll(acc, RET_D - 1, 1), pltpu.roll(acc, 1, 1))
        return (acc * cos_ref[...] + swapped * sin_ref[...]) * sc

    _proj_body(xb_ref, wb_ref, out_ref, epilogue)


def _proj_logf_kernel(xb_ref, w_ref, lb_ref, wif_ref, bif_ref, g_ref, gt_ref, wb_ref):
    _cast_weights(w_ref, wb_ref)
    pt = _dot(xb_ref[...], wif_ref[...]).T[0:2 * ML_HEADS, :] + bif_ref[...]
    gt_ref[...] = jnp.where(_iota(pt.shape, 0) < ML_HEADS, pt, _log_sigmoid(pt))

    def epilogue(acc, cols):
        lb = lb_ref[:, cols]
        return jnp.log(jnp.maximum(lb, HG_LB_FLOOR) + (1.0 - lb) * _sigmoid(acc))

    _proj_body(xb_ref, wb_ref, g_ref, epilogue)


def _proj(kern, xb, w_in, layer, blocks, extra_in, extra_specs, out_dtype, *, tm, extra_out=(), extra_out_specs=()):
    t = xb.shape[0]
    out_shape = [jax.ShapeDtypeStruct((t, len(blocks) * BW), out_dtype)] + list(extra_out)
    out_specs = [pl.BlockSpec((tm, BW), lambda j, i: (i, j))] + list(extra_out_specs)
    res = pl.pallas_call(
        kern,
        grid=(len(blocks), t // tm),
        in_specs=[pl.BlockSpec((tm, D_MODEL), lambda j, i: (i, 0)),
                  pl.BlockSpec((None, BW, D_MODEL), lambda j, i: (layer, _lookup(j, blocks), 0))] + list(extra_specs),
        out_specs=out_specs,
        out_shape=out_shape,
        scratch_shapes=[pltpu.VMEM((D_MODEL, BW), BF16)],
        compiler_params=_cparams(2),
        name="proj",
    )(xb, w_in, *extra_in)
    return res


def _project_all(xb, w_in, layer, lbs3, w_if, b_if, cos_t, sin_t, *, tm):
    t = xb.shape[0]
    n_pos = cos_t.shape[0] // tm
    g, gt = _proj(
        _proj_logf_kernel, xb, w_in, layer, (_HF,),
        (lbs3, w_if, b_if),
        (pl.BlockSpec((None, 1, BW), lambda j, i: (layer, 0, 0)),
         pl.BlockSpec((None, D_MODEL, 128), lambda j, i: (layer, 0, 0)),
         pl.BlockSpec((None, 2 * ML_HEADS, 1), lambda j, i: (layer, 0, 0))),
        F32, tm=tm,
        extra_out=(jax.ShapeDtypeStruct((2 * ML_HEADS, t), F32),),
        extra_out_specs=(pl.BlockSpec((2 * ML_HEADS, tm), lambda j, i: (0, i)),))
    ps, = _proj(functools.partial(_proj_silu_kernel, sigmoid_block=SILU_BLOCKS.index(_MO)),
                xb, w_in, layer, SILU_BLOCKS, (), (), BF16, tm=tm)
    pr, = _proj(functools.partial(_proj_rotary_kernel, scaled_block=ROT_BLOCKS.index(_RK), scale=RET_D ** -0.5),
                xb, w_in, layer, ROT_BLOCKS, (cos_t, sin_t),
                (pl.BlockSpec((tm, RET_D), lambda j, i: (i % n_pos, 0)),
                 pl.BlockSpec((tm, RET_D), lambda j, i: (i % n_pos, 0))), BF16, tm=tm)
    pp, = _proj(functools.partial(_proj_plain_kernel, scaled_block=PLAIN_BLOCKS.index(_MK), scale=ML_D ** -0.5),
                xb, w_in, layer, PLAIN_BLOCKS, (), (), BF16, tm=tm)
    return g, gt, ps, pr, pp


def _rms_gate(o, gain, zs):
    y = o * lax.rsqrt(jnp.mean(o * o, axis=-1, keepdims=True) + NORM_EPS)
    return (y * gain * zs).astype(BF16)


def _ln_gate(o, gain, zs):
    mu = jnp.mean(o, axis=-1, keepdims=True)
    d = o - mu
    var = jnp.mean(d * d, axis=-1, keepdims=True)
    return (d * lax.rsqrt(var + NORM_EPS) * gain * zs).astype(BF16)


def _state_chain(state_shape, block, index_map, prev_out, n_in, out_pos):
    spec = pl.BlockSpec(block, index_map)
    shape = jax.ShapeDtypeStruct(state_shape, F32)
    if prev_out is None:
        return spec, shape, [], [], {}
    return spec, shape, [pl.BlockSpec(memory_space=pl.ANY)], [prev_out], {n_in: out_pos}


def _hgrn_level_ref(bc, m):
    cs, d = bc.shape
    if m >= 8:
        parts = [jnp.broadcast_to(bc[b * 2 * m + m - 1:b * 2 * m + m], (2 * m, d)) for b in range(cs // (2 * m))]
        return parts[0] if len(parts) == 1 else jnp.concatenate(parts, axis=0)
    x3 = bc.reshape(cs // 8, 8, d)
    sub = _iota((cs // 8, 8, d), 1)
    pick = lambda r: jnp.broadcast_to(x3[:, r:r + 1, :], x3.shape)
    if m == 4:
        r3 = pick(3)
    elif m == 2:
        r3 = jnp.where(sub < 4, pick(1), pick(5))
    else:
        r3 = jnp.where(sub < 2, pick(0), jnp.where(sub < 4, pick(2), jnp.where(sub < 6, pick(4), pick(6))))
    return r3.reshape(cs, d)


def _hgrn_kernel(q_ref, g_ref, v_ref, z_ref, gain_ref, *rest):
    y_ref, s_ref = rest[-2:]
    c = pl.program_id(1)
    cs = q_ref.shape[0]

    @pl.when(c == 0)
    def _():
        s_ref[...] = jnp.zeros_like(s_ref)

    row = _iota((cs, cs), 0)
    col = _iota((cs, cs), 1)
    tril = (row >= col).astype(F32)
    level = jnp.where(row > col, 31 - lax.clz(row ^ col), -1)
    eye_c = row == col
    eye_d = _iota((HG_D, HG_D), 0) == _iota((HG_D, HG_D), 1)
    levels = [cs >> (k + 1) for k in range(cs.bit_length() - 1)]
    heads = [slice(h * HG_D, (h + 1) * HG_D) for h in range(HG_HEADS)]
    q = q_ref[...].astype(F32)
    g = g_ref[...]
    kk = 1.0 - jnp.exp(g)
    bc = jnp.dot(tril, g, preferred_element_type=F32, precision=HIGHEST)
    qk = q * kk
    a = [jnp.where(eye_c, jnp.sum(qk[:, sl], axis=1, keepdims=True), 0.0) for sl in heads]
    for m in levels:
        e = jnp.exp(-jnp.abs(bc - _hgrn_level_ref(bc, m)))
        qs = (q * e).astype(BF16)
        ks = (kk * e).astype(BF16)
        sel = level == m.bit_length() - 1
        a = [jnp.where(sel, _dot_t(qs[:, sl], ks[:, sl]), a_h) for sl, a_h in zip(heads, a)]
    qe = (q * jnp.exp(bc)).astype(BF16)
    last = bc[cs - 1:cs]
    k_dec_t = (kk * jnp.exp(last - bc)).T.astype(BF16)
    dec_row = jnp.exp(last)
    for h, sl in enumerate(heads):
        v = v_ref[:, sl]
        s_old = s_ref[0, 0, h]
        o = _dot(qe[:, sl], s_old.astype(BF16)) + _dot(a[h].astype(BF16), v)
        y_ref[:, sl] = _rms_gate(o, gain_ref[:, sl], z_ref[:, sl].astype(F32))
        dec_col = jnp.sum(jnp.where(eye_d, dec_row[:, sl], 0.0), axis=1, keepdims=True)
        s_ref[0, 0, h] = dec_col * s_old + _dot(k_dec_t[sl, :], v)


def _hgrn_prompt(ps, g, pp, gain3, layer, prev_out, *, bsz, seq, cs):
    nc = seq // cs
    tok = lambda b, c: b * nc + c
    in_specs = [
        pl.BlockSpec((cs, BW), lambda b, c: (tok(b, c), SILU_BLOCKS.index(_HQ))),
        pl.BlockSpec((cs, BW), lambda b, c: (tok(b, c), 0)),
        pl.BlockSpec((cs, BW), lambda b, c: (tok(b, c), PLAIN_BLOCKS.index(_HI))),
        pl.BlockSpec((cs, BW), lambda b, c: (tok(b, c), SILU_BLOCKS.index(_HZ))),
        pl.BlockSpec((None, 1, BW), lambda b, c: (layer, 0, 0)),
    ]
    args = [ps, g, pp, ps, gain3]
    s_spec, s_shape, x_specs, x_args, aliases = _state_chain(
        (DEPTH, bsz, HG_HEADS, HG_D, HG_D), (1, 1, HG_HEADS, HG_D, HG_D), lambda b, c: (layer, b, 0, 0, 0),
        prev_out, len(args), 1)
    return pl.pallas_call(
        _hgrn_kernel,
        grid=(bsz, nc),
        in_specs=in_specs + x_specs,
        out_specs=[pl.BlockSpec((cs, BW), lambda b, c: (tok(b, c), 0)), s_spec],
        out_shape=[jax.ShapeDtypeStruct((bsz * seq, BW), BF16), s_shape],
        input_output_aliases=aliases,
        compiler_params=_cparams(2),
        name="hgrn_prompt",
    )(*args, *x_args)


def _ret_log_gamma(h):
    return jnp.log(jnp.full((1, 1), 1.0 - 2.0 ** (-5.0 - h), F32))


def _ret_kernel(q_ref, k_ref, v_ref, z_ref, gain_ref, *rest):
    y_ref, s_ref = rest[-2:]
    c = pl.program_id(1)
    cs = q_ref.shape[0]

    @pl.when(c == 0)
    def _():
        s_ref[...] = jnp.zeros_like(s_ref)

    t_col = _iota((cs, 1), 0).astype(F32)
    rel = (_iota((cs, cs), 0) - _iota((cs, cs), 1)).astype(F32)
    causal = rel >= 0.0
    heads = [slice(h * RET_D, (h + 1) * RET_D) for h in range(RET_HEADS)]
    hr = range(RET_HEADS)
    lg = [_ret_log_gamma(h) for h in hr]
    s_qk = [_dot_t(q_ref[:, sl], k_ref[:, sl]) for sl in heads]
    s_old = [s_ref[0, 0, h] for h in hr]
    q_s = [_dot(q_ref[:, sl], s_old[h].astype(BF16)) for h, sl in enumerate(heads)]
    k_dec_t = [(k_ref[:, sl].astype(F32) * jnp.exp((cs - 1.0 - t_col) * lg[h])).T.astype(BF16)
               for h, sl in enumerate(heads)]
    upd = [_dot(k_dec_t[h], v_ref[:, sl]) for h, sl in enumerate(heads)]
    a = [s_qk[h] * jnp.where(causal, jnp.exp(jnp.where(causal, rel * lg[h], 0.0)), 0.0) for h in hr]
    a_v = [_dot(a[h].astype(BF16), v_ref[:, sl]) for h, sl in enumerate(heads)]
    for h, sl in enumerate(heads):
        o = jnp.exp((t_col + 1.0) * lg[h]) * q_s[h] + a_v[h]
        y_ref[:, sl] = _ln_gate(o, gain_ref[:, sl], z_ref[:, sl].astype(F32))
        s_ref[0, 0, h] = jnp.exp(cs * lg[h]) * s_old[h] + upd[h]


def _ret_prompt(ps, pr, pp, gain3, layer, prev_out, *, bsz, seq, cs):
    nc = seq // cs
    tok = lambda b, c: b * nc + c
    in_specs = [
        pl.BlockSpec((cs, BW), lambda b, c: (tok(b, c), ROT_BLOCKS.index(_RQ))),
        pl.BlockSpec((cs, BW), lambda b, c: (tok(b, c), ROT_BLOCKS.index(_RK))),
        pl.BlockSpec((cs, BW), lambda b, c: (tok(b, c), PLAIN_BLOCKS.index(_RV))),
        pl.BlockSpec((cs, BW), lambda b, c: (tok(b, c), SILU_BLOCKS.index(_RZ))),
        pl.BlockSpec((None, 1, BW), lambda b, c: (layer, 0, 0)),
    ]
    args = [pr, pr, pp, ps, gain3]
    s_spec, s_shape, x_specs, x_args, aliases = _state_chain(
        (DEPTH, bsz, RET_HEADS, RET_D, RET_D), (1, 1, RET_HEADS, RET_D, RET_D), lambda b, c: (layer, b, 0, 0, 0),
        prev_out, len(args), 1)
    return pl.pallas_call(
        _ret_kernel,
        grid=(bsz, nc),
        in_specs=in_specs + x_specs,
        out_specs=[pl.BlockSpec((cs, BW), lambda b, c: (tok(b, c), 0)), s_spec],
        out_shape=[jax.ShapeDtypeStruct((bsz * seq, BW), BF16), s_shape],
        input_output_aliases=aliases,
        compiler_params=_cparams(2),
        name="ret_prompt",
    )(*args, *x_args)


def _mlstm_kernel(q_ref, k_ref, v_ref, z_ref, og_ref, gt_ref, gain_ref, *rest):
    y_ref, c_ref, n_ref, m_ref = rest[-4:]
    c = pl.program_id(1)
    cs = q_ref.shape[0]

    @pl.when(c == 0)
    def _():
        c_ref[...] = jnp.zeros_like(c_ref)
        n_ref[...] = jnp.zeros_like(n_ref)
        m_ref[...] = jnp.zeros_like(m_ref)

    row = _iota((cs, cs), 0)
    col = _iota((cs, cs), 1)
    causal = row >= col
    tril = causal.astype(F32)
    triu = (row <= col).astype(F32)
    eye = (row == col).astype(F32)
    gi = gt_ref[0:ML_HEADS, :]
    gf = gt_ref[ML_HEADS:2 * ML_HEADS, :]
    b_rows = jnp.dot(gf, triu, preferred_element_type=F32, precision=HIGHEST)
    b_cols = _dot_t(tril, gf, precision=HIGHEST)
    i_cols = _dot_t(eye, gi, precision=HIGHEST)
    heads = [slice(h * ML_D, (h + 1) * ML_D) for h in range(ML_HEADS)]
    hr = range(ML_HEADS)
    s_qk = [_dot_t(q_ref[:, sl], k_ref[:, sl]) for sl in heads]
    c_old = [c_ref[0, 0, h] for h in hr]
    q_c = [_dot(q_ref[:, sl], c_old[h].astype(BF16)) for h, sl in enumerate(heads)]
    m_old = [m_ref[0, :, h * 128:h * 128 + 1] for h in hr]
    b_col = [b_cols[:, h:h + 1] for h in hr]
    log_inter = [m_old[h] + b_col[h] for h in hr]
    log_w = [jnp.where(causal, b_col[h] - b_rows[h:h + 1, :] + gi[h:h + 1, :], NEG_LARGE) for h in hr]
    m = [jnp.maximum(log_inter[h], jnp.max(log_w[h], axis=1, keepdims=True)) for h in hr]
    qk = [s_qk[h] * jnp.where(causal, jnp.exp(jnp.where(causal, log_w[h] - m[h], 0.0)), 0.0) for h in hr]
    sc = [jnp.exp(log_inter[h] - m[h]) for h in hr]
    num = [_dot(qk[h].astype(BF16), v_ref[:, sl]) + sc[h] * q_c[h] for h, sl in enumerate(heads)]
    m_last = [m[h][cs - 1:cs] for h in hr]
    b_last = [b_col[h][cs - 1:cs] for h in hr]
    kw = [k_ref[:, sl].astype(F32) * jnp.exp(b_last[h] - b_col[h] + i_cols[:, h:h + 1] - m_last[h])
          for h, sl in enumerate(heads)]
    upd = [_dot(kw[h].T.astype(BF16), v_ref[:, sl]) for h, sl in enumerate(heads)]
    for h, sl in enumerate(heads):
        n_old = n_ref[0, :, sl]
        den = (jnp.sum(qk[h], axis=1, keepdims=True)
               + sc[h] * jnp.sum(q_ref[:, sl].astype(F32) * n_old, axis=1, keepdims=True))
        hid = num[h] / jnp.maximum(jnp.abs(den), jnp.exp(-m[h]))
        hid = og_ref[:, sl].astype(F32) * hid
        y_ref[:, sl] = _ln_gate(hid, gain_ref[:, sl], z_ref[:, sl].astype(F32))
        dec = jnp.exp(m_old[h] + b_last[h] - m_last[h])
        c_ref[0, 0, h] = dec * c_old[h] + upd[h]
        n_ref[0, :, sl] = dec * n_old + jnp.sum(kw[h], axis=0, keepdims=True)
        m_ref[0, :, h * 128:(h + 1) * 128] = jnp.broadcast_to(m_last[h], (1, 128))


def _mlstm_prompt(ps, pp, gt, gain3, layer, prev_out, *, bsz, seq, cs):
    nc = seq // cs
    tok = lambda b, c: b * nc + c
    in_specs = [
        pl.BlockSpec((cs, BW), lambda b, c: (tok(b, c), PLAIN_BLOCKS.index(_MQ))),
        pl.BlockSpec((cs, BW), lambda b, c: (tok(b, c), PLAIN_BLOCKS.index(_MK))),
        pl.BlockSpec((cs, BW), lambda b, c: (tok(b, c), PLAIN_BLOCKS.index(_MV))),
        pl.BlockSpec((cs, BW), lambda b, c: (tok(b, c), SILU_BLOCKS.index(_MZ))),
        pl.BlockSpec((cs, BW), lambda b, c: (tok(b, c), SILU_BLOCKS.index(_MO))),
        pl.BlockSpec((2 * ML_HEADS, cs), lambda b, c: (0, tok(b, c))),
        pl.BlockSpec((None, 1, BW), lambda b, c: (layer, 0, 0)),
    ]
    args = [pp, pp, pp, ps, ps, gt, gain3]
    c_spec, c_shape, x_specs, x_args, aliases = _state_chain(
        (DEPTH, bsz, ML_HEADS, ML_D, ML_D), (1, 1, ML_HEADS, ML_D, ML_D), lambda b, c: (layer, b, 0, 0, 0),
        prev_out, len(args), 1)
    return pl.pallas_call(
        _mlstm_kernel,
        grid=(bsz, nc),
        in_specs=in_specs + x_specs,
        out_specs=[
            pl.BlockSpec((cs, BW), lambda b, c: (tok(b, c), 0)),
            c_spec,
            pl.BlockSpec((1, 1, BW), lambda b, c: (b, 0, 0)),
            pl.BlockSpec((1, 1, ML_HEADS * 128), lambda b, c: (b, 0, 0)),
        ],
        out_shape=[
            jax.ShapeDtypeStruct((bsz * seq, BW), BF16),
            c_shape,
            jax.ShapeDtypeStruct((bsz, 1, BW), F32),
            jax.ShapeDtypeStruct((bsz, 1, ML_HEADS * 128), F32),
        ],
        input_output_aliases=aliases,
        compiler_params=_cparams(2),
        name="mlstm_prompt",
    )(*args, *x_args)


def _merge_kernel(xb_ref, ya_ref, yb_ref, yc_ref, wg0_ref, wg1_ref, wg2_ref, wbr_ref, out_ref):
    xb = xb_ref[...]
    for s in range(out_ref.shape[1] // V7X_MXU_COLS):
        cols = slice(s * V7X_MXU_COLS, (s + 1) * V7X_MXU_COLS)
        merged = None
        for n, (y_ref, wg_ref) in enumerate(((ya_ref, wg0_ref), (yb_ref, wg1_ref), (yc_ref, wg2_ref))):
            term = _sigmoid(_dot(xb, wg_ref[:, cols])) * _dot(y_ref[...], wbr_ref[n, :, cols])
            merged = term if merged is None else merged + term
        out_ref[:, cols] = merged.astype(BF16)


def _merge(xb, ya, yb, yc, w_gate, w_br, layer, *, tm, tn):
    t = xb.shape[0]
    nj = D_MODEL // tn
    gate_spec = lambda n: pl.BlockSpec((None, D_MODEL, tn), lambda i, j: (layer, 0, n * nj + j))
    return pl.pallas_call(
        _merge_kernel,
        grid=(t // tm, nj),
        in_specs=[
            pl.BlockSpec((tm, D_MODEL), lambda i, j: (i, 0)),
            pl.BlockSpec((tm, BW), lambda i, j: (i, 0)),
            pl.BlockSpec((tm, BW), lambda i, j: (i, 0)),
            pl.BlockSpec((tm, BW), lambda i, j: (i, 0)),
            gate_spec(0), gate_spec(1), gate_spec(2),
            pl.BlockSpec((None, 3, BW, tn), lambda i, j: (layer, 0, 0, j)),
        ],
        out_specs=pl.BlockSpec((tm, tn), lambda i, j: (i, j)),
        out_shape=jax.ShapeDtypeStruct((t, D_MODEL), BF16),
        compiler_params=_cparams(2),
        name="merge",
    )(xb, ya, yb, yc, w_gate, w_gate, w_gate, w_br)


def _outproj_kernel(m_ref, x_ref, wo_ref, lng_ref, lnb_ref, xo_ref, xbo_ref):
    for s in range(D_MODEL // V7X_MXU_COLS):
        cols = slice(s * V7X_MXU_COLS, (s + 1) * V7X_MXU_COLS)
        xo_ref[:, cols] = DEEPNORM_ALPHA * x_ref[:, cols] + _dot(m_ref[...], wo_ref[:, cols])
    hres = xo_ref[...]
    mu = jnp.mean(hres, axis=-1, keepdims=True)
    d = hres - mu
    var = jnp.mean(d * d, axis=-1, keepdims=True)
    x_new = d * lax.rsqrt(var + LN_EPS) * lng_ref[...] + lnb_ref[...]
    xo_ref[...] = x_new
    xbo_ref[...] = x_new.astype(BF16)


def _outproj(merged, x, w_o, lng3, lnb3, layer, *, tm):
    t = x.shape[0]
    return pl.pallas_call(
        _outproj_kernel,
        grid=(t // tm,),
        in_specs=[
            pl.BlockSpec((tm, D_MODEL), lambda i: (i, 0)),
            pl.BlockSpec((tm, D_MODEL), lambda i: (i, 0)),
            pl.BlockSpec((None, D_MODEL, D_MODEL), lambda i: (layer, 0, 0)),
            pl.BlockSpec((None, 1, D_MODEL), lambda i: (layer, 0, 0)),
            pl.BlockSpec((None, 1, D_MODEL), lambda i: (layer, 0, 0)),
        ],
        out_specs=[pl.BlockSpec((tm, D_MODEL), lambda i: (i, 0)), pl.BlockSpec((tm, D_MODEL), lambda i: (i, 0))],
        out_shape=[jax.ShapeDtypeStruct((t, D_MODEL), F32), jax.ShapeDtypeStruct((t, D_MODEL), BF16)],
        compiler_params=_cparams(1),
        name="outproj",
    )(merged, x, w_o, lng3, lnb3)


DEC_NB = 8


def _columns(x):
    d = x.shape[1]
    eye = (_iota((d, d), 0) == _iota((d, d), 1)).astype(F32)
    return _dot_t(eye, x, precision=HIGHEST)


def _pad_rows_bf16(x):
    return jnp.concatenate([x, jnp.zeros_like(x)], axis=0).astype(BF16)


def _bf16_columns(x16):
    d = x16.shape[1]
    eye = jnp.where(_iota((d, d), 0) == _iota((d, d), 1), 1.0, 0.0).astype(BF16)
    return _dot_t(eye, x16).astype(BF16)


def _rank1_updates(s_ref, so_ref, o_scr, h, sl, decays, k_cols16, v16, q16):
    d = k_cols16.shape[0]
    lane = _iota(k_cols16.shape, 1)
    k_stack = jnp.concatenate(
        [jnp.where(lane == j, k_cols16, jnp.zeros_like(k_cols16)) for j in range(DEC_NB)], axis=0)
    upd = _dot(k_stack, v16)
    s_bf = []
    for j in range(DEC_NB):
        s_new = decays[j] * s_ref[0, j, h] + upd[j * d:(j + 1) * d]
        so_ref[0, j, h] = s_new
        s_bf.append(s_new.astype(BF16))
    for j in range(DEC_NB):
        o_scr[j:j + 1, sl] = _dot(q16, s_bf[j])[j:j + 1, :]


def _hgrn_dec_kernel(q_ref, g_ref, v_ref, z_ref, gain_ref, s_ref, *rest):
    y_ref, so_ref, o_scr = rest[-3:]
    for h in range(HG_HEADS):
        sl = slice(h * HG_D, (h + 1) * HG_D)
        f = jnp.exp(g_ref[:, sl])
        f_cols = _columns(f)
        k_cols = _columns(1.0 - f)
        q_cols = _columns(q_ref[:, sl].astype(F32))
        v_rows = v_ref[:, sl].astype(F32)
        for j in range(DEC_NB):
            s_new = f_cols[:, j:j + 1] * s_ref[0, j, h] + k_cols[:, j:j + 1] * v_rows[j:j + 1, :]
            so_ref[0, j, h] = s_new
            o_scr[j:j + 1, sl] = jnp.sum(q_cols[:, j:j + 1] * s_new, axis=0, keepdims=True)
        y_ref[:, sl] = _rms_gate(o_scr[:, sl], gain_ref[:, sl], z_ref[:, sl].astype(F32))


def _ret_dec_kernel(q_ref, k_ref, v_ref, z_ref, gain_ref, s_ref, *rest):
    y_ref, so_ref, o_scr = rest[-3:]
    for h in range(RET_HEADS):
        sl = slice(h * RET_D, (h + 1) * RET_D)
        gamma = 1.0 - 2.0 ** (-5.0 - h)
        k_cols16 = _bf16_columns(_pad_rows_bf16(k_ref[:, sl].astype(F32)))
        q16 = _pad_rows_bf16(q_ref[:, sl].astype(F32))
        v16 = _pad_rows_bf16(v_ref[:, sl].astype(F32))
        _rank1_updates(s_ref, so_ref, o_scr, h, sl, [gamma] * DEC_NB, k_cols16, v16, q16)
        y_ref[:, sl] = _ln_gate(o_scr[:, sl], gain_ref[:, sl], z_ref[:, sl].astype(F32))


def _mlstm_gate_dec_kernel(gt_ref, m_ref, mo_ref, w_ref, sc_ref, eps_ref):
    gi = gt_ref[0:ML_HEADS, :]
    gf = gt_ref[ML_HEADS:2 * ML_HEADS, :]
    log_inter = m_ref[...] + gf
    m_new = jnp.maximum(log_inter, gi)
    mo_ref[...] = m_new
    w_ref[...] = jnp.exp(gi - m_new)
    sc_ref[...] = jnp.exp(log_inter - m_new)
    eps_ref[...] = jnp.exp(-m_new)


def _mlstm_gate_dec(gt, m_t):
    shp = jax.ShapeDtypeStruct(m_t.shape, F32)
    return pl.pallas_call(_mlstm_gate_dec_kernel, out_shape=[shp, shp, shp, shp], name="mlstm_gate_dec")(gt, m_t)


def _mlstm_dec_kernel(w_sm, sc_sm, eps_sm, q_ref, k_ref, v_ref, z_ref, og_ref, gain_ref, n_ref, c_ref, *rest, bsz):
    y_ref, no_ref, co_ref, o_scr, kw_scr, sc_scr, eps_scr = rest[-7:]
    i = pl.program_id(0)
    for h in range(ML_HEADS):
        sl = slice(h * ML_D, (h + 1) * ML_D)
        base = h * bsz + i * DEC_NB
        qf = q_ref[:, sl].astype(F32)
        k_rows = k_ref[:, sl].astype(F32)
        for j in range(DEC_NB):
            kw_scr[j:j + 1, :] = k_rows[j:j + 1, :] * w_sm[base + j]
            sc_scr[j:j + 1, :] = jnp.full((1, 128), sc_sm[base + j], F32)
            eps_scr[j:j + 1, :] = jnp.full((1, 128), eps_sm[base + j], F32)
        kw_rows = kw_scr[...]
        kw_cols16 = _bf16_columns(_pad_rows_bf16(kw_rows))
        q16 = _pad_rows_bf16(qf)
        v16 = _pad_rows_bf16(v_ref[:, sl].astype(F32))
        _rank1_updates(c_ref, co_ref, o_scr, h, sl, [sc_sm[base + j] for j in range(DEC_NB)], kw_cols16, v16, q16)
        n_new = sc_scr[:, 0:1] * n_ref[:, sl] + kw_rows
        no_ref[:, sl] = n_new
        den = jnp.sum(qf * n_new, axis=1, keepdims=True)
        hid = o_scr[:, sl] / jnp.maximum(jnp.abs(den), eps_scr[:, 0:1])
        hid = og_ref[:, sl].astype(F32) * hid
        y_ref[:, sl] = _ln_gate(hid, gain_ref[:, sl], z_ref[:, sl].astype(F32))


def _hgrn_dec(ps, g, pp, gain3, state, layer, prev_out):
    bsz = ps.shape[0]
    in_specs = [
        pl.BlockSpec((DEC_NB, BW), lambda i: (i, SILU_BLOCKS.index(_HQ))),
        pl.BlockSpec((DEC_NB, BW), lambda i: (i, 0)),
        pl.BlockSpec((DEC_NB, BW), lambda i: (i, PLAIN_BLOCKS.index(_HI))),
        pl.BlockSpec((DEC_NB, BW), lambda i: (i, SILU_BLOCKS.index(_HZ))),
        pl.BlockSpec((None, 1, BW), lambda i: (layer, 0, 0)),
        pl.BlockSpec((1, DEC_NB, HG_HEADS, HG_D, HG_D), lambda i: (layer, i, 0, 0, 0)),
    ]
    args = [ps, g, pp, ps, gain3, state]
    s_spec, s_shape, x_specs, x_args, aliases = _state_chain(
        state.shape, (1, DEC_NB, HG_HEADS, HG_D, HG_D), lambda i: (layer, i, 0, 0, 0), prev_out, len(args), 1)
    return pl.pallas_call(
        _hgrn_dec_kernel,
        grid=(bsz // DEC_NB,),
        in_specs=in_specs + x_specs,
        out_specs=[pl.BlockSpec((DEC_NB, BW), lambda i: (i, 0)), s_spec],
        out_shape=[jax.ShapeDtypeStruct((bsz, BW), BF16), s_shape],
        scratch_shapes=[pltpu.VMEM((DEC_NB, BW), F32)],
        input_output_aliases=aliases,
        compiler_params=_cparams(1),
        name="hgrn_dec",
    )(*args, *x_args)


def _ret_dec(ps, pr, pp, gain3, state, layer, prev_out):
    bsz = ps.shape[0]
    d, nh = RET_D, RET_HEADS
    in_specs = [
        pl.BlockSpec((DEC_NB, BW), lambda i: (i, ROT_BLOCKS.index(_RQ))),
        pl.BlockSpec((DEC_NB, BW), lambda i: (i, ROT_BLOCKS.index(_RK))),
        pl.BlockSpec((DEC_NB, BW), lambda i: (i, PLAIN_BLOCKS.index(_RV))),
        pl.BlockSpec((DEC_NB, BW), lambda i: (i, SILU_BLOCKS.index(_RZ))),
        pl.BlockSpec((None, 1, BW), lambda i: (layer, 0, 0)),
        pl.BlockSpec((1, DEC_NB, nh, d, d), lambda i: (layer, i, 0, 0, 0)),
    ]
    args = [pr, pr, pp, ps, gain3, state]
    s_spec, s_shape, x_specs, x_args, aliases = _state_chain(
        state.shape, (1, DEC_NB, nh, d, d), lambda i: (layer, i, 0, 0, 0), prev_out, len(args), 1)
    return pl.pallas_call(
        _ret_dec_kernel,
        grid=(bsz // DEC_NB,),
        in_specs=in_specs + x_specs,
        out_specs=[pl.BlockSpec((DEC_NB, BW), lambda i: (i, 0)), s_spec],
        out_shape=[jax.ShapeDtypeStruct((bsz, BW), BF16), s_shape],
        scratch_shapes=[pltpu.VMEM((DEC_NB, BW), F32)],
        input_output_aliases=aliases,
        compiler_params=_cparams(1),
        name="ret_dec",
    )(*args, *x_args)


def _mlstm_dec(ps, pp, gain3, w_flat, sc_flat, eps_flat, n_state, c_state, layer, prev_out):
    bsz = ps.shape[0]
    d, nh = ML_D, ML_HEADS
    row = lambda blocks, blk: pl.BlockSpec((DEC_NB, BW), lambda i, *_: (i, blocks.index(blk)))
    in_specs = [
        row(PLAIN_BLOCKS, _MQ), row(PLAIN_BLOCKS, _MK), row(PLAIN_BLOCKS, _MV), row(SILU_BLOCKS, _MZ),
        row(SILU_BLOCKS, _MO),
        pl.BlockSpec((None, 1, BW), lambda i, *_: (layer, 0, 0)),
        pl.BlockSpec((DEC_NB, BW), lambda i, *_: (i, 0)),
        pl.BlockSpec((1, DEC_NB, nh, d, d), lambda i, *_: (layer, i, 0, 0, 0)),
    ]
    args = [pp, pp, pp, ps, ps, gain3, n_state, c_state]
    n_prefetch = 3
    c_spec, c_shape, x_specs, x_args, aliases = _state_chain(
        c_state.shape, (1, DEC_NB, nh, d, d), lambda i, *_: (layer, i, 0, 0, 0), prev_out,
        n_prefetch + len(args), 2)
    grid_spec = pltpu.PrefetchScalarGridSpec(
        num_scalar_prefetch=n_prefetch,
        grid=(bsz // DEC_NB,),
        in_specs=in_specs + x_specs,
        out_specs=[pl.BlockSpec((DEC_NB, BW), lambda i, *_: (i, 0)),
                   pl.BlockSpec((DEC_NB, BW), lambda i, *_: (i, 0)),
                   c_spec],
        scratch_shapes=[pltpu.VMEM((DEC_NB, BW), F32), pltpu.VMEM((DEC_NB, d), F32),
                        pltpu.VMEM((DEC_NB, 128), F32), pltpu.VMEM((DEC_NB, 128), F32)],
    )
    return pl.pallas_call(
        functools.partial(_mlstm_dec_kernel, bsz=bsz),
        grid_spec=grid_spec,
        out_shape=[jax.ShapeDtypeStruct((bsz, BW), BF16), jax.ShapeDtypeStruct((bsz, BW), F32), c_shape],
        input_output_aliases=aliases,
        compiler_params=_cparams(1),
        name="mlstm_dec",
    )(w_flat, sc_flat, eps_flat, *args, *x_args)


def _rotary_tables(pos):
    theta = 1.0 / (ROPE_BASE ** jnp.linspace(0.0, 1.0, RET_D // 2, dtype=F32))
    ang = pos.astype(F32)[:, None] * theta[None]
    cos = jnp.repeat(jnp.cos(ang), 2, axis=1)
    sin = jnp.repeat(jnp.sin(ang), 2, axis=1)
    sign = jnp.tile(jnp.asarray([-1.0, 1.0], F32), RET_D // 2)
    return cos, sin * sign[None]


def _rows3(a):
    return a.astype(F32)[:, None, :]


def _tile_sizes(n_tokens, seq):
    return min(1024, seq), min(1024, n_tokens), 512, min(512, n_tokens)


def kernel(x_prompt, x_sample, state_hgrn, state_ret, state_mlstm_c, state_mlstm_n, state_mlstm_m,
           w_in, hgrn_lb_logits, hgrn_norm, ret_norm, mlstm_norm, mlstm_b_i, mlstm_b_f,
           w_branch, w_out, ln_g, ln_b):
    bp, lp, _ = x_prompt.shape
    bs, ls, _ = x_sample.shape
    assert ls == 1 and w_in.dtype == F32
    lbs3 = _lower_bounds(hgrn_lb_logits.astype(F32))[:, None, :]
    assert w_in.shape[-1] == GATE_COL0 + 3 * D_MODEL
    w_in = jnp.swapaxes(w_in, 1, 2)
    w_gate, w_if = _gate_prep(w_in)
    b_if = jnp.concatenate([mlstm_b_i, mlstm_b_f], axis=-1).astype(F32)[:, :, None]
    w_br = w_branch.astype(BF16)
    w_o = w_out.astype(BF16)
    hg_gain, ret_gain, ml_gain = _rows3(hgrn_norm), _rows3(ret_norm), _rows3(mlstm_norm)
    lng3, lnb3 = _rows3(ln_g), _rows3(ln_b)
    cos_p, sin_p = _rotary_tables(jnp.arange(lp))
    cos_s, sin_s = _rotary_tables(jnp.full((bs,), PAST_LEN))

    tp, tmm, tnm, tmo = _tile_sizes(bp * lp, lp)
    x = x_prompt.reshape(bp * lp, D_MODEL)
    xb = x.astype(BF16)
    hg_p = ret_p = mc_p = None
    mn_l, mm_l = [], []
    for l in range(DEPTH):
        g, gt, ps, pr, pp = _project_all(xb, w_in, l, lbs3, w_if, b_if, cos_p, sin_p, tm=tp)
        ya, hg_p = _hgrn_prompt(ps, g, pp, hg_gain, l, hg_p, bsz=bp, seq=lp, cs=min(128, lp))
        yb, ret_p = _ret_prompt(ps, pr, pp, ret_gain, l, ret_p, bsz=bp, seq=lp, cs=min(128, lp))
        yc, mc_p, mn, mm = _mlstm_prompt(ps, pp, gt, ml_gain, l, mc_p, bsz=bp, seq=lp, cs=min(128, lp))
        merged = _merge(xb, ya, yb, yc, w_gate, w_br, l, tm=tmm, tn=tnm)
        x, xb = _outproj(merged, x, w_o, lng3, lnb3, l, tm=tmo)
        mn_l.append(mn.reshape(bp, ML_HEADS, ML_D))
        mm_l.append(mm.reshape(bp, ML_HEADS, 128)[:, :, 0])
    y_prompt = x.reshape(bp, lp, D_MODEL)

    ts, tmm, tnm, tmo = _tile_sizes(bs, bs)
    x = x_sample.reshape(bs, D_MODEL)
    xb = x.astype(BF16)
    hg_s = ret_s = mc_s = None
    mn_sl, mm_sl = [], []
    for l in range(DEPTH):
        g, gt, ps, pr, pp = _project_all(xb, w_in, l, lbs3, w_if, b_if, cos_s, sin_s, tm=ts)
        ya, hg_s = _hgrn_dec(ps, g, pp, hg_gain, state_hgrn, l, hg_s)
        yb, ret_s = _ret_dec(ps, pr, pp, ret_gain, state_ret, l, ret_s)
        m_new, w_t, sc_t, eps_t = _mlstm_gate_dec(gt, state_mlstm_m[l].T)
        yc, mn, mc_s = _mlstm_dec(ps, pp, ml_gain, w_t.reshape(-1), sc_t.reshape(-1), eps_t.reshape(-1),
                                  state_mlstm_n[l].reshape(bs, BW), state_mlstm_c, l, mc_s)
        merged = _merge(xb, ya, yb, yc, w_gate, w_br, l, tm=tmm, tn=tnm)
        x, xb = _outproj(merged, x, w_o, lng3, lnb3, l, tm=tmo)
        mn_sl.append(mn.reshape(bs, ML_HEADS, ML_D))
        mm_sl.append(m_new.T)
    y_sample = x.reshape(bs, 1, D_MODEL)

    return (y_prompt, y_sample, hg_p, ret_p, mc_p, jnp.stack(mn_l), jnp.stack(mm_l),
            hg_s, ret_s, mc_s, jnp.stack(mn_sl), jnp.stack(mm_sl))
```

```python
import functools

import jax
import jax.numpy as jnp
from jax import lax
from jax.experimental import pallas as pl
from jax.experimental.pallas import tpu as pltpu

F32 = jnp.float32
BF16 = jnp.bfloat16
HIGHEST = lax.Precision.HIGHEST

D_MODEL = 2048
DEPTH = 4
PAST_LEN = 16384
BW = D_MODEL // 2
HG_HEADS, HG_D = 8, 128
RET_HEADS, RET_D = 4, 256
ML_HEADS, ML_D = 4, 256
HG_LB_FLOOR = 1e-30
ROPE_BASE = 10000.0
NEG_LARGE = -1e30
NORM_EPS = 1e-6
LN_EPS = 1e-5
DEEPNORM_ALPHA = (2 * DEPTH) ** 0.25

_HQ, _HF, _HI, _HZ, _RQ, _RK, _RV, _RZ, _MQ, _MK, _MV, _MZ, _MO = range(13)
SILU_BLOCKS = (_HQ, _HZ, _RZ, _MZ, _MO)
ROT_BLOCKS = (_RQ, _RK)
PLAIN_BLOCKS = (_HI, _RV, _MQ, _MK, _MV)
IF_COL0 = 13 * BW
GATE_COL0 = 13 * BW + 2 * ML_HEADS

V7X_VMEM_LIMIT = 56 * 1024 * 1024
V7X_MXU_COLS = 256


def _cparams(n_grid):
    return pltpu.CompilerParams(dimension_semantics=("arbitrary",) * n_grid, vmem_limit_bytes=V7X_VMEM_LIMIT)


def _sigmoid(x):
    return 1.0 / (1.0 + jnp.exp(-x))


def _silu(x):
    return x * _sigmoid(x)


def _log_sigmoid(x):
    return jnp.minimum(x, 0.0) - jnp.log1p(jnp.exp(-jnp.abs(x)))


def _iota(shape, dim):
    return lax.broadcasted_iota(jnp.int32, shape, dim)


def _dot(a, b):
    return jnp.dot(a, b, preferred_element_type=F32)


def _dot_t(a, b, precision=None):
    return lax.dot_general(a, b, (((1,), (1,)), ((), ())), preferred_element_type=F32, precision=precision)


def _lookup(j, table):
    out = table[0]
    for k in range(1, len(table)):
        out = jnp.where(j == k, table[k], out)
    return out


def _lb_kernel(logit_ref, lb_ref):
    z = logit_ref[...]
    rows = [z[l:l + 1] for l in range(DEPTH)]
    mx = functools.reduce(jnp.maximum, rows)
    ex = [jnp.exp(r - mx) for r in rows]
    tot = functools.reduce(lambda a, b: a + b, ex)
    sm = [e / tot for e in ex]
    run = sm[0]
    lb_ref[0:1, :] = run - sm[0]
    for l in range(1, DEPTH):
        run = run + sm[l]
        lb_ref[l:l + 1, :] = run - sm[0]


def _lower_bounds(logits):
    return pl.pallas_call(_lb_kernel, out_shape=jax.ShapeDtypeStruct(logits.shape, F32), name="lower_bounds")(logits)


GATE_PREP_TN = 512
XPOSE_CHUNK = 256


def _transpose_cast(w_ref, wb_ref):
    for r in range(w_ref.shape[1] // XPOSE_CHUNK):
        ks = slice(r * XPOSE_CHUNK, (r + 1) * XPOSE_CHUNK)
        wb_ref[ks, :] = w_ref[:, ks].T.astype(BF16)


def _gate_prep_kernel(wt_ref, wif_rows_ref, wg_ref, wif_ref):
    _transpose_cast(wt_ref, wg_ref)

    @pl.when(pl.program_id(1) == 0)
    def _():
        rows = wif_rows_ref[...]
        padded = jnp.concatenate([rows, jnp.zeros((128 - rows.shape[0], D_MODEL), F32)], axis=0)
        wif_ref[...] = padded.T.astype(BF16)


def _gate_prep(w_t):
    tn = GATE_PREP_TN
    assert GATE_COL0 % 8 == 0 and IF_COL0 % 8 == 0
    return pl.pallas_call(
        _gate_prep_kernel,
        grid=(DEPTH, 3 * D_MODEL // tn),
        in_specs=[pl.BlockSpec((None, pl.Element(tn), pl.Element(D_MODEL)), lambda l, j: (l, pl.multiple_of(GATE_COL0 + j * tn, 8), 0)),
                  pl.BlockSpec((None, 2 * ML_HEADS, D_MODEL), lambda l, j: (l, IF_COL0 // (2 * ML_HEADS), 0))],
        out_specs=[pl.BlockSpec((None, D_MODEL, tn), lambda l, j: (l, 0, j)),
                   pl.BlockSpec((None, D_MODEL, 128), lambda l, j: (l, 0, 0))],
        out_shape=[jax.ShapeDtypeStruct((DEPTH, D_MODEL, 3 * D_MODEL), BF16),
                   jax.ShapeDtypeStruct((DEPTH, D_MODEL, 128), BF16)],
        compiler_params=_cparams(2),
        name="gate_prep",
    )(w_t, w_t)


def _proj_body(xb_ref, wb_ref, out_ref, epilogue):
    for s in range(wb_ref.shape[1] // V7X_MXU_COLS):
        cols = slice(s * V7X_MXU_COLS, (s + 1) * V7X_MXU_COLS)
        out_ref[:, cols] = epilogue(_dot(xb_ref[...], wb_ref[:, cols]), cols).astype(out_ref.dtype)


def _cast_weights(w_ref, wb_ref):
    @pl.when(pl.program_id(1) == 0)
    def _():
        _transpose_cast(w_ref, wb_ref)


def _proj_silu_kernel(xb_ref, w_ref, out_ref, wb_ref, *, sigmoid_block):
    _cast_weights(w_ref, wb_ref)
    plain_sigmoid = pl.program_id(0) == sigmoid_block

    def epilogue(acc, cols):
        s = _sigmoid(acc)
        return jnp.where(plain_sigmoid, s, acc * s)

    _proj_body(xb_ref, wb_ref, out_ref, epilogue)


def _proj_plain_kernel(xb_ref, w_ref, out_ref, wb_ref, *, scaled_block, scale):
    _cast_weights(w_ref, wb_ref)
    sc = jnp.where(pl.program_id(0) == scaled_block, scale, 1.0).astype(F32)
    _proj_body(xb_ref, wb_ref, out_ref, lambda acc, cols: acc * sc)


def _proj_rotary_kernel(xb_ref, w_ref, cos_ref, sin_ref, out_ref, wb_ref, *, scaled_block, scale):
    _cast_weights(w_ref, wb_ref)
    sc = jnp.where(pl.program_id(0) == scaled_block, scale, 1.0).astype(F32)
    assert V7X_MXU_COLS == RET_D

    def epilogue(acc, cols):
        even = (_iota(acc.shape, 1) % 2) == 0
        swapped = jnp.where(even, pltpu.roll(acc, RET_D - 1, 1), pltpu.roll(acc, 1, 1))
        return (acc * cos_ref[...] + swapped * sin_ref[...]) * sc

    _proj_body(xb_ref, wb_ref, out_ref, epilogue)


def _proj_logf_kernel(xb_ref, w_ref, lb_ref, wif_ref, bif_ref, g_ref, gt_ref, wb_ref):
    _cast_weights(w_ref, wb_ref)
    pt = _dot(xb_ref[...], wif_ref[...]).T[0:2 * ML_HEADS, :] + bif_ref[...]
    gt_ref[...] = jnp.where(_iota(pt.shape, 0) < ML_HEADS, pt, _log_sigmoid(pt))

    def epilogue(acc, cols):
        lb = lb_ref[:, cols]
        return jnp.log(jnp.maximum(lb, HG_LB_FLOOR) + (1.0 - lb) * _sigmoid(acc))

    _proj_body(xb_ref, wb_ref, g_ref, epilogue)


def _proj(kern, xb, w_in, layer, blocks, extra_in, extra_specs, out_dtype, *, tm, extra_out=(), extra_out_specs=()):
    t = xb.shape[0]
    out_shape = [jax.ShapeDtypeStruct((t, len(blocks) * BW), out_dtype)] + list(extra_out)
    out_specs = [pl.BlockSpec((tm, BW), lambda j, i: (i, j))] + list(extra_out_specs)
    res = pl.pallas_call(
        kern,
        grid=(len(blocks), t // tm),
        in_specs=[pl.BlockSpec((tm, D_MODEL), lambda j, i: (i, 0)),
                  pl.BlockSpec((None, BW, D_MODEL), lambda j, i: (layer, _lookup(j, blocks), 0))] + list(extra_specs),
        out_specs=out_specs,
        out_shape=out_shape,
        scratch_shapes=[pltpu.VMEM((D_MODEL, BW), BF16)],
        compiler_params=_cparams(2),
        name="proj",
    )(xb, w_in, *extra_in)
    return res


def _project_all(xb, w_in, layer, lbs3, w_if, b_if, cos_t, sin_t, *, tm):
    t = xb.shape[0]
    n_pos = cos_t.shape[0] // tm
    g, gt = _proj(
        _proj_logf_kernel, xb, w_in, layer, (_HF,),
        (lbs3, w_if, b_if),
        (pl.BlockSpec((None, 1, BW), lambda j, i: (layer, 0, 0)),
         pl.BlockSpec((None, D_MODEL, 128), lambda j, i: (layer, 0, 0)),
         pl.BlockSpec((None, 2 * ML_HEADS, 1), lambda j, i: (layer, 0, 0))),
        F32, tm=tm,
        extra_out=(jax.ShapeDtypeStruct((2 * ML_HEADS, t), F32),),
        extra_out_specs=(pl.BlockSpec((2 * ML_HEADS, tm), lambda j, i: (0, i)),))
    ps, = _proj(functools.partial(_proj_silu_kernel, sigmoid_block=SILU_BLOCKS.index(_MO)),
                xb, w_in, layer, SILU_BLOCKS, (), (), BF16, tm=tm)
    pr, = _proj(functools.partial(_proj_rotary_kernel, scaled_block=ROT_BLOCKS.index(_RK), scale=RET_D ** -0.5),
                xb, w_in, layer, ROT_BLOCKS, (cos_t, sin_t),
                (pl.BlockSpec((tm, RET_D), lambda j, i: (i % n_pos, 0)),
                 pl.BlockSpec((tm, RET_D), lambda j, i: (i % n_pos, 0))), BF16, tm=tm)
    pp, = _proj(functools.partial(_proj_plain_kernel, scaled_block=PLAIN_BLOCKS.index(_MK), scale=ML_D ** -0.5),
                xb, w_in, layer, PLAIN_BLOCKS, (), (), BF16, tm=tm)
    return g, gt, ps, pr, pp


def _rms_gate(o, gain, zs):
    y = o * lax.rsqrt(jnp.mean(o * o, axis=-1, keepdims=True) + NORM_EPS)
    return (y * gain * zs).astype(BF16)


def _ln_gate(o, gain, zs):
    mu = jnp.mean(o, axis=-1, keepdims=True)
    d = o - mu
    var = jnp.mean(d * d, axis=-1, keepdims=True)
    return (d * lax.rsqrt(var + NORM_EPS) * gain * zs).astype(BF16)


def _state_chain(state_shape, block, index_map, prev_out, n_in, out_pos):
    spec = pl.BlockSpec(block, index_map)
    shape = jax.ShapeDtypeStruct(state_shape, F32)
    if prev_out is None:
        return spec, shape, [], [], {}
    return spec, shape, [pl.BlockSpec(memory_space=pl.ANY)], [prev_out], {n_in: out_pos}


def _hgrn_level_ref(bc, m):
    cs, d = bc.shape
    if m >= 8:
        parts = [jnp.broadcast_to(bc[b * 2 * m + m - 1:b * 2 * m + m], (2 * m, d)) for b in range(cs // (2 * m))]
        return parts[0] if len(parts) == 1 else jnp.concatenate(parts, axis=0)
    x3 = bc.reshape(cs // 8, 8, d)
    sub = _iota((cs // 8, 8, d), 1)
    pick = lambda r: jnp.broadcast_to(x3[:, r:r + 1, :], x3.shape)
    if m == 4:
        r3 = pick(3)
    elif m == 2:
        r3 = jnp.where(sub < 4, pick(1), pick(5))
    else:
        r3 = jnp.where(sub < 2, pick(0), jnp.where(sub < 4, pick(2), jnp.where(sub < 6, pick(4), pick(6))))
    return r3.reshape(cs, d)


def _hgrn_stages(q_ref, g_ref, v_ref, z_ref, gain_ref, y_ref, s_ref):
    c = pl.program_id(1)
    cs = q_ref.shape[0]

    @pl.when(c == 0)
    def _():
        s_ref[...] = jnp.zeros_like(s_ref)

    row = _iota((cs, cs), 0)
    col = _iota((cs, cs), 1)
    tril = (row >= col).astype(F32)
    level = jnp.where(row > col, 31 - lax.clz(row ^ col), -1)
    eye_c = row == col
    eye_d = _iota((HG_D, HG_D), 0) == _iota((HG_D, HG_D), 1)
    levels = [cs >> (k + 1) for k in range(cs.bit_length() - 1)]
    heads = [slice(h * HG_D, (h + 1) * HG_D) for h in range(HG_HEADS)]
    q = q_ref[...].astype(F32)
    g = g_ref[...]
    kk = 1.0 - jnp.exp(g)
    bc = jnp.dot(tril, g, preferred_element_type=F32, precision=HIGHEST)
    qk = q * kk
    a = [jnp.where(eye_c, jnp.sum(qk[:, sl], axis=1, keepdims=True), 0.0) for sl in heads]
    yield
    for m in levels:
        e = jnp.exp(-jnp.abs(bc - _hgrn_level_ref(bc, m)))
        qs = (q * e).astype(BF16)
        ks = (kk * e).astype(BF16)
        sel = level == m.bit_length() - 1
        a = [jnp.where(sel, _dot_t(qs[:, sl], ks[:, sl]), a_h) for sl, a_h in zip(heads, a)]
        yield
    qe = (q * jnp.exp(bc)).astype(BF16)
    last = bc[cs - 1:cs]
    k_dec_t = (kk * jnp.exp(last - bc)).T.astype(BF16)
    dec_row = jnp.exp(last)
    yield
    for h, sl in enumerate(heads):
        v = v_ref[:, sl]
        s_old = s_ref[0, 0, h]
        o = _dot(qe[:, sl], s_old.astype(BF16)) + _dot(a[h].astype(BF16), v)
        y_ref[:, sl] = _rms_gate(o, gain_ref[:, sl], z_ref[:, sl].astype(F32))
        dec_col = jnp.sum(jnp.where(eye_d, dec_row[:, sl], 0.0), axis=1, keepdims=True)
        s_ref[0, 0, h] = dec_col * s_old + _dot(k_dec_t[sl, :], v)
        if h % 2 == 1:
            yield


def _ret_log_gamma(h):
    return jnp.log(jnp.full((1, 1), 1.0 - 2.0 ** (-5.0 - h), F32))


def _ret_stages(q_ref, k_ref, v_ref, z_ref, gain_ref, y_ref, s_ref):
    c = pl.program_id(1)
    cs = q_ref.shape[0]

    @pl.when(c == 0)
    def _():
        s_ref[...] = jnp.zeros_like(s_ref)

    t_col = _iota((cs, 1), 0).astype(F32)
    rel = (_iota((cs, cs), 0) - _iota((cs, cs), 1)).astype(F32)
    causal = rel >= 0.0
    heads = [slice(h * RET_D, (h + 1) * RET_D) for h in range(RET_HEADS)]
    hr = range(RET_HEADS)
    lg = [_ret_log_gamma(h) for h in hr]
    s_qk = [_dot_t(q_ref[:, sl], k_ref[:, sl]) for sl in heads]
    yield
    s_old = [s_ref[0, 0, h] for h in hr]
    q_s = [_dot(q_ref[:, sl], s_old[h].astype(BF16)) for h, sl in enumerate(heads)]
    yield
    k_dec_t = [(k_ref[:, sl].astype(F32) * jnp.exp((cs - 1.0 - t_col) * lg[h])).T.astype(BF16)
               for h, sl in enumerate(heads)]
    upd = [_dot(k_dec_t[h], v_ref[:, sl]) for h, sl in enumerate(heads)]
    yield
    a = [s_qk[h] * jnp.where(causal, jnp.exp(jnp.where(causal, rel * lg[h], 0.0)), 0.0) for h in hr]
    a_v = [_dot(a[h].astype(BF16), v_ref[:, sl]) for h, sl in enumerate(heads)]
    yield
    for h, sl in enumerate(heads):
        o = jnp.exp((t_col + 1.0) * lg[h]) * q_s[h] + a_v[h]
        y_ref[:, sl] = _ln_gate(o, gain_ref[:, sl], z_ref[:, sl].astype(F32))
        s_ref[0, 0, h] = jnp.exp(cs * lg[h]) * s_old[h] + upd[h]
        if h % 2 == 1:
            yield


def _mlstm_stages(q_ref, k_ref, v_ref, z_ref, og_ref, gt_ref, gain_ref, y_ref, c_ref, n_ref, m_ref):
    c = pl.program_id(1)
    cs = q_ref.shape[0]

    @pl.when(c == 0)
    def _():
        c_ref[...] = jnp.zeros_like(c_ref)
        n_ref[...] = jnp.zeros_like(n_ref)
        m_ref[...] = jnp.zeros_like(m_ref)

    row = _iota((cs, cs), 0)
    col = _iota((cs, cs), 1)
    causal = row >= col
    tril = causal.astype(F32)
    triu = (row <= col).astype(F32)
    eye = (row == col).astype(F32)
    gi = gt_ref[0:ML_HEADS, :]
    gf = gt_ref[ML_HEADS:2 * ML_HEADS, :]
    b_rows = jnp.dot(gf, triu, preferred_element_type=F32, precision=HIGHEST)
    b_cols = _dot_t(tril, gf, precision=HIGHEST)
    i_cols = _dot_t(eye, gi, precision=HIGHEST)
    heads = [slice(h * ML_D, (h + 1) * ML_D) for h in range(ML_HEADS)]
    hr = range(ML_HEADS)
    s_qk = [_dot_t(q_ref[:, sl], k_ref[:, sl]) for sl in heads]
    yield
    c_old = [c_ref[0, 0, h] for h in hr]
    q_c = [_dot(q_ref[:, sl], c_old[h].astype(BF16)) for h, sl in enumerate(heads)]
    yield
    m_old = [m_ref[0, :, h * 128:h * 128 + 1] for h in hr]
    b_col = [b_cols[:, h:h + 1] for h in hr]
    log_inter = [m_old[h] + b_col[h] for h in hr]
    log_w = [jnp.where(causal, b_col[h] - b_rows[h:h + 1, :] + gi[h:h + 1, :], NEG_LARGE) for h in hr]
    m = [jnp.maximum(log_inter[h], jnp.max(log_w[h], axis=1, keepdims=True)) for h in hr]
    yield
    qk = [s_qk[h] * jnp.where(causal, jnp.exp(jnp.where(causal, log_w[h] - m[h], 0.0)), 0.0) for h in hr]
    sc = [jnp.exp(log_inter[h] - m[h]) for h in hr]
    num = [_dot(qk[h].astype(BF16), v_ref[:, sl]) + sc[h] * q_c[h] for h, sl in enumerate(heads)]
    yield
    m_last = [m[h][cs - 1:cs] for h in hr]
    b_last = [b_col[h][cs - 1:cs] for h in hr]
    kw = [k_ref[:, sl].astype(F32) * jnp.exp(b_last[h] - b_col[h] + i_cols[:, h:h + 1] - m_last[h])
          for h, sl in enumerate(heads)]
    upd = [_dot(kw[h].T.astype(BF16), v_ref[:, sl]) for h, sl in enumerate(heads)]
    yield
    for h, sl in enumerate(heads):
        n_old = n_ref[0, :, sl]
        den = (jnp.sum(qk[h], axis=1, keepdims=True)
               + sc[h] * jnp.sum(q_ref[:, sl].astype(F32) * n_old, axis=1, keepdims=True))
        hid = num[h] / jnp.maximum(jnp.abs(den), jnp.exp(-m[h]))
        hid = og_ref[:, sl].astype(F32) * hid
        y_ref[:, sl] = _ln_gate(hid, gain_ref[:, sl], z_ref[:, sl].astype(F32))
        dec = jnp.exp(m_old[h] + b_last[h] - m_last[h])
        c_ref[0, 0, h] = dec * c_old[h] + upd[h]
        n_ref[0, :, sl] = dec * n_old + jnp.sum(kw[h], axis=0, keepdims=True)
        m_ref[0, :, h * 128:(h + 1) * 128] = jnp.broadcast_to(m_last[h], (1, 128))
        if h % 2 == 1:
            yield


N_SCAN_IN = 17


def _scan_kernel(*refs):
    ins, outs = refs[:N_SCAN_IN], refs[-8:]
    stages = [_hgrn_stages(*ins[0:5], *outs[0:2]), _ret_stages(*ins[5:10], *outs[2:4]),
              _mlstm_stages(*ins[10:17], *outs[4:8])]
    while stages:
        stages = [s for s in stages if next(s, True) is None]


def _scan_prompt(g, gt, ps, pr, pp, hg_gain, ret_gain, ml_gain, layer, prev, *, bsz, seq, cs):
    nc = seq // cs
    tok = lambda b, c: b * nc + c
    col = lambda blocks, blk: pl.BlockSpec((cs, BW), lambda b, c: (tok(b, c), blocks.index(blk)))
    gain = pl.BlockSpec((None, 1, BW), lambda b, c: (layer, 0, 0))
    in_specs = [
        col(SILU_BLOCKS, _HQ), pl.BlockSpec((cs, BW), lambda b, c: (tok(b, c), 0)), col(PLAIN_BLOCKS, _HI),
        col(SILU_BLOCKS, _HZ), gain,
        col(ROT_BLOCKS, _RQ), col(ROT_BLOCKS, _RK), col(PLAIN_BLOCKS, _RV), col(SILU_BLOCKS, _RZ), gain,
        col(PLAIN_BLOCKS, _MQ), col(PLAIN_BLOCKS, _MK), col(PLAIN_BLOCKS, _MV), col(SILU_BLOCKS, _MZ),
        col(SILU_BLOCKS, _MO), pl.BlockSpec((2 * ML_HEADS, cs), lambda b, c: (0, tok(b, c))), gain,
    ]
    args = [ps, g, pp, ps, hg_gain, pr, pr, pp, ps, ret_gain, pp, pp, pp, ps, ps, gt, ml_gain]
    assert len(args) == N_SCAN_IN
    y_spec = pl.BlockSpec((cs, BW), lambda b, c: (tok(b, c), 0))
    y_shape = jax.ShapeDtypeStruct((bsz * seq, BW), BF16)
    state_specs, state_shapes, aliases = [], [], {}
    for k, (nh, d) in enumerate(((HG_HEADS, HG_D), (RET_HEADS, RET_D), (ML_HEADS, ML_D))):
        spec, shape, x_specs, x_args, alias = _state_chain(
            (DEPTH, bsz, nh, d, d), (1, 1, nh, d, d), lambda b, c: (layer, b, 0, 0, 0),
            None if prev is None else prev[k], len(args), 2 * k + 1)
        state_specs.append(spec)
        state_shapes.append(shape)
        in_specs = in_specs + x_specs
        args = args + x_args
        aliases.update(alias)
    return pl.pallas_call(
        _scan_kernel,
        grid=(bsz, nc),
        in_specs=in_specs,
        out_specs=[y_spec, state_specs[0], y_spec, state_specs[1], y_spec, state_specs[2],
                   pl.BlockSpec((1, 1, BW), lambda b, c: (b, 0, 0)),
                   pl.BlockSpec((1, 1, ML_HEADS * 128), lambda b, c: (b, 0, 0))],
        out_shape=[y_shape, state_shapes[0], y_shape, state_shapes[1], y_shape, state_shapes[2],
                   jax.ShapeDtypeStruct((bsz, 1, BW), F32),
                   jax.ShapeDtypeStruct((bsz, 1, ML_HEADS * 128), F32)],
        input_output_aliases=aliases,
        compiler_params=_cparams(2),
        name="scan_prompt",
    )(*args)


def _merge_kernel(xb_ref, ya_ref, yb_ref, yc_ref, wg0_ref, wg1_ref, wg2_ref, wbr_ref, out_ref):
    xb = xb_ref[...]
    for s in range(out_ref.shape[1] // V7X_MXU_COLS):
        cols = slice(s * V7X_MXU_COLS, (s + 1) * V7X_MXU_COLS)
        merged = None
        for n, (y_ref, wg_ref) in enumerate(((ya_ref, wg0_ref), (yb_ref, wg1_ref), (yc_ref, wg2_ref))):
            term = _sigmoid(_dot(xb, wg_ref[:, cols])) * _dot(y_ref[...], wbr_ref[n, :, cols])
            merged = term if merged is None else merged + term
        out_ref[:, cols] = merged.astype(BF16)


def _merge(xb, ya, yb, yc, w_gate, w_br, layer, *, tm, tn):
    t = xb.shape[0]
    nj = D_MODEL // tn
    gate_spec = lambda n: pl.BlockSpec((None, D_MODEL, tn), lambda i, j: (layer, 0, n * nj + j))
    return pl.pallas_call(
        _merge_kernel,
        grid=(t // tm, nj),
        in_specs=[
            pl.BlockSpec((tm, D_MODEL), lambda i, j: (i, 0)),
            pl.BlockSpec((tm, BW), lambda i, j: (i, 0)),
            pl.BlockSpec((tm, BW), lambda i, j: (i, 0)),
            pl.BlockSpec((tm, BW), lambda i, j: (i, 0)),
            gate_spec(0), gate_spec(1), gate_spec(2),
            pl.BlockSpec((None, 3, BW, tn), lambda i, j: (layer, 0, 0, j)),
        ],
        out_specs=pl.BlockSpec((tm, tn), lambda i, j: (i, j)),
        out_shape=jax.ShapeDtypeStruct((t, D_MODEL), BF16),
        compiler_params=_cparams(2),
        name="merge",
    )(xb, ya, yb, yc, w_gate, w_gate, w_gate, w_br)


def _outproj_kernel(m_ref, x_ref, wo_ref, lng_ref, lnb_ref, xo_ref, xbo_ref):
    for s in range(D_MODEL // V7X_MXU_COLS):
        cols = slice(s * V7X_MXU_COLS, (s + 1) * V7X_MXU_COLS)
        xo_ref[:, cols] = DEEPNORM_ALPHA * x_ref[:, cols] + _dot(m_ref[...], wo_ref[:, cols])
    hres = xo_ref[...]
    mu = jnp.mean(hres, axis=-1, keepdims=True)
    d = hres - mu
    var = jnp.mean(d * d, axis=-1, keepdims=True)
    x_new = d * lax.rsqrt(var + LN_EPS) * lng_ref[...] + lnb_ref[...]
    xo_ref[...] = x_new
    xbo_ref[...] = x_new.astype(BF16)


def _outproj(merged, x, w_o, lng3, lnb3, layer, *, tm):
    t = x.shape[0]
    return pl.pallas_call(
        _outproj_kernel,
        grid=(t // tm,),
        in_specs=[
            pl.BlockSpec((tm, D_MODEL), lambda i: (i, 0)),
            pl.BlockSpec((tm, D_MODEL), lambda i: (i, 0)),
            pl.BlockSpec((None, D_MODEL, D_MODEL), lambda i: (layer, 0, 0)),
            pl.BlockSpec((None, 1, D_MODEL), lambda i: (layer, 0, 0)),
            pl.BlockSpec((None, 1, D_MODEL), lambda i: (layer, 0, 0)),
        ],
        out_specs=[pl.BlockSpec((tm, D_MODEL), lambda i: (i, 0)), pl.BlockSpec((tm, D_MODEL), lambda i: (i, 0))],
        out_shape=[jax.ShapeDtypeStruct((t, D_MODEL), F32), jax.ShapeDtypeStruct((t, D_MODEL), BF16)],
        compiler_params=_cparams(1),
        name="outproj",
    )(merged, x, w_o, lng3, lnb3)


DEC_NB = 8


def _columns(x):
    d = x.shape[1]
    eye = (_iota((d, d), 0) == _iota((d, d), 1)).astype(F32)
    return _dot_t(eye, x, precision=HIGHEST)


def _pad_rows_bf16(x):
    return jnp.concatenate([x, jnp.zeros_like(x)], axis=0).astype(BF16)


def _bf16_columns(x16):
    d = x16.shape[1]
    eye = jnp.where(_iota((d, d), 0) == _iota((d, d), 1), 1.0, 0.0).astype(BF16)
    return _dot_t(eye, x16).astype(BF16)


def _rank1_updates(s_ref, so_ref, o_scr, h, sl, decays, k_cols16, v16, q16):
    d = k_cols16.shape[0]
    lane = _iota(k_cols16.shape, 1)
    k_stack = jnp.concatenate(
        [jnp.where(lane == j, k_cols16, jnp.zeros_like(k_cols16)) for j in range(DEC_NB)], axis=0)
    upd = _dot(k_stack, v16)
    s_bf = []
    for j in range(DEC_NB):
        s_new = decays[j] * s_ref[0, j, h] + upd[j * d:(j + 1) * d]
        so_ref[0, j, h] = s_new
        s_bf.append(s_new.astype(BF16))
    for j in range(DEC_NB):
        o_scr[j:j + 1, sl] = _dot(q16, s_bf[j])[j:j + 1, :]


def _hgrn_dec_kernel(q_ref, g_ref, v_ref, z_ref, gain_ref, s_ref, *rest):
    y_ref, so_ref, o_scr = rest[-3:]
    for h in range(HG_HEADS):
        sl = slice(h * HG_D, (h + 1) * HG_D)
        f = jnp.exp(g_ref[:, sl])
        f_cols = _columns(f)
        k_cols16 = _bf16_columns(_pad_rows_bf16(1.0 - f))
        q16 = _pad_rows_bf16(q_ref[:, sl].astype(F32))
        v16 = _pad_rows_bf16(v_ref[:, sl].astype(F32))
        _rank1_updates(s_ref, so_ref, o_scr, h, sl, [f_cols[:, j:j + 1] for j in range(DEC_NB)], k_cols16, v16, q16)
        y_ref[:, sl] = _rms_gate(o_scr[:, sl], gain_ref[:, sl], z_ref[:, sl].astype(F32))


def _ret_dec_kernel(q_ref, k_ref, v_ref, z_ref, gain_ref, s_ref, *rest):
    y_ref, so_ref, o_scr = rest[-3:]
    for h in range(RET_HEADS):
        sl = slice(h * RET_D, (h + 1) * RET_D)
        gamma = 1.0 - 2.0 ** (-5.0 - h)
        k_cols16 = _bf16_columns(_pad_rows_bf16(k_ref[:, sl].astype(F32)))
        q16 = _pad_rows_bf16(q_ref[:, sl].astype(F32))
        v16 = _pad_rows_bf16(v_ref[:, sl].astype(F32))
        _rank1_updates(s_ref, so_ref, o_scr, h, sl, [gamma] * DEC_NB, k_cols16, v16, q16)
        y_ref[:, sl] = _ln_gate(o_scr[:, sl], gain_ref[:, sl], z_ref[:, sl].astype(F32))


def _mlstm_gate_dec_kernel(gt_ref, m_ref, mo_ref, w_ref, sc_ref, eps_ref):
    gi = gt_ref[0:ML_HEADS, :]
    gf = gt_ref[ML_HEADS:2 * ML_HEADS, :]
    log_inter = m_ref[...] + gf
    m_new = jnp.maximum(log_inter, gi)
    mo_ref[...] = m_new
    w_ref[...] = jnp.exp(gi - m_new)
    sc_ref[...] = jnp.exp(log_inter - m_new)
    eps_ref[...] = jnp.exp(-m_new)


def _mlstm_gate_dec(gt, m_t):
    shp = jax.ShapeDtypeStruct(m_t.shape, F32)
    return pl.pallas_call(_mlstm_gate_dec_kernel, out_shape=[shp, shp, shp, shp], name="mlstm_gate_dec")(gt, m_t)


def _mlstm_dec_kernel(w_sm, sc_sm, eps_sm, q_ref, k_ref, v_ref, z_ref, og_ref, gain_ref, n_ref, c_ref, *rest, bsz):
    y_ref, no_ref, co_ref, o_scr, kw_scr, sc_scr, eps_scr = rest[-7:]
    i = pl.program_id(0)
    for h in range(ML_HEADS):
        sl = slice(h * ML_D, (h + 1) * ML_D)
        base = h * bsz + i * DEC_NB
        qf = q_ref[:, sl].astype(F32)
        k_rows = k_ref[:, sl].astype(F32)
        for j in range(DEC_NB):
            kw_scr[j:j + 1, :] = k_rows[j:j + 1, :] * w_sm[base + j]
            sc_scr[j:j + 1, :] = jnp.full((1, 128), sc_sm[base + j], F32)
            eps_scr[j:j + 1, :] = jnp.full((1, 128), eps_sm[base + j], F32)
        kw_rows = kw_scr[...]
        kw_cols16 = _bf16_columns(_pad_rows_bf16(kw_rows))
        q16 = _pad_rows_bf16(qf)
        v16 = _pad_rows_bf16(v_ref[:, sl].astype(F32))
        _rank1_updates(c_ref, co_ref, o_scr, h, sl, [sc_sm[base + j] for j in range(DEC_NB)], kw_cols16, v16, q16)
        n_new = sc_scr[:, 0:1] * n_ref[:, sl] + kw_rows
        no_ref[:, sl] = n_new
        den = jnp.sum(qf * n_new, axis=1, keepdims=True)
        hid = o_scr[:, sl] / jnp.maximum(jnp.abs(den), eps_scr[:, 0:1])
        hid = og_ref[:, sl].astype(F32) * hid
        y_ref[:, sl] = _ln_gate(hid, gain_ref[:, sl], z_ref[:, sl].astype(F32))


def _hgrn_dec(ps, g, pp, gain3, state, layer, prev_out):
    bsz = ps.shape[0]
    in_specs = [
        pl.BlockSpec((DEC_NB, BW), lambda i: (i, SILU_BLOCKS.index(_HQ))),
        pl.BlockSpec((DEC_NB, BW), lambda i: (i, 0)),
        pl.BlockSpec((DEC_NB, BW), lambda i: (i, PLAIN_BLOCKS.index(_HI))),
        pl.BlockSpec((DEC_NB, BW), lambda i: (i, SILU_BLOCKS.index(_HZ))),
        pl.BlockSpec((None, 1, BW), lambda i: (layer, 0, 0)),
        pl.BlockSpec((1, DEC_NB, HG_HEADS, HG_D, HG_D), lambda i: (layer, i, 0, 0, 0)),
    ]
    args = [ps, g, pp, ps, gain3, state]
    s_spec, s_shape, x_specs, x_args, aliases = _state_chain(
        state.shape, (1, DEC_NB, HG_HEADS, HG_D, HG_D), lambda i: (layer, i, 0, 0, 0), prev_out, len(args), 1)
    return pl.pallas_call(
        _hgrn_dec_kernel,
        grid=(bsz // DEC_NB,),
        in_specs=in_specs + x_specs,
        out_specs=[pl.BlockSpec((DEC_NB, BW), lambda i: (i, 0)), s_spec],
        out_shape=[jax.ShapeDtypeStruct((bsz, BW), BF16), s_shape],
        scratch_shapes=[pltpu.VMEM((DEC_NB, BW), F32)],
        input_output_aliases=aliases,
        compiler_params=_cparams(1),
        name="hgrn_dec",
    )(*args, *x_args)


def _ret_dec(ps, pr, pp, gain3, state, layer, prev_out):
    bsz = ps.shape[0]
    d, nh = RET_D, RET_HEADS
    in_specs = [
        pl.BlockSpec((DEC_NB, BW), lambda i: (i, ROT_BLOCKS.index(_RQ))),
        pl.BlockSpec((DEC_NB, BW), lambda i: (i, ROT_BLOCKS.index(_RK))),
        pl.BlockSpec((DEC_NB, BW), lambda i: (i, PLAIN_BLOCKS.index(_RV))),
        pl.BlockSpec((DEC_NB, BW), lambda i: (i, SILU_BLOCKS.index(_RZ))),
        pl.BlockSpec((None, 1, BW), lambda i: (layer, 0, 0)),
        pl.BlockSpec((1, DEC_NB, nh, d, d), lambda i: (layer, i, 0, 0, 0)),
    ]
    args = [pr, pr, pp, ps, gain3, state]
    s_spec, s_shape, x_specs, x_args, aliases = _state_chain(
        state.shape, (1, DEC_NB, nh, d, d), lambda i: (layer, i, 0, 0, 0), prev_out, len(args), 1)
    return pl.pallas_call(
        _ret_dec_kernel,
        grid=(bsz // DEC_NB,),
        in_specs=in_specs + x_specs,
        out_specs=[pl.BlockSpec((DEC_NB, BW), lambda i: (i, 0)), s_spec],
        out_shape=[jax.ShapeDtypeStruct((bsz, BW), BF16), s_shape],
        scratch_shapes=[pltpu.VMEM((DEC_NB, BW), F32)],
        input_output_aliases=aliases,
        compiler_params=_cparams(1),
        name="ret_dec",
    )(*args, *x_args)


def _mlstm_dec(ps, pp, gain3, w_flat, sc_flat, eps_flat, n_state, c_state, layer, prev_out):
    bsz = ps.shape[0]
    d, nh = ML_D, ML_HEADS
    row = lambda blocks, blk: pl.BlockSpec((DEC_NB, BW), lambda i, *_: (i, blocks.index(blk)))
    in_specs = [
        row(PLAIN_BLOCKS, _MQ), row(PLAIN_BLOCKS, _MK), row(PLAIN_BLOCKS, _MV), row(SILU_BLOCKS, _MZ),
        row(SILU_BLOCKS, _MO),
        pl.BlockSpec((None, 1, BW), lambda i, *_: (layer, 0, 0)),
        pl.BlockSpec((DEC_NB, BW), lambda i, *_: (i, 0)),
        pl.BlockSpec((1, DEC_NB, nh, d, d), lambda i, *_: (layer, i, 0, 0, 0)),
    ]
    args = [pp, pp, pp, ps, ps, gain3, n_state, c_state]
    n_prefetch = 3
    c_spec, c_shape, x_specs, x_args, aliases = _state_chain(
        c_state.shape, (1, DEC_NB, nh, d, d), lambda i, *_: (layer, i, 0, 0, 0), prev_out,
        n_prefetch + len(args), 2)
    grid_spec = pltpu.PrefetchScalarGridSpec(
        num_scalar_prefetch=n_prefetch,
        grid=(bsz // DEC_NB,),
        in_specs=in_specs + x_specs,
        out_specs=[pl.BlockSpec((DEC_NB, BW), lambda i, *_: (i, 0)),
                   pl.BlockSpec((DEC_NB, BW), lambda i, *_: (i, 0)),
                   c_spec],
        scratch_shapes=[pltpu.VMEM((DEC_NB, BW), F32), pltpu.VMEM((DEC_NB, d), F32),
                        pltpu.VMEM((DEC_NB, 128), F32), pltpu.VMEM((DEC_NB, 128), F32)],
    )
    return pl.pallas_call(
        functools.partial(_mlstm_dec_kernel, bsz=bsz),
        grid_spec=grid_spec,
        out_shape=[jax.ShapeDtypeStruct((bsz, BW), BF16), jax.ShapeDtypeStruct((bsz, BW), F32), c_shape],
        input_output_aliases=aliases,
        compiler_params=_cparams(1),
        name="mlstm_dec",
    )(w_flat, sc_flat, eps_flat, *args, *x_args)


def _rotary_tables(pos):
    theta = 1.0 / (ROPE_BASE ** jnp.linspace(0.0, 1.0, RET_D // 2, dtype=F32))
    ang = pos.astype(F32)[:, None] * theta[None]
    cos = jnp.repeat(jnp.cos(ang), 2, axis=1)
    sin = jnp.repeat(jnp.sin(ang), 2, axis=1)
    sign = jnp.tile(jnp.asarray([-1.0, 1.0], F32), RET_D // 2)
    return cos, sin * sign[None]


def _rows3(a):
    return a.astype(F32)[:, None, :]


def _tile_sizes(n_tokens, seq):
    return min(1024, seq), min(1024, n_tokens), 512, min(512, n_tokens)


def kernel(x_prompt, x_sample, state_hgrn, state_ret, state_mlstm_c, state_mlstm_n, state_mlstm_m,
           w_in, hgrn_lb_logits, hgrn_norm, ret_norm, mlstm_norm, mlstm_b_i, mlstm_b_f,
           w_branch, w_out, ln_g, ln_b):
    bp, lp, _ = x_prompt.shape
    bs, ls, _ = x_sample.shape
    assert ls == 1 and w_in.dtype == F32
    lbs3 = _lower_bounds(hgrn_lb_logits.astype(F32))[:, None, :]
    assert w_in.shape[-1] == GATE_COL0 + 3 * D_MODEL
    w_in = jnp.swapaxes(w_in, 1, 2)
    w_gate, w_if = _gate_prep(w_in)
    b_if = jnp.concatenate([mlstm_b_i, mlstm_b_f], axis=-1).astype(F32)[:, :, None]
    w_br = w_branch.astype(BF16)
    w_o = w_out.astype(BF16)
    hg_gain, ret_gain, ml_gain = _rows3(hgrn_norm), _rows3(ret_norm), _rows3(mlstm_norm)
    lng3, lnb3 = _rows3(ln_g), _rows3(ln_b)
    cos_p, sin_p = _rotary_tables(jnp.arange(lp))
    cos_s, sin_s = _rotary_tables(jnp.full((bs,), PAST_LEN))

    tp, tmm, tnm, tmo = _tile_sizes(bp * lp, lp)
    x = x_prompt.reshape(bp * lp, D_MODEL)
    xb = x.astype(BF16)
    hg_p = ret_p = mc_p = None
    mn_l, mm_l = [], []
    for l in range(DEPTH):
        g, gt, ps, pr, pp = _project_all(xb, w_in, l, lbs3, w_if, b_if, cos_p, sin_p, tm=tp)
        ya, hg_p, yb, ret_p, yc, mc_p, mn, mm = _scan_prompt(
            g, gt, ps, pr, pp, hg_gain, ret_gain, ml_gain, l, None if l == 0 else (hg_p, ret_p, mc_p),
            bsz=bp, seq=lp, cs=min(128, lp))
        merged = _merge(xb, ya, yb, yc, w_gate, w_br, l, tm=tmm, tn=tnm)
        x, xb = _outproj(merged, x, w_o, lng3, lnb3, l, tm=tmo)
        mn_l.append(mn.reshape(bp, ML_HEADS, ML_D))
        mm_l.append(mm.reshape(bp, ML_HEADS, 128)[:, :, 0])
    y_prompt = x.reshape(bp, lp, D_MODEL)

    ts, tmm, tnm, tmo = _tile_sizes(bs, bs)
    x = x_sample.reshape(bs, D_MODEL)
    xb = x.astype(BF16)
    hg_s = ret_s = mc_s = None
    mn_sl, mm_sl = [], []
    for l in range(DEPTH):
        g, gt, ps, pr, pp = _project_all(xb, w_in, l, lbs3, w_if, b_if, cos_s, sin_s, tm=ts)
        ya, hg_s = _hgrn_dec(ps, g, pp, hg_gain, state_hgrn, l, hg_s)
        yb, ret_s = _ret_dec(ps, pr, pp, ret_gain, state_ret, l, ret_s)
        m_new, w_t, sc_t, eps_t = _mlstm_gate_dec(gt, state_mlstm_m[l].T)
        yc, mn, mc_s = _mlstm_dec(ps, pp, ml_gain, w_t.reshape(-1), sc_t.reshape(-1), eps_t.reshape(-1),
                                  state_mlstm_n[l].reshape(bs, BW), state_mlstm_c, l, mc_s)
        merged = _merge(xb, ya, yb, yc, w_gate, w_br, l, tm=tmm, tn=tnm)
        x, xb = _outproj(merged, x, w_o, lng3, lnb3, l, tm=tmo)
        mn_sl.append(mn.reshape(bs, ML_HEADS, ML_D))
        mm_sl.append(m_new.T)
    y_sample = x.reshape(bs, 1, D_MODEL)

    return (y_prompt, y_sample, hg_p, ret_p, mc_p, jnp.stack(mn_l), jnp.stack(mm_l),
            hg_s, ret_s, mc_s, jnp.stack(mn_sl), jnp.stack(mm_sl))
```

```python
import functools

import jax
import jax.numpy as jnp
from jax import lax
from jax.experimental import pallas as pl
from jax.experimental.pallas import tpu as pltpu

F32 = jnp.float32
BF16 = jnp.bfloat16
HIGHEST = lax.Precision.HIGHEST

D_MODEL = 2048
DEPTH = 4
PAST_LEN = 16384
BW = D_MODEL // 2
HG_HEADS, HG_D = 8, 128
RET_HEADS, RET_D = 4, 256
ML_HEADS, ML_D = 4, 256
HG_LB_FLOOR = 1e-30
ROPE_BASE = 10000.0
NEG_LARGE = -1e30
NORM_EPS = 1e-6
LN_EPS = 1e-5
DEEPNORM_ALPHA = (2 * DEPTH) ** 0.25

_HQ, _HF, _HI, _HZ, _RQ, _RK, _RV, _RZ, _MQ, _MK, _MV, _MZ, _MO = range(13)
SILU_BLOCKS = (_HQ, _HZ, _RZ, _MZ, _MO)
ROT_BLOCKS = (_RQ, _RK)
PLAIN_BLOCKS = (_HI, _RV, _MQ, _MK, _MV)
IF_COL0 = 13 * BW
GATE_COL0 = 13 * BW + 2 * ML_HEADS

V7X_VMEM_LIMIT = 56 * 1024 * 1024
V7X_MXU_COLS = 256


def _cparams(n_grid):
    return pltpu.CompilerParams(dimension_semantics=("arbitrary",) * n_grid, vmem_limit_bytes=V7X_VMEM_LIMIT)


def _sigmoid(x):
    return 1.0 / (1.0 + jnp.exp(-x))


def _silu(x):
    return x * _sigmoid(x)


def _log_sigmoid(x):
    return jnp.minimum(x, 0.0) - jnp.log1p(jnp.exp(-jnp.abs(x)))


def _iota(shape, dim):
    return lax.broadcasted_iota(jnp.int32, shape, dim)


def _dot(a, b):
    return jnp.dot(a, b, preferred_element_type=F32)


def _dot_t(a, b, precision=None):
    return lax.dot_general(a, b, (((1,), (1,)), ((), ())), preferred_element_type=F32, precision=precision)


def _lookup(j, table):
    out = table[0]
    for k in range(1, len(table)):
        out = jnp.where(j == k, table[k], out)
    return out


def _lb_kernel(logit_ref, lb_ref):
    z = logit_ref[...]
    rows = [z[l:l + 1] for l in range(DEPTH)]
    mx = functools.reduce(jnp.maximum, rows)
    ex = [jnp.exp(r - mx) for r in rows]
    tot = functools.reduce(lambda a, b: a + b, ex)
    sm = [e / tot for e in ex]
    run = sm[0]
    lb_ref[0:1, :] = run - sm[0]
    for l in range(1, DEPTH):
        run = run + sm[l]
        lb_ref[l:l + 1, :] = run - sm[0]


def _lower_bounds(logits):
    return pl.pallas_call(_lb_kernel, out_shape=jax.ShapeDtypeStruct(logits.shape, F32), name="lower_bounds")(logits)


GATE_PREP_TN = 512
XPOSE_CHUNK = 256


def _transpose_cast(w_ref, wb_ref):
    for r in range(w_ref.shape[1] // XPOSE_CHUNK):
        ks = slice(r * XPOSE_CHUNK, (r + 1) * XPOSE_CHUNK)
        wb_ref[ks, :] = w_ref[:, ks].T.astype(BF16)


def _gate_prep_kernel(wt_ref, wif_rows_ref, wg_ref, wif_ref):
    _transpose_cast(wt_ref, wg_ref)

    @pl.when(pl.program_id(1) == 0)
    def _():
        rows = wif_rows_ref[...]
        padded = jnp.concatenate([rows, jnp.zeros((128 - rows.shape[0], D_MODEL), F32)], axis=0)
        wif_ref[...] = padded.T.astype(BF16)


def _gate_prep(w_t):
    tn = GATE_PREP_TN
    assert GATE_COL0 % 8 == 0 and IF_COL0 % 8 == 0
    return pl.pallas_call(
        _gate_prep_kernel,
        grid=(DEPTH, 3 * D_MODEL // tn),
        in_specs=[pl.BlockSpec((None, pl.Element(tn), pl.Element(D_MODEL)), lambda l, j: (l, pl.multiple_of(GATE_COL0 + j * tn, 8), 0)),
                  pl.BlockSpec((None, 2 * ML_HEADS, D_MODEL), lambda l, j: (l, IF_COL0 // (2 * ML_HEADS), 0))],
        out_specs=[pl.BlockSpec((None, D_MODEL, tn), lambda l, j: (l, 0, j)),
                   pl.BlockSpec((None, D_MODEL, 128), lambda l, j: (l, 0, 0))],
        out_shape=[jax.ShapeDtypeStruct((DEPTH, D_MODEL, 3 * D_MODEL), BF16),
                   jax.ShapeDtypeStruct((DEPTH, D_MODEL, 128), BF16)],
        compiler_params=_cparams(2),
        name="gate_prep",
    )(w_t, w_t)


def _proj_body(xb_ref, wb_ref, out_ref, epilogue):
    for s in range(wb_ref.shape[1] // V7X_MXU_COLS):
        cols = slice(s * V7X_MXU_COLS, (s + 1) * V7X_MXU_COLS)
        out_ref[:, cols] = epilogue(_dot(xb_ref[...], wb_ref[:, cols]), cols).astype(out_ref.dtype)


def _cast_weights(w_ref, wb_ref):
    @pl.when(pl.program_id(1) == 0)
    def _():
        _transpose_cast(w_ref, wb_ref)


def _two_groups(body):
    i = pl.program_id(1)
    n_prompt = pl.num_programs(1) - 1

    @pl.when(i < n_prompt)
    def _():
        body(0)

    @pl.when(i == n_prompt)
    def _():
        body(1)


def _proj_silu_kernel(xp_ref, xs_ref, w_ref, op_ref, os_ref, wb_ref, *, sigmoid_block):
    _cast_weights(w_ref, wb_ref)
    plain_sigmoid = pl.program_id(0) == sigmoid_block

    def epilogue(acc, cols):
        s = _sigmoid(acc)
        return jnp.where(plain_sigmoid, s, acc * s)

    _two_groups(lambda grp: _proj_body((xp_ref, xs_ref)[grp], wb_ref, (op_ref, os_ref)[grp], epilogue))


def _proj_plain_kernel(xp_ref, xs_ref, w_ref, op_ref, os_ref, wb_ref, *, scaled_block, scale):
    _cast_weights(w_ref, wb_ref)
    sc = jnp.where(pl.program_id(0) == scaled_block, scale, 1.0).astype(F32)
    _two_groups(lambda grp: _proj_body((xp_ref, xs_ref)[grp], wb_ref, (op_ref, os_ref)[grp],
                                       lambda acc, cols: acc * sc))


def _proj_rotary_kernel(xp_ref, xs_ref, w_ref, cosp_ref, sinp_ref, coss_ref, sins_ref, op_ref, os_ref, wb_ref, *,
                        scaled_block, scale):
    _cast_weights(w_ref, wb_ref)
    sc = jnp.where(pl.program_id(0) == scaled_block, scale, 1.0).astype(F32)
    assert V7X_MXU_COLS == RET_D

    def body(grp):
        cos_ref, sin_ref = ((cosp_ref, sinp_ref), (coss_ref, sins_ref))[grp]

        def epilogue(acc, cols):
            even = (_iota(acc.shape, 1) % 2) == 0
            swapped = jnp.where(even, pltpu.roll(acc, RET_D - 1, 1), pltpu.roll(acc, 1, 1))
            return (acc * cos_ref[...] + swapped * sin_ref[...]) * sc

        _proj_body((xp_ref, xs_ref)[grp], wb_ref, (op_ref, os_ref)[grp], epilogue)

    _two_groups(body)


def _proj_logf_kernel(xp_ref, xs_ref, w_ref, lb_ref, wif_ref, bif_ref, gp_ref, gs_ref, gtp_ref, gts_ref, wb_ref):
    _cast_weights(w_ref, wb_ref)

    def body(grp):
        x_ref, g_ref, gt_ref = ((xp_ref, gp_ref, gtp_ref), (xs_ref, gs_ref, gts_ref))[grp]
        pt = _dot(x_ref[...], wif_ref[...]).T[0:2 * ML_HEADS, :] + bif_ref[...]
        gt_ref[...] = jnp.where(_iota(pt.shape, 0) < ML_HEADS, pt, _log_sigmoid(pt))

        def epilogue(acc, cols):
            lb = lb_ref[:, cols]
            return jnp.log(jnp.maximum(lb, HG_LB_FLOOR) + (1.0 - lb) * _sigmoid(acc))

        _proj_body(x_ref, wb_ref, g_ref, epilogue)

    _two_groups(body)


def _proj(kern, xp, xs, w_t, layer, blocks, extra_in, extra_specs, out_dtype, *, tm, extra_out=(), extra_out_specs=()):
    tp, ts = xp.shape[0], xs.shape[0]
    n_prompt = tp // tm
    ptile = lambda i: jnp.minimum(i, n_prompt - 1)
    out_shape = [jax.ShapeDtypeStruct((tp, len(blocks) * BW), out_dtype),
                 jax.ShapeDtypeStruct((ts, len(blocks) * BW), out_dtype)] + list(extra_out)
    out_specs = [pl.BlockSpec((tm, BW), lambda j, i: (ptile(i), j)),
                 pl.BlockSpec((ts, BW), lambda j, i: (0, j))] + list(extra_out_specs)
    return pl.pallas_call(
        kern,
        grid=(len(blocks), n_prompt + 1),
        in_specs=[pl.BlockSpec((tm, D_MODEL), lambda j, i: (ptile(i), 0)),
                  pl.BlockSpec((ts, D_MODEL), lambda j, i: (0, 0)),
                  pl.BlockSpec((None, BW, D_MODEL), lambda j, i: (layer, _lookup(j, blocks), 0))] + list(extra_specs),
        out_specs=out_specs,
        out_shape=out_shape,
        scratch_shapes=[pltpu.VMEM((D_MODEL, BW), BF16)],
        compiler_params=_cparams(2),
        name="proj",
    )(xp, xs, w_t, *extra_in)


def _project_all(xp, xs, w_t, layer, lbs3, w_if, b_if, cos_p, sin_p, cos_s, sin_s, *, tm):
    tp, ts = xp.shape[0], xs.shape[0]
    n_prompt = tp // tm
    n_pos = cos_p.shape[0] // tm
    ptile = lambda i: jnp.minimum(i, n_prompt - 1)
    g_p, g_s, gt_p, gt_s = _proj(
        _proj_logf_kernel, xp, xs, w_t, layer, (_HF,),
        (lbs3, w_if, b_if),
        (pl.BlockSpec((None, 1, BW), lambda j, i: (layer, 0, 0)),
         pl.BlockSpec((None, D_MODEL, 128), lambda j, i: (layer, 0, 0)),
         pl.BlockSpec((None, 2 * ML_HEADS, 1), lambda j, i: (layer, 0, 0))),
        F32, tm=tm,
        extra_out=(jax.ShapeDtypeStruct((2 * ML_HEADS, tp), F32), jax.ShapeDtypeStruct((2 * ML_HEADS, ts), F32)),
        extra_out_specs=(pl.BlockSpec((2 * ML_HEADS, tm), lambda j, i: (0, ptile(i))),
                         pl.BlockSpec((2 * ML_HEADS, ts), lambda j, i: (0, 0))))
    ps = _proj(functools.partial(_proj_silu_kernel, sigmoid_block=SILU_BLOCKS.index(_MO)),
               xp, xs, w_t, layer, SILU_BLOCKS, (), (), BF16, tm=tm)
    pos_spec = pl.BlockSpec((tm, RET_D), lambda j, i: (ptile(i) % n_pos, 0))
    one_spec = pl.BlockSpec((ts, RET_D), lambda j, i: (0, 0))
    pr = _proj(functools.partial(_proj_rotary_kernel, scaled_block=ROT_BLOCKS.index(_RK), scale=RET_D ** -0.5),
               xp, xs, w_t, layer, ROT_BLOCKS, (cos_p, sin_p, cos_s, sin_s),
               (pos_spec, pos_spec, one_spec, one_spec), BF16, tm=tm)
    pp = _proj(functools.partial(_proj_plain_kernel, scaled_block=PLAIN_BLOCKS.index(_MK), scale=ML_D ** -0.5),
               xp, xs, w_t, layer, PLAIN_BLOCKS, (), (), BF16, tm=tm)
    return (g_p, g_s), (gt_p, gt_s), ps, pr, pp


def _rms_gate(o, gain, zs):
    y = o * lax.rsqrt(jnp.mean(o * o, axis=-1, keepdims=True) + NORM_EPS)
    return (y * gain * zs).astype(BF16)


def _ln_gate(o, gain, zs):
    mu = jnp.mean(o, axis=-1, keepdims=True)
    d = o - mu
    var = jnp.mean(d * d, axis=-1, keepdims=True)
    return (d * lax.rsqrt(var + NORM_EPS) * gain * zs).astype(BF16)


def _state_chain(state_shape, block, index_map, prev_out, n_in, out_pos):
    spec = pl.BlockSpec(block, index_map)
    shape = jax.ShapeDtypeStruct(state_shape, F32)
    if prev_out is None:
        return spec, shape, [], [], {}
    return spec, shape, [pl.BlockSpec(memory_space=pl.ANY)], [prev_out], {n_in: out_pos}


def _hgrn_level_ref(bc, m):
    cs, d = bc.shape
    if m >= 8:
        parts = [jnp.broadcast_to(bc[b * 2 * m + m - 1:b * 2 * m + m], (2 * m, d)) for b in range(cs // (2 * m))]
        return parts[0] if len(parts) == 1 else jnp.concatenate(parts, axis=0)
    x3 = bc.reshape(cs // 8, 8, d)
    sub = _iota((cs // 8, 8, d), 1)
    pick = lambda r: jnp.broadcast_to(x3[:, r:r + 1, :], x3.shape)
    if m == 4:
        r3 = pick(3)
    elif m == 2:
        r3 = jnp.where(sub < 4, pick(1), pick(5))
    else:
        r3 = jnp.where(sub < 2, pick(0), jnp.where(sub < 4, pick(2), jnp.where(sub < 6, pick(4), pick(6))))
    return r3.reshape(cs, d)


def _hgrn_stages(q_ref, g_ref, v_ref, z_ref, gain_ref, y_ref, s_ref):
    c = pl.program_id(1)
    cs = q_ref.shape[0]

    @pl.when(c == 0)
    def _():
        s_ref[...] = jnp.zeros_like(s_ref)

    row = _iota((cs, cs), 0)
    col = _iota((cs, cs), 1)
    tril = (row >= col).astype(F32)
    level = jnp.where(row > col, 31 - lax.clz(row ^ col), -1)
    eye_c = row == col
    eye_d = _iota((HG_D, HG_D), 0) == _iota((HG_D, HG_D), 1)
    levels = [cs >> (k + 1) for k in range(cs.bit_length() - 1)]
    heads = [slice(h * HG_D, (h + 1) * HG_D) for h in range(HG_HEADS)]
    q = q_ref[...].astype(F32)
    g = g_ref[...]
    kk = 1.0 - jnp.exp(g)
    bc = jnp.dot(tril, g, preferred_element_type=F32, precision=HIGHEST)
    qk = q * kk
    a = [jnp.where(eye_c, jnp.sum(qk[:, sl], axis=1, keepdims=True), 0.0) for sl in heads]
    yield
    for m in levels:
        e = jnp.exp(-jnp.abs(bc - _hgrn_level_ref(bc, m)))
        qs = (q * e).astype(BF16)
        ks = (kk * e).astype(BF16)
        sel = level == m.bit_length() - 1
        a = [jnp.where(sel, _dot_t(qs[:, sl], ks[:, sl]), a_h) for sl, a_h in zip(heads, a)]
        yield
    qe = (q * jnp.exp(bc)).astype(BF16)
    last = bc[cs - 1:cs]
    k_dec_t = (kk * jnp.exp(last - bc)).T.astype(BF16)
    dec_row = jnp.exp(last)
    yield
    for h, sl in enumerate(heads):
        v = v_ref[:, sl]
        s_old = s_ref[0, 0, h]
        o = _dot(qe[:, sl], s_old.astype(BF16)) + _dot(a[h].astype(BF16), v)
        y_ref[:, sl] = _rms_gate(o, gain_ref[:, sl], z_ref[:, sl].astype(F32))
        dec_col = jnp.sum(jnp.where(eye_d, dec_row[:, sl], 0.0), axis=1, keepdims=True)
        s_ref[0, 0, h] = dec_col * s_old + _dot(k_dec_t[sl, :], v)
        if h % 2 == 1:
            yield


def _ret_log_gamma(h):
    return jnp.log(jnp.full((1, 1), 1.0 - 2.0 ** (-5.0 - h), F32))


def _ret_stages(q_ref, k_ref, v_ref, z_ref, gain_ref, y_ref, s_ref):
    c = pl.program_id(1)
    cs = q_ref.shape[0]

    @pl.when(c == 0)
    def _():
        s_ref[...] = jnp.zeros_like(s_ref)

    t_col = _iota((cs, 1), 0).astype(F32)
    rel = (_iota((cs, cs), 0) - _iota((cs, cs), 1)).astype(F32)
    causal = rel >= 0.0
    heads = [slice(h * RET_D, (h + 1) * RET_D) for h in range(RET_HEADS)]
    hr = range(RET_HEADS)
    lg = [_ret_log_gamma(h) for h in hr]
    s_qk = [_dot_t(q_ref[:, sl], k_ref[:, sl]) for sl in heads]
    yield
    s_old = [s_ref[0, 0, h] for h in hr]
    q_s = [_dot(q_ref[:, sl], s_old[h].astype(BF16)) for h, sl in enumerate(heads)]
    yield
    k_dec_t = [(k_ref[:, sl].astype(F32) * jnp.exp((cs - 1.0 - t_col) * lg[h])).T.astype(BF16)
               for h, sl in enumerate(heads)]
    upd = [_dot(k_dec_t[h], v_ref[:, sl]) for h, sl in enumerate(heads)]
    yield
    a = [s_qk[h] * jnp.where(causal, jnp.exp(jnp.where(causal, rel * lg[h], 0.0)), 0.0) for h in hr]
    a_v = [_dot(a[h].astype(BF16), v_ref[:, sl]) for h, sl in enumerate(heads)]
    yield
    for h, sl in enumerate(heads):
        o = jnp.exp((t_col + 1.0) * lg[h]) * q_s[h] + a_v[h]
        y_ref[:, sl] = _ln_gate(o, gain_ref[:, sl], z_ref[:, sl].astype(F32))
        s_ref[0, 0, h] = jnp.exp(cs * lg[h]) * s_old[h] + upd[h]
        if h % 2 == 1:
            yield


def _mlstm_stages(q_ref, k_ref, v_ref, z_ref, og_ref, gt_ref, gain_ref, y_ref, c_ref, n_ref, m_ref):
    c = pl.program_id(1)
    cs = q_ref.shape[0]

    @pl.when(c == 0)
    def _():
        c_ref[...] = jnp.zeros_like(c_ref)
        n_ref[...] = jnp.zeros_like(n_ref)
        m_ref[...] = jnp.zeros_like(m_ref)

    row = _iota((cs, cs), 0)
    col = _iota((cs, cs), 1)
    causal = row >= col
    tril = causal.astype(F32)
    triu = (row <= col).astype(F32)
    eye = (row == col).astype(F32)
    gi = gt_ref[0:ML_HEADS, :]
    gf = gt_ref[ML_HEADS:2 * ML_HEADS, :]
    b_rows = jnp.dot(gf, triu, preferred_element_type=F32, precision=HIGHEST)
    b_cols = _dot_t(tril, gf, precision=HIGHEST)
    i_cols = _dot_t(eye, gi, precision=HIGHEST)
    heads = [slice(h * ML_D, (h + 1) * ML_D) for h in range(ML_HEADS)]
    hr = range(ML_HEADS)
    s_qk = [_dot_t(q_ref[:, sl], k_ref[:, sl]) for sl in heads]
    yield
    c_old = [c_ref[0, 0, h] for h in hr]
    q_c = [_dot(q_ref[:, sl], c_old[h].astype(BF16)) for h, sl in enumerate(heads)]
    yield
    m_old = [m_ref[0, :, h * 128:h * 128 + 1] for h in hr]
    b_col = [b_cols[:, h:h + 1] for h in hr]
    log_inter = [m_old[h] + b_col[h] for h in hr]
    log_w = [jnp.where(causal, b_col[h] - b_rows[h:h + 1, :] + gi[h:h + 1, :], NEG_LARGE) for h in hr]
    m = [jnp.maximum(log_inter[h], jnp.max(log_w[h], axis=1, keepdims=True)) for h in hr]
    yield
    qk = [s_qk[h] * jnp.where(causal, jnp.exp(jnp.where(causal, log_w[h] - m[h], 0.0)), 0.0) for h in hr]
    sc = [jnp.exp(log_inter[h] - m[h]) for h in hr]
    num = [_dot(qk[h].astype(BF16), v_ref[:, sl]) + sc[h] * q_c[h] for h, sl in enumerate(heads)]
    yield
    m_last = [m[h][cs - 1:cs] for h in hr]
    b_last = [b_col[h][cs - 1:cs] for h in hr]
    kw = [k_ref[:, sl].astype(F32) * jnp.exp(b_last[h] - b_col[h] + i_cols[:, h:h + 1] - m_last[h])
          for h, sl in enumerate(heads)]
    upd = [_dot(kw[h].T.astype(BF16), v_ref[:, sl]) for h, sl in enumerate(heads)]
    yield
    for h, sl in enumerate(heads):
        n_old = n_ref[0, :, sl]
        den = (jnp.sum(qk[h], axis=1, keepdims=True)
               + sc[h] * jnp.sum(q_ref[:, sl].astype(F32) * n_old, axis=1, keepdims=True))
        hid = num[h] / jnp.maximum(jnp.abs(den), jnp.exp(-m[h]))
        hid = og_ref[:, sl].astype(F32) * hid
        y_ref[:, sl] = _ln_gate(hid, gain_ref[:, sl], z_ref[:, sl].astype(F32))
        dec = jnp.exp(m_old[h] + b_last[h] - m_last[h])
        c_ref[0, 0, h] = dec * c_old[h] + upd[h]
        n_ref[0, :, sl] = dec * n_old + jnp.sum(kw[h], axis=0, keepdims=True)
        m_ref[0, :, h * 128:(h + 1) * 128] = jnp.broadcast_to(m_last[h], (1, 128))
        if h % 2 == 1:
            yield


N_SCAN_IN = 17


def _scan_kernel(*refs):
    ins, outs = refs[:N_SCAN_IN], refs[-8:]
    stages = [_hgrn_stages(*ins[0:5], *outs[0:2]), _ret_stages(*ins[5:10], *outs[2:4]),
              _mlstm_stages(*ins[10:17], *outs[4:8])]
    while stages:
        stages = [s for s in stages if next(s, True) is None]


def _scan_prompt(g, gt, ps, pr, pp, hg_gain, ret_gain, ml_gain, layer, prev, *, bsz, seq, cs):
    nc = seq // cs
    tok = lambda b, c: b * nc + c
    col = lambda blocks, blk: pl.BlockSpec((cs, BW), lambda b, c: (tok(b, c), blocks.index(blk)))
    gain = pl.BlockSpec((None, 1, BW), lambda b, c: (layer, 0, 0))
    in_specs = [
        col(SILU_BLOCKS, _HQ), pl.BlockSpec((cs, BW), lambda b, c: (tok(b, c), 0)), col(PLAIN_BLOCKS, _HI),
        col(SILU_BLOCKS, _HZ), gain,
        col(ROT_BLOCKS, _RQ), col(ROT_BLOCKS, _RK), col(PLAIN_BLOCKS, _RV), col(SILU_BLOCKS, _RZ), gain,
        col(PLAIN_BLOCKS, _MQ), col(PLAIN_BLOCKS, _MK), col(PLAIN_BLOCKS, _MV), col(SILU_BLOCKS, _MZ),
        col(SILU_BLOCKS, _MO), pl.BlockSpec((2 * ML_HEADS, cs), lambda b, c: (0, tok(b, c))), gain,
    ]
    args = [ps, g, pp, ps, hg_gain, pr, pr, pp, ps, ret_gain, pp, pp, pp, ps, ps, gt, ml_gain]
    assert len(args) == N_SCAN_IN
    y_spec = pl.BlockSpec((cs, BW), lambda b, c: (tok(b, c), 0))
    y_shape = jax.ShapeDtypeStruct((bsz * seq, BW), BF16)
    state_specs, state_shapes, aliases = [], [], {}
    for k, (nh, d) in enumerate(((HG_HEADS, HG_D), (RET_HEADS, RET_D), (ML_HEADS, ML_D))):
        spec, shape, x_specs, x_args, alias = _state_chain(
            (DEPTH, bsz, nh, d, d), (1, 1, nh, d, d), lambda b, c: (layer, b, 0, 0, 0),
            None if prev is None else prev[k], len(args), 2 * k + 1)
        state_specs.append(spec)
        state_shapes.append(shape)
        in_specs = in_specs + x_specs
        args = args + x_args
        aliases.update(alias)
    return pl.pallas_call(
        _scan_kernel,
        grid=(bsz, nc),
        in_specs=in_specs,
        out_specs=[y_spec, state_specs[0], y_spec, state_specs[1], y_spec, state_specs[2],
                   pl.BlockSpec((1, 1, BW), lambda b, c: (b, 0, 0)),
                   pl.BlockSpec((1, 1, ML_HEADS * 128), lambda b, c: (b, 0, 0))],
        out_shape=[y_shape, state_shapes[0], y_shape, state_shapes[1], y_shape, state_shapes[2],
                   jax.ShapeDtypeStruct((bsz, 1, BW), F32),
                   jax.ShapeDtypeStruct((bsz, 1, ML_HEADS * 128), F32)],
        input_output_aliases=aliases,
        compiler_params=_cparams(2),
        name="scan_prompt",
    )(*args)


def _merge_kernel(xb_ref, ya_ref, yb_ref, yc_ref, wg0_ref, wg1_ref, wg2_ref, wbr_ref, out_ref):
    xb = xb_ref[...]
    for s in range(out_ref.shape[1] // V7X_MXU_COLS):
        cols = slice(s * V7X_MXU_COLS, (s + 1) * V7X_MXU_COLS)
        merged = None
        for n, (y_ref, wg_ref) in enumerate(((ya_ref, wg0_ref), (yb_ref, wg1_ref), (yc_ref, wg2_ref))):
            term = _sigmoid(_dot(xb, wg_ref[:, cols])) * _dot(y_ref[...], wbr_ref[n, :, cols])
            merged = term if merged is None else merged + term
        out_ref[:, cols] = merged.astype(BF16)


def _merge(xb, ya, yb, yc, w_gate, w_br, layer, *, tm, tn):
    t = xb.shape[0]
    nj = D_MODEL // tn
    gate_spec = lambda n: pl.BlockSpec((None, D_MODEL, tn), lambda i, j: (layer, 0, n * nj + j))
    return pl.pallas_call(
        _merge_kernel,
        grid=(t // tm, nj),
        in_specs=[
            pl.BlockSpec((tm, D_MODEL), lambda i, j: (i, 0)),
            pl.BlockSpec((tm, BW), lambda i, j: (i, 0)),
            pl.BlockSpec((tm, BW), lambda i, j: (i, 0)),
            pl.BlockSpec((tm, BW), lambda i, j: (i, 0)),
            gate_spec(0), gate_spec(1), gate_spec(2),
            pl.BlockSpec((None, 3, BW, tn), lambda i, j: (layer, 0, 0, j)),
        ],
        out_specs=pl.BlockSpec((tm, tn), lambda i, j: (i, j)),
        out_shape=jax.ShapeDtypeStruct((t, D_MODEL), BF16),
        compiler_params=_cparams(2),
        name="merge",
    )(xb, ya, yb, yc, w_gate, w_gate, w_gate, w_br)


def _outproj_kernel(m_ref, x_ref, wo_ref, lng_ref, lnb_ref, xo_ref, xbo_ref):
    for s in range(D_MODEL // V7X_MXU_COLS):
        cols = slice(s * V7X_MXU_COLS, (s + 1) * V7X_MXU_COLS)
        xo_ref[:, cols] = DEEPNORM_ALPHA * x_ref[:, cols] + _dot(m_ref[...], wo_ref[:, cols])
    hres = xo_ref[...]
    mu = jnp.mean(hres, axis=-1, keepdims=True)
    d = hres - mu
    var = jnp.mean(d * d, axis=-1, keepdims=True)
    x_new = d * lax.rsqrt(var + LN_EPS) * lng_ref[...] + lnb_ref[...]
    xo_ref[...] = x_new
    xbo_ref[...] = x_new.astype(BF16)


def _outproj(merged, x, w_o, lng3, lnb3, layer, *, tm):
    t = x.shape[0]
    return pl.pallas_call(
        _outproj_kernel,
        grid=(t // tm,),
        in_specs=[
            pl.BlockSpec((tm, D_MODEL), lambda i: (i, 0)),
            pl.BlockSpec((tm, D_MODEL), lambda i: (i, 0)),
            pl.BlockSpec((None, D_MODEL, D_MODEL), lambda i: (layer, 0, 0)),
            pl.BlockSpec((None, 1, D_MODEL), lambda i: (layer, 0, 0)),
            pl.BlockSpec((None, 1, D_MODEL), lambda i: (layer, 0, 0)),
        ],
        out_specs=[pl.BlockSpec((tm, D_MODEL), lambda i: (i, 0)), pl.BlockSpec((tm, D_MODEL), lambda i: (i, 0))],
        out_shape=[jax.ShapeDtypeStruct((t, D_MODEL), F32), jax.ShapeDtypeStruct((t, D_MODEL), BF16)],
        compiler_params=_cparams(1),
        name="outproj",
    )(merged, x, w_o, lng3, lnb3)


DEC_NB = 8


def _columns(x):
    d = x.shape[1]
    eye = (_iota((d, d), 0) == _iota((d, d), 1)).astype(F32)
    return _dot_t(eye, x, precision=HIGHEST)


def _pad_rows_bf16(x):
    return jnp.concatenate([x, jnp.zeros_like(x)], axis=0).astype(BF16)


def _bf16_columns(x16):
    d = x16.shape[1]
    eye = jnp.where(_iota((d, d), 0) == _iota((d, d), 1), 1.0, 0.0).astype(BF16)
    return _dot_t(eye, x16).astype(BF16)


def _rank1_updates(s_ref, so_ref, o_scr, h, sl, decays, k_cols16, v16, q16):
    d = k_cols16.shape[0]
    lane = _iota(k_cols16.shape, 1)
    k_stack = jnp.concatenate(
        [jnp.where(lane == j, k_cols16, jnp.zeros_like(k_cols16)) for j in range(DEC_NB)], axis=0)
    upd = _dot(k_stack, v16)
    s_bf = []
    for j in range(DEC_NB):
        s_new = decays[j] * s_ref[0, j, h] + upd[j * d:(j + 1) * d]
        so_ref[0, j, h] = s_new
        s_bf.append(s_new.astype(BF16))
    for j in range(DEC_NB):
        o_scr[j:j + 1, sl] = _dot(q16, s_bf[j])[j:j + 1, :]


def _hgrn_dec_kernel(q_ref, g_ref, v_ref, z_ref, gain_ref, s_ref, *rest):
    y_ref, so_ref, o_scr = rest[-3:]
    for h in range(HG_HEADS):
        sl = slice(h * HG_D, (h + 1) * HG_D)
        f = jnp.exp(g_ref[:, sl])
        f_cols = _columns(f)
        k_cols16 = _bf16_columns(_pad_rows_bf16(1.0 - f))
        q16 = _pad_rows_bf16(q_ref[:, sl].astype(F32))
        v16 = _pad_rows_bf16(v_ref[:, sl].astype(F32))
        _rank1_updates(s_ref, so_ref, o_scr, h, sl, [f_cols[:, j:j + 1] for j in range(DEC_NB)], k_cols16, v16, q16)
        y_ref[:, sl] = _rms_gate(o_scr[:, sl], gain_ref[:, sl], z_ref[:, sl].astype(F32))


def _ret_dec_kernel(q_ref, k_ref, v_ref, z_ref, gain_ref, s_ref, *rest):
    y_ref, so_ref, o_scr = rest[-3:]
    for h in range(RET_HEADS):
        sl = slice(h * RET_D, (h + 1) * RET_D)
        gamma = 1.0 - 2.0 ** (-5.0 - h)
        k_cols16 = _bf16_columns(_pad_rows_bf16(k_ref[:, sl].astype(F32)))
        q16 = _pad_rows_bf16(q_ref[:, sl].astype(F32))
        v16 = _pad_rows_bf16(v_ref[:, sl].astype(F32))
        _rank1_updates(s_ref, so_ref, o_scr, h, sl, [gamma] * DEC_NB, k_cols16, v16, q16)
        y_ref[:, sl] = _ln_gate(o_scr[:, sl], gain_ref[:, sl], z_ref[:, sl].astype(F32))


def _mlstm_gate_dec_kernel(gt_ref, m_ref, mo_ref, w_ref, sc_ref, eps_ref):
    gi = gt_ref[0:ML_HEADS, :]
    gf = gt_ref[ML_HEADS:2 * ML_HEADS, :]
    log_inter = m_ref[...] + gf
    m_new = jnp.maximum(log_inter, gi)
    mo_ref[...] = m_new
    w_ref[...] = jnp.exp(gi - m_new)
    sc_ref[...] = jnp.exp(log_inter - m_new)
    eps_ref[...] = jnp.exp(-m_new)


def _mlstm_gate_dec(gt, m_t):
    shp = jax.ShapeDtypeStruct(m_t.shape, F32)
    return pl.pallas_call(_mlstm_gate_dec_kernel, out_shape=[shp, shp, shp, shp], name="mlstm_gate_dec")(gt, m_t)


def _mlstm_dec_kernel(w_sm, sc_sm, eps_sm, q_ref, k_ref, v_ref, z_ref, og_ref, gain_ref, n_ref, c_ref, *rest, bsz):
    y_ref, no_ref, co_ref, o_scr, kw_scr, sc_scr, eps_scr = rest[-7:]
    i = pl.program_id(0)
    for h in range(ML_HEADS):
        sl = slice(h * ML_D, (h + 1) * ML_D)
        base = h * bsz + i * DEC_NB
        qf = q_ref[:, sl].astype(F32)
        k_rows = k_ref[:, sl].astype(F32)
        for j in range(DEC_NB):
            kw_scr[j:j + 1, :] = k_rows[j:j + 1, :] * w_sm[base + j]
            sc_scr[j:j + 1, :] = jnp.full((1, 128), sc_sm[base + j], F32)
            eps_scr[j:j + 1, :] = jnp.full((1, 128), eps_sm[base + j], F32)
        kw_rows = kw_scr[...]
        kw_cols16 = _bf16_columns(_pad_rows_bf16(kw_rows))
        q16 = _pad_rows_bf16(qf)
        v16 = _pad_rows_bf16(v_ref[:, sl].astype(F32))
        _rank1_updates(c_ref, co_ref, o_scr, h, sl, [sc_sm[base + j] for j in range(DEC_NB)], kw_cols16, v16, q16)
        n_new = sc_scr[:, 0:1] * n_ref[:, sl] + kw_rows
        no_ref[:, sl] = n_new
        den = jnp.sum(qf * n_new, axis=1, keepdims=True)
        hid = o_scr[:, sl] / jnp.maximum(jnp.abs(den), eps_scr[:, 0:1])
        hid = og_ref[:, sl].astype(F32) * hid
        y_ref[:, sl] = _ln_gate(hid, gain_ref[:, sl], z_ref[:, sl].astype(F32))


def _hgrn_dec(ps, g, pp, gain3, state, layer, prev_out):
    bsz = ps.shape[0]
    in_specs = [
        pl.BlockSpec((DEC_NB, BW), lambda i: (i, SILU_BLOCKS.index(_HQ))),
        pl.BlockSpec((DEC_NB, BW), lambda i: (i, 0)),
        pl.BlockSpec((DEC_NB, BW), lambda i: (i, PLAIN_BLOCKS.index(_HI))),
        pl.BlockSpec((DEC_NB, BW), lambda i: (i, SILU_BLOCKS.index(_HZ))),
        pl.BlockSpec((None, 1, BW), lambda i: (layer, 0, 0)),
        pl.BlockSpec((1, DEC_NB, HG_HEADS, HG_D, HG_D), lambda i: (layer, i, 0, 0, 0)),
    ]
    args = [ps, g, pp, ps, gain3, state]
    s_spec, s_shape, x_specs, x_args, aliases = _state_chain(
        state.shape, (1, DEC_NB, HG_HEADS, HG_D, HG_D), lambda i: (layer, i, 0, 0, 0), prev_out, len(args), 1)
    return pl.pallas_call(
        _hgrn_dec_kernel,
        grid=(bsz // DEC_NB,),
        in_specs=in_specs + x_specs,
        out_specs=[pl.BlockSpec((DEC_NB, BW), lambda i: (i, 0)), s_spec],
        out_shape=[jax.ShapeDtypeStruct((bsz, BW), BF16), s_shape],
        scratch_shapes=[pltpu.VMEM((DEC_NB, BW), F32)],
        input_output_aliases=aliases,
        compiler_params=_cparams(1),
        name="hgrn_dec",
    )(*args, *x_args)


def _ret_dec(ps, pr, pp, gain3, state, layer, prev_out):
    bsz = ps.shape[0]
    d, nh = RET_D, RET_HEADS
    in_specs = [
        pl.BlockSpec((DEC_NB, BW), lambda i: (i, ROT_BLOCKS.index(_RQ))),
        pl.BlockSpec((DEC_NB, BW), lambda i: (i, ROT_BLOCKS.index(_RK))),
        pl.BlockSpec((DEC_NB, BW), lambda i: (i, PLAIN_BLOCKS.index(_RV))),
        pl.BlockSpec((DEC_NB, BW), lambda i: (i, SILU_BLOCKS.index(_RZ))),
        pl.BlockSpec((None, 1, BW), lambda i: (layer, 0, 0)),
        pl.BlockSpec((1, DEC_NB, nh, d, d), lambda i: (layer, i, 0, 0, 0)),
    ]
    args = [pr, pr, pp, ps, gain3, state]
    s_spec, s_shape, x_specs, x_args, aliases = _state_chain(
        state.shape, (1, DEC_NB, nh, d, d), lambda i: (layer, i, 0, 0, 0), prev_out, len(args), 1)
    return pl.pallas_call(
        _ret_dec_kernel,
        grid=(bsz // DEC_NB,),
        in_specs=in_specs + x_specs,
        out_specs=[pl.BlockSpec((DEC_NB, BW), lambda i: (i, 0)), s_spec],
        out_shape=[jax.ShapeDtypeStruct((bsz, BW), BF16), s_shape],
        scratch_shapes=[pltpu.VMEM((DEC_NB, BW), F32)],
        input_output_aliases=aliases,
        compiler_params=_cparams(1),
        name="ret_dec",
    )(*args, *x_args)


def _mlstm_dec(ps, pp, gain3, w_flat, sc_flat, eps_flat, n_state, c_state, layer, prev_out):
    bsz = ps.shape[0]
    d, nh = ML_D, ML_HEADS
    row = lambda blocks, blk: pl.BlockSpec((DEC_NB, BW), lambda i, *_: (i, blocks.index(blk)))
    in_specs = [
        row(PLAIN_BLOCKS, _MQ), row(PLAIN_BLOCKS, _MK), row(PLAIN_BLOCKS, _MV), row(SILU_BLOCKS, _MZ),
        row(SILU_BLOCKS, _MO),
        pl.BlockSpec((None, 1, BW), lambda i, *_: (layer, 0, 0)),
        pl.BlockSpec((DEC_NB, BW), lambda i, *_: (i, 0)),
        pl.BlockSpec((1, DEC_NB, nh, d, d), lambda i, *_: (layer, i, 0, 0, 0)),
    ]
    args = [pp, pp, pp, ps, ps, gain3, n_state, c_state]
    n_prefetch = 3
    c_spec, c_shape, x_specs, x_args, aliases = _state_chain(
        c_state.shape, (1, DEC_NB, nh, d, d), lambda i, *_: (layer, i, 0, 0, 0), prev_out,
        n_prefetch + len(args), 2)
    grid_spec = pltpu.PrefetchScalarGridSpec(
        num_scalar_prefetch=n_prefetch,
        grid=(bsz // DEC_NB,),
        in_specs=in_specs + x_specs,
        out_specs=[pl.BlockSpec((DEC_NB, BW), lambda i, *_: (i, 0)),
                   pl.BlockSpec((DEC_NB, BW), lambda i, *_: (i, 0)),
                   c_spec],
        scratch_shapes=[pltpu.VMEM((DEC_NB, BW), F32), pltpu.VMEM((DEC_NB, d), F32),
                        pltpu.VMEM((DEC_NB, 128), F32), pltpu.VMEM((DEC_NB, 128), F32)],
    )
    return pl.pallas_call(
        functools.partial(_mlstm_dec_kernel, bsz=bsz),
        grid_spec=grid_spec,
        out_shape=[jax.ShapeDtypeStruct((bsz, BW), BF16), jax.ShapeDtypeStruct((bsz, BW), F32), c_shape],
        input_output_aliases=aliases,
        compiler_params=_cparams(1),
        name="mlstm_dec",
    )(w_flat, sc_flat, eps_flat, *args, *x_args)


def _rotary_tables(pos):
    theta = 1.0 / (ROPE_BASE ** jnp.linspace(0.0, 1.0, RET_D // 2, dtype=F32))
    ang = pos.astype(F32)[:, None] * theta[None]
    cos = jnp.repeat(jnp.cos(ang), 2, axis=1)
    sin = jnp.repeat(jnp.sin(ang), 2, axis=1)
    sign = jnp.tile(jnp.asarray([-1.0, 1.0], F32), RET_D // 2)
    return cos, sin * sign[None]


def _rows3(a):
    return a.astype(F32)[:, None, :]


def _tile_sizes(n_tokens, seq):
    return min(1024, seq), min(1024, n_tokens), 512, min(512, n_tokens)


def kernel(x_prompt, x_sample, state_hgrn, state_ret, state_mlstm_c, state_mlstm_n, state_mlstm_m,
           w_in, hgrn_lb_logits, hgrn_norm, ret_norm, mlstm_norm, mlstm_b_i, mlstm_b_f,
           w_branch, w_out, ln_g, ln_b):
    bp, lp, _ = x_prompt.shape
    bs, ls, _ = x_sample.shape
    assert ls == 1 and w_in.dtype == F32
    lbs3 = _lower_bounds(hgrn_lb_logits.astype(F32))[:, None, :]
    assert w_in.shape[-1] == GATE_COL0 + 3 * D_MODEL
    w_in = jnp.swapaxes(w_in, 1, 2)
    w_gate, w_if = _gate_prep(w_in)
    b_if = jnp.concatenate([mlstm_b_i, mlstm_b_f], axis=-1).astype(F32)[:, :, None]
    w_br = w_branch.astype(BF16)
    w_o = w_out.astype(BF16)
    hg_gain, ret_gain, ml_gain = _rows3(hgrn_norm), _rows3(ret_norm), _rows3(mlstm_norm)
    lng3, lnb3 = _rows3(ln_g), _rows3(ln_b)
    cos_p, sin_p = _rotary_tables(jnp.arange(lp))
    cos_s, sin_s = _rotary_tables(jnp.full((bs,), PAST_LEN))

    tp, tmm_p, tnm, tmo_p = _tile_sizes(bp * lp, lp)
    _, tmm_s, _, tmo_s = _tile_sizes(bs, bs)
    x_p = x_prompt.reshape(bp * lp, D_MODEL)
    x_s = x_sample.reshape(bs, D_MODEL)
    xb_p, xb_s = x_p.astype(BF16), x_s.astype(BF16)
    hg_p = ret_p = mc_p = hg_s = ret_s = mc_s = None
    mn_l, mm_l, mn_sl, mm_sl = [], [], [], []
    for l in range(DEPTH):
        g, gt, ps, pr, pp = _project_all(xb_p, xb_s, w_in, l, lbs3, w_if, b_if, cos_p, sin_p, cos_s, sin_s, tm=tp)
        ya, hg_p, yb, ret_p, yc, mc_p, mn, mm = _scan_prompt(
            g[0], gt[0], ps[0], pr[0], pp[0], hg_gain, ret_gain, ml_gain, l,
            None if l == 0 else (hg_p, ret_p, mc_p), bsz=bp, seq=lp, cs=min(128, lp))
        merged = _merge(xb_p, ya, yb, yc, w_gate, w_br, l, tm=tmm_p, tn=tnm)
        x_p, xb_p = _outproj(merged, x_p, w_o, lng3, lnb3, l, tm=tmo_p)
        mn_l.append(mn.reshape(bp, ML_HEADS, ML_D))
        mm_l.append(mm.reshape(bp, ML_HEADS, 128)[:, :, 0])
        ya, hg_s = _hgrn_dec(ps[1], g[1], pp[1], hg_gain, state_hgrn, l, hg_s)
        yb, ret_s = _ret_dec(ps[1], pr[1], pp[1], ret_gain, state_ret, l, ret_s)
        m_new, w_t, sc_t, eps_t = _mlstm_gate_dec(gt[1], state_mlstm_m[l].T)
        yc, mn, mc_s = _mlstm_dec(ps[1], pp[1], ml_gain, w_t.reshape(-1), sc_t.reshape(-1), eps_t.reshape(-1),
                                  state_mlstm_n[l].reshape(bs, BW), state_mlstm_c, l, mc_s)
        merged = _merge(xb_s, ya, yb, yc, w_gate, w_br, l, tm=tmm_s, tn=tnm)
        x_s, xb_s = _outproj(merged, x_s, w_o, lng3, lnb3, l, tm=tmo_s)
        mn_sl.append(mn.reshape(bs, ML_HEADS, ML_D))
        mm_sl.append(m_new.T)
    y_prompt = x_p.reshape(bp, lp, D_MODEL)
    y_sample = x_s.reshape(bs, 1, D_MODEL)

    return (y_prompt, y_sample, hg_p, ret_p, mc_p, jnp.stack(mn_l), jnp.stack(mm_l),
            hg_s, ret_s, mc_s, jnp.stack(mn_sl), jnp.stack(mm_sl))
```

```python
import functools
import math

import jax
import jax.numpy as jnp
from jax import lax
from jax.experimental import pallas as pl
from jax.experimental.pallas import tpu as pltpu

F32 = jnp.float32
BF16 = jnp.bfloat16
HIGHEST = lax.Precision.HIGHEST

D_MODEL = 2048
DEPTH = 4
PAST_LEN = 16384
BW = D_MODEL // 2
HG_HEADS, HG_D = 8, 128
RET_HEADS, RET_D = 4, 256
ML_HEADS, ML_D = 4, 256
HG_LB_FLOOR = 1e-30
ROPE_BASE = 10000.0
NEG_LARGE = -1e30
NORM_EPS = 1e-6
LN_EPS = 1e-5
DEEPNORM_ALPHA = (2 * DEPTH) ** 0.25
LOG2_E = math.log2(math.e)

_HQ, _HF, _HI, _HZ, _RQ, _RK, _RV, _RZ, _MQ, _MK, _MV, _MZ, _MO = range(13)
SILU_BLOCKS = (_HQ, _HZ, _RZ, _MZ, _MO)
ROT_BLOCKS = (_RQ, _RK)
PLAIN_BLOCKS = (_HI, _RV, _MQ, _MK, _MV)
IF_COL0 = 13 * BW
GATE_COL0 = 13 * BW + 2 * ML_HEADS

V7X_VMEM_LIMIT = 56 * 1024 * 1024
V7X_MXU_COLS = 256


def _cparams(n_grid):
    return pltpu.CompilerParams(dimension_semantics=("arbitrary",) * n_grid, vmem_limit_bytes=V7X_VMEM_LIMIT)


def _sigmoid(x):
    return 1.0 / (1.0 + jnp.exp(-x))


def _silu(x):
    return x * _sigmoid(x)


def _log_sigmoid(x):
    return jnp.minimum(x, 0.0) - jnp.log1p(jnp.exp(-jnp.abs(x)))


def _iota(shape, dim):
    return lax.broadcasted_iota(jnp.int32, shape, dim)


def _dot(a, b):
    return jnp.dot(a, b, preferred_element_type=F32)


def _dot_t(a, b, precision=None):
    return lax.dot_general(a, b, (((1,), (1,)), ((), ())), preferred_element_type=F32, precision=precision)


def _lookup(j, table):
    out = table[0]
    for k in range(1, len(table)):
        out = jnp.where(j == k, table[k], out)
    return out


def _lb_kernel(logit_ref, lb_ref):
    z = logit_ref[...]
    rows = [z[l:l + 1] for l in range(DEPTH)]
    mx = functools.reduce(jnp.maximum, rows)
    ex = [jnp.exp(r - mx) for r in rows]
    tot = functools.reduce(lambda a, b: a + b, ex)
    sm = [e / tot for e in ex]
    run = sm[0]
    lb_ref[0:1, :] = run - sm[0]
    for l in range(1, DEPTH):
        run = run + sm[l]
        lb_ref[l:l + 1, :] = run - sm[0]


def _lower_bounds(logits):
    return pl.pallas_call(_lb_kernel, out_shape=jax.ShapeDtypeStruct(logits.shape, F32), name="lower_bounds")(logits)


GATE_PREP_TN = 512
XPOSE_CHUNK = 256


def _transpose_cast(w_ref, wb_ref):
    for r in range(w_ref.shape[1] // XPOSE_CHUNK):
        ks = slice(r * XPOSE_CHUNK, (r + 1) * XPOSE_CHUNK)
        wb_ref[ks, :] = w_ref[:, ks].T.astype(BF16)


def _gate_prep_kernel(wt_ref, wif_rows_ref, wg_ref, wif_ref):
    _transpose_cast(wt_ref, wg_ref)

    @pl.when(pl.program_id(1) == 0)
    def _():
        rows = wif_rows_ref[...]
        padded = jnp.concatenate([rows, jnp.zeros((128 - rows.shape[0], D_MODEL), F32)], axis=0)
        wif_ref[...] = padded.T.astype(BF16)


def _gate_prep(w_t):
    tn = GATE_PREP_TN
    assert GATE_COL0 % 8 == 0 and IF_COL0 % 8 == 0
    return pl.pallas_call(
        _gate_prep_kernel,
        grid=(DEPTH, 3 * D_MODEL // tn),
        in_specs=[pl.BlockSpec((None, pl.Element(tn), pl.Element(D_MODEL)), lambda l, j: (l, pl.multiple_of(GATE_COL0 + j * tn, 8), 0)),
                  pl.BlockSpec((None, 2 * ML_HEADS, D_MODEL), lambda l, j: (l, IF_COL0 // (2 * ML_HEADS), 0))],
        out_specs=[pl.BlockSpec((None, D_MODEL, tn), lambda l, j: (l, 0, j)),
                   pl.BlockSpec((None, D_MODEL, 128), lambda l, j: (l, 0, 0))],
        out_shape=[jax.ShapeDtypeStruct((DEPTH, D_MODEL, 3 * D_MODEL), BF16),
                   jax.ShapeDtypeStruct((DEPTH, D_MODEL, 128), BF16)],
        compiler_params=_cparams(2),
        name="gate_prep",
    )(w_t, w_t)


def _proj_body(xb_ref, wb_ref, out_ref, epilogue):
    for s in range(wb_ref.shape[1] // V7X_MXU_COLS):
        cols = slice(s * V7X_MXU_COLS, (s + 1) * V7X_MXU_COLS)
        out_ref[:, cols] = epilogue(_dot(xb_ref[...], wb_ref[:, cols]), cols).astype(out_ref.dtype)


def _cast_weights(w_ref, wb_ref):
    @pl.when(pl.program_id(1) == 0)
    def _():
        _transpose_cast(w_ref, wb_ref)


def _proj_two_groups(xp_ref, xs_ref, wb_ref, op_ref, os_ref, epilogue_p, epilogue_s):
    last = pl.program_id(1) == pl.num_programs(1) - 1
    tm = xp_ref.shape[0]

    @pl.when(jnp.logical_not(last))
    def _():
        _proj_body(xp_ref, wb_ref, op_ref, epilogue_p)

    @pl.when(last)
    def _():
        x_all = jnp.concatenate([xp_ref[...], xs_ref[...]], axis=0)
        for s in range(wb_ref.shape[1] // V7X_MXU_COLS):
            cols = slice(s * V7X_MXU_COLS, (s + 1) * V7X_MXU_COLS)
            acc = _dot(x_all, wb_ref[:, cols])
            op_ref[:, cols] = epilogue_p(acc[:tm], cols).astype(op_ref.dtype)
            os_ref[:, cols] = epilogue_s(acc[tm:], cols).astype(os_ref.dtype)


def _proj_silu_kernel(xp_ref, xs_ref, w_ref, op_ref, os_ref, wb_ref, *, sigmoid_block):
    _cast_weights(w_ref, wb_ref)
    plain_sigmoid = pl.program_id(0) == sigmoid_block

    def epilogue(acc, cols):
        s = _sigmoid(acc)
        return jnp.where(plain_sigmoid, s, acc * s)

    _proj_two_groups(xp_ref, xs_ref, wb_ref, op_ref, os_ref, epilogue, epilogue)


def _proj_plain_kernel(xp_ref, xs_ref, w_ref, op_ref, os_ref, wb_ref, *, scaled_block, scale):
    _cast_weights(w_ref, wb_ref)
    sc = jnp.where(pl.program_id(0) == scaled_block, scale, 1.0).astype(F32)
    epilogue = lambda acc, cols: acc * sc
    _proj_two_groups(xp_ref, xs_ref, wb_ref, op_ref, os_ref, epilogue, epilogue)


def _proj_rotary_kernel(xp_ref, xs_ref, w_ref, cosp_ref, sinp_ref, coss_ref, sins_ref, op_ref, os_ref, wb_ref, *,
                        scaled_block, scale):
    _cast_weights(w_ref, wb_ref)
    sc = jnp.where(pl.program_id(0) == scaled_block, scale, 1.0).astype(F32)
    assert V7X_MXU_COLS == RET_D

    def rotate(cos_ref, sin_ref):
        def epilogue(acc, cols):
            even = (_iota(acc.shape, 1) % 2) == 0
            swapped = jnp.where(even, pltpu.roll(acc, RET_D - 1, 1), pltpu.roll(acc, 1, 1))
            return (acc * cos_ref[...] + swapped * sin_ref[...]) * sc
        return epilogue

    _proj_two_groups(xp_ref, xs_ref, wb_ref, op_ref, os_ref, rotate(cosp_ref, sinp_ref), rotate(coss_ref, sins_ref))


def _proj_logf_kernel(xp_ref, xs_ref, w_ref, lb_ref, wif_ref, bif_ref, gp_ref, gs_ref, gtp_ref, gts_ref, wb_ref):
    _cast_weights(w_ref, wb_ref)

    def gates(x_ref, gt_ref):
        pt = _dot(x_ref[...], wif_ref[...]).T[0:2 * ML_HEADS, :] + bif_ref[...]
        gt_ref[...] = jnp.where(_iota(pt.shape, 0) < ML_HEADS, pt, _log_sigmoid(pt))

    gates(xp_ref, gtp_ref)

    @pl.when(pl.program_id(1) == pl.num_programs(1) - 1)
    def _():
        gates(xs_ref, gts_ref)

    def epilogue(acc, cols):
        lb = lb_ref[:, cols]
        return jnp.log(jnp.maximum(lb, HG_LB_FLOOR) + (1.0 - lb) * _sigmoid(acc))

    _proj_two_groups(xp_ref, xs_ref, wb_ref, gp_ref, gs_ref, epilogue, epilogue)


def _proj(kern, xp, xs, w_t, layer, blocks, extra_in, extra_specs, out_dtype, *, tm, extra_out=(), extra_out_specs=()):
    tp, ts = xp.shape[0], xs.shape[0]
    n_prompt = tp // tm
    out_shape = [jax.ShapeDtypeStruct((tp, len(blocks) * BW), out_dtype),
                 jax.ShapeDtypeStruct((ts, len(blocks) * BW), out_dtype)] + list(extra_out)
    out_specs = [pl.BlockSpec((tm, BW), lambda j, i: (i, j)),
                 pl.BlockSpec((ts, BW), lambda j, i: (0, j))] + list(extra_out_specs)
    return pl.pallas_call(
        kern,
        grid=(len(blocks), n_prompt),
        in_specs=[pl.BlockSpec((tm, D_MODEL), lambda j, i: (i, 0)),
                  pl.BlockSpec((ts, D_MODEL), lambda j, i: (0, 0)),
                  pl.BlockSpec((None, BW, D_MODEL), lambda j, i: (layer, _lookup(j, blocks), 0))] + list(extra_specs),
        out_specs=out_specs,
        out_shape=out_shape,
        scratch_shapes=[pltpu.VMEM((D_MODEL, BW), BF16)],
        compiler_params=_cparams(2),
        name="proj",
    )(xp, xs, w_t, *extra_in)


def _project_all(xp, xs, w_t, layer, lbs3, w_if, b_if, cos_p, sin_p, cos_s, sin_s, *, tm):
    tp, ts = xp.shape[0], xs.shape[0]
    n_pos = cos_p.shape[0] // tm
    g_p, g_s, gt_p, gt_s = _proj(
        _proj_logf_kernel, xp, xs, w_t, layer, (_HF,),
        (lbs3, w_if, b_if),
        (pl.BlockSpec((None, 1, BW), lambda j, i: (layer, 0, 0)),
         pl.BlockSpec((None, D_MODEL, 128), lambda j, i: (layer, 0, 0)),
         pl.BlockSpec((None, 2 * ML_HEADS, 1), lambda j, i: (layer, 0, 0))),
        F32, tm=tm,
        extra_out=(jax.ShapeDtypeStruct((2 * ML_HEADS, tp), F32), jax.ShapeDtypeStruct((2 * ML_HEADS, ts), F32)),
        extra_out_specs=(pl.BlockSpec((2 * ML_HEADS, tm), lambda j, i: (0, i)),
                         pl.BlockSpec((2 * ML_HEADS, ts), lambda j, i: (0, 0))))
    ps = _proj(functools.partial(_proj_silu_kernel, sigmoid_block=SILU_BLOCKS.index(_MO)),
               xp, xs, w_t, layer, SILU_BLOCKS, (), (), BF16, tm=tm)
    pos_spec = pl.BlockSpec((tm, RET_D), lambda j, i: (i % n_pos, 0))
    one_spec = pl.BlockSpec((ts, RET_D), lambda j, i: (0, 0))
    pr = _proj(functools.partial(_proj_rotary_kernel, scaled_block=ROT_BLOCKS.index(_RK), scale=RET_D ** -0.5),
               xp, xs, w_t, layer, ROT_BLOCKS, (cos_p, sin_p, cos_s, sin_s),
               (pos_spec, pos_spec, one_spec, one_spec), BF16, tm=tm)
    pp = _proj(functools.partial(_proj_plain_kernel, scaled_block=PLAIN_BLOCKS.index(_MK), scale=ML_D ** -0.5),
               xp, xs, w_t, layer, PLAIN_BLOCKS, (), (), BF16, tm=tm)
    return (g_p, g_s), (gt_p, gt_s), ps, pr, pp


def _rms_gate(o, gain, zs):
    y = o * lax.rsqrt(jnp.mean(o * o, axis=-1, keepdims=True) + NORM_EPS)
    return (y * gain * zs).astype(BF16)


def _ln_gate(o, gain, zs):
    mu = jnp.mean(o, axis=-1, keepdims=True)
    d = o - mu
    var = jnp.mean(d * d, axis=-1, keepdims=True)
    return (d * lax.rsqrt(var + NORM_EPS) * gain * zs).astype(BF16)


def _state_chain(state_shape, block, index_map, prev_out, n_in, out_pos):
    spec = pl.BlockSpec(block, index_map)
    shape = jax.ShapeDtypeStruct(state_shape, F32)
    if prev_out is None:
        return spec, shape, [], [], {}
    return spec, shape, [pl.BlockSpec(memory_space=pl.ANY)], [prev_out], {n_in: out_pos}


def _hgrn_level_ref(bc, m):
    cs, d = bc.shape
    if m >= 8:
        parts = [jnp.broadcast_to(bc[b * 2 * m + m - 1:b * 2 * m + m], (2 * m, d)) for b in range(cs // (2 * m))]
        return parts[0] if len(parts) == 1 else jnp.concatenate(parts, axis=0)
    x3 = bc.reshape(cs // 8, 8, d)
    sub = _iota((cs // 8, 8, d), 1)
    pick = lambda r: jnp.broadcast_to(x3[:, r:r + 1, :], x3.shape)
    if m == 4:
        r3 = pick(3)
    elif m == 2:
        r3 = jnp.where(sub < 4, pick(1), pick(5))
    else:
        r3 = jnp.where(sub < 2, pick(0), jnp.where(sub < 4, pick(2), jnp.where(sub < 6, pick(4), pick(6))))
    return r3.reshape(cs, d)


def _hgrn_stages(q_ref, g_ref, v_ref, z_ref, gain_ref, y_ref, s_ref):
    c = pl.program_id(1)
    cs = q_ref.shape[0]

    @pl.when(c == 0)
    def _():
        s_ref[...] = jnp.zeros_like(s_ref)

    row = _iota((cs, cs), 0)
    col = _iota((cs, cs), 1)
    tril = (row >= col).astype(F32)
    level = jnp.where(row > col, 31 - lax.clz(row ^ col), -1)
    eye_c = row == col
    eye_d = _iota((HG_D, HG_D), 0) == _iota((HG_D, HG_D), 1)
    levels = [cs >> (k + 1) for k in range(cs.bit_length() - 1)]
    heads = [slice(h * HG_D, (h + 1) * HG_D) for h in range(HG_HEADS)]
    qb = q_ref[...]
    q = qb.astype(F32)
    g = g_ref[...] * LOG2_E
    kk = 1.0 - jnp.exp2(g)
    kkb = kk.astype(BF16)
    bc = jnp.dot(tril, g, preferred_element_type=F32, precision=HIGHEST)
    qk = q * kk
    a = [jnp.where(eye_c, jnp.sum(qk[:, sl], axis=1, keepdims=True), 0.0) for sl in heads]
    yield
    for m in levels:
        e = jnp.exp2(-jnp.abs(bc - _hgrn_level_ref(bc, m))).astype(BF16)
        qs = qb * e
        ks = kkb * e
        sel = level == m.bit_length() - 1
        a = [jnp.where(sel, _dot_t(qs[:, sl], ks[:, sl]), a_h) for sl, a_h in zip(heads, a)]
        yield
    qe = (q * jnp.exp2(bc)).astype(BF16)
    last = bc[cs - 1:cs]
    k_dec_t = (kk * jnp.exp2(last - bc)).T.astype(BF16)
    dec_row = jnp.exp2(last)
    yield
    for h, sl in enumerate(heads):
        v = v_ref[:, sl]
        s_old = s_ref[0, 0, h]
        o = _dot(qe[:, sl], s_old.astype(BF16)) + _dot(a[h].astype(BF16), v)
        y_ref[:, sl] = _rms_gate(o, gain_ref[:, sl], z_ref[:, sl].astype(F32))
        dec_col = jnp.sum(jnp.where(eye_d, dec_row[:, sl], 0.0), axis=1, keepdims=True)
        s_ref[0, 0, h] = dec_col * s_old + _dot(k_dec_t[sl, :], v)
        if h % 2 == 1:
            yield


def _ret_log_gamma(h):
    return jnp.log(jnp.full((1, 1), 1.0 - 2.0 ** (-5.0 - h), F32))


def _ret_stages(q_ref, k_ref, v_ref, z_ref, gain_ref, y_ref, s_ref):
    c = pl.program_id(1)
    cs = q_ref.shape[0]

    @pl.when(c == 0)
    def _():
        s_ref[...] = jnp.zeros_like(s_ref)

    t_col = _iota((cs, 1), 0).astype(F32)
    rel = (_iota((cs, cs), 0) - _iota((cs, cs), 1)).astype(F32)
    causal = rel >= 0.0
    heads = [slice(h * RET_D, (h + 1) * RET_D) for h in range(RET_HEADS)]
    hr = range(RET_HEADS)
    lg = [_ret_log_gamma(h) for h in hr]
    s_qk = [_dot_t(q_ref[:, sl], k_ref[:, sl]) for sl in heads]
    yield
    s_old = [s_ref[0, 0, h] for h in hr]
    q_s = [_dot(q_ref[:, sl], s_old[h].astype(BF16)) for h, sl in enumerate(heads)]
    yield
    k_dec_t = [(k_ref[:, sl].astype(F32) * jnp.exp((cs - 1.0 - t_col) * lg[h])).T.astype(BF16)
               for h, sl in enumerate(heads)]
    upd = [_dot(k_dec_t[h], v_ref[:, sl]) for h, sl in enumerate(heads)]
    yield
    a = [s_qk[h] * jnp.where(causal, jnp.exp(jnp.where(causal, rel * lg[h], 0.0)), 0.0) for h in hr]
    a_v = [_dot(a[h].astype(BF16), v_ref[:, sl]) for h, sl in enumerate(heads)]
    yield
    for h, sl in enumerate(heads):
        o = jnp.exp((t_col + 1.0) * lg[h]) * q_s[h] + a_v[h]
        y_ref[:, sl] = _ln_gate(o, gain_ref[:, sl], z_ref[:, sl].astype(F32))
        s_ref[0, 0, h] = jnp.exp(cs * lg[h]) * s_old[h] + upd[h]
        if h % 2 == 1:
            yield


def _mlstm_stages(q_ref, k_ref, v_ref, z_ref, og_ref, gt_ref, gain_ref, y_ref, c_ref, n_ref, m_ref):
    c = pl.program_id(1)
    cs = q_ref.shape[0]

    @pl.when(c == 0)
    def _():
        c_ref[...] = jnp.zeros_like(c_ref)
        n_ref[...] = jnp.zeros_like(n_ref)
        m_ref[...] = jnp.zeros_like(m_ref)

    row = _iota((cs, cs), 0)
    col = _iota((cs, cs), 1)
    causal = row >= col
    tril = causal.astype(F32)
    triu = (row <= col).astype(F32)
    eye = (row == col).astype(F32)
    gi = gt_ref[0:ML_HEADS, :]
    gf = gt_ref[ML_HEADS:2 * ML_HEADS, :]
    b_rows = jnp.dot(gf, triu, preferred_element_type=F32, precision=HIGHEST)
    b_cols = _dot_t(tril, gf, precision=HIGHEST)
    i_cols = _dot_t(eye, gi, precision=HIGHEST)
    heads = [slice(h * ML_D, (h + 1) * ML_D) for h in range(ML_HEADS)]
    hr = range(ML_HEADS)
    s_qk = [_dot_t(q_ref[:, sl], k_ref[:, sl]) for sl in heads]
    yield
    c_old = [c_ref[0, 0, h] for h in hr]
    q_c = [_dot(q_ref[:, sl], c_old[h].astype(BF16)) for h, sl in enumerate(heads)]
    yield
    m_old = [m_ref[0, :, h * 128:h * 128 + 1] for h in hr]
    b_col = [b_cols[:, h:h + 1] for h in hr]
    log_inter = [m_old[h] + b_col[h] for h in hr]
    log_w = [jnp.where(causal, b_col[h] - b_rows[h:h + 1, :] + gi[h:h + 1, :], NEG_LARGE) for h in hr]
    m = [jnp.maximum(log_inter[h], jnp.max(log_w[h], axis=1, keepdims=True)) for h in hr]
    yield
    qk = [s_qk[h] * jnp.where(causal, jnp.exp(jnp.where(causal, log_w[h] - m[h], 0.0)), 0.0) for h in hr]
    sc = [jnp.exp(log_inter[h] - m[h]) for h in hr]
    num = [_dot(qk[h].astype(BF16), v_ref[:, sl]) + sc[h] * q_c[h] for h, sl in enumerate(heads)]
    yield
    m_last = [m[h][cs - 1:cs] for h in hr]
    b_last = [b_col[h][cs - 1:cs] for h in hr]
    kw = [k_ref[:, sl].astype(F32) * jnp.exp(b_last[h] - b_col[h] + i_cols[:, h:h + 1] - m_last[h])
          for h, sl in enumerate(heads)]
    upd = [_dot(kw[h].T.astype(BF16), v_ref[:, sl]) for h, sl in enumerate(heads)]
    yield
    for h, sl in enumerate(heads):
        n_old = n_ref[0, :, sl]
        den = (jnp.sum(qk[h], axis=1, keepdims=True)
               + sc[h] * jnp.sum(q_ref[:, sl].astype(F32) * n_old, axis=1, keepdims=True))
        hid = num[h] / jnp.maximum(jnp.abs(den), jnp.exp(-m[h]))
        hid = og_ref[:, sl].astype(F32) * hid
        y_ref[:, sl] = _ln_gate(hid, gain_ref[:, sl], z_ref[:, sl].astype(F32))
        dec = jnp.exp(m_old[h] + b_last[h] - m_last[h])
        c_ref[0, 0, h] = dec * c_old[h] + upd[h]
        n_ref[0, :, sl] = dec * n_old + jnp.sum(kw[h], axis=0, keepdims=True)
        m_ref[0, :, h * 128:(h + 1) * 128] = jnp.broadcast_to(m_last[h], (1, 128))
        if h % 2 == 1:
            yield


N_SCAN_IN = 17


def _scan_kernel(*refs):
    ins, outs = refs[:N_SCAN_IN], refs[-8:]
    stages = [_hgrn_stages(*ins[0:5], *outs[0:2]), _ret_stages(*ins[5:10], *outs[2:4]),
              _mlstm_stages(*ins[10:17], *outs[4:8])]
    while stages:
        stages = [s for s in stages if next(s, True) is None]


def _scan_prompt(g, gt, ps, pr, pp, hg_gain, ret_gain, ml_gain, layer, prev, *, bsz, seq, cs):
    nc = seq // cs
    tok = lambda b, c: b * nc + c
    col = lambda blocks, blk: pl.BlockSpec((cs, BW), lambda b, c: (tok(b, c), blocks.index(blk)))
    gain = pl.BlockSpec((None, 1, BW), lambda b, c: (layer, 0, 0))
    in_specs = [
        col(SILU_BLOCKS, _HQ), pl.BlockSpec((cs, BW), lambda b, c: (tok(b, c), 0)), col(PLAIN_BLOCKS, _HI),
        col(SILU_BLOCKS, _HZ), gain,
        col(ROT_BLOCKS, _RQ), col(ROT_BLOCKS, _RK), col(PLAIN_BLOCKS, _RV), col(SILU_BLOCKS, _RZ), gain,
        col(PLAIN_BLOCKS, _MQ), col(PLAIN_BLOCKS, _MK), col(PLAIN_BLOCKS, _MV), col(SILU_BLOCKS, _MZ),
        col(SILU_BLOCKS, _MO), pl.BlockSpec((2 * ML_HEADS, cs), lambda b, c: (0, tok(b, c))), gain,
    ]
    args = [ps, g, pp, ps, hg_gain, pr, pr, pp, ps, ret_gain, pp, pp, pp, ps, ps, gt, ml_gain]
    assert len(args) == N_SCAN_IN
    y_spec = pl.BlockSpec((cs, BW), lambda b, c: (tok(b, c), 0))
    y_shape = jax.ShapeDtypeStruct((bsz * seq, BW), BF16)
    state_specs, state_shapes, aliases = [], [], {}
    for k, (nh, d) in enumerate(((HG_HEADS, HG_D), (RET_HEADS, RET_D), (ML_HEADS, ML_D))):
        spec, shape, x_specs, x_args, alias = _state_chain(
            (DEPTH, bsz, nh, d, d), (1, 1, nh, d, d), lambda b, c: (layer, b, 0, 0, 0),
            None if prev is None else prev[k], len(args), 2 * k + 1)
        state_specs.append(spec)
        state_shapes.append(shape)
        in_specs = in_specs + x_specs
        args = args + x_args
        aliases.update(alias)
    return pl.pallas_call(
        _scan_kernel,
        grid=(bsz, nc),
        in_specs=in_specs,
        out_specs=[y_spec, state_specs[0], y_spec, state_specs[1], y_spec, state_specs[2],
                   pl.BlockSpec((1, 1, BW), lambda b, c: (b, 0, 0)),
                   pl.BlockSpec((1, 1, ML_HEADS * 128), lambda b, c: (b, 0, 0))],
        out_shape=[y_shape, state_shapes[0], y_shape, state_shapes[1], y_shape, state_shapes[2],
                   jax.ShapeDtypeStruct((bsz, 1, BW), F32),
                   jax.ShapeDtypeStruct((bsz, 1, ML_HEADS * 128), F32)],
        input_output_aliases=aliases,
        compiler_params=_cparams(2),
        name="scan_prompt",
    )(*args)


def _merge_kernel(xb_ref, ya_ref, yb_ref, yc_ref, wg0_ref, wg1_ref, wg2_ref, wbr_ref, out_ref):
    xb = xb_ref[...]
    for s in range(out_ref.shape[1] // V7X_MXU_COLS):
        cols = slice(s * V7X_MXU_COLS, (s + 1) * V7X_MXU_COLS)
        merged = None
        for n, (y_ref, wg_ref) in enumerate(((ya_ref, wg0_ref), (yb_ref, wg1_ref), (yc_ref, wg2_ref))):
            term = _sigmoid(_dot(xb, wg_ref[:, cols])) * _dot(y_ref[...], wbr_ref[n, :, cols])
            merged = term if merged is None else merged + term
        out_ref[:, cols] = merged.astype(BF16)


def _merge(xb, ya, yb, yc, w_gate, w_br, layer, *, tm, tn):
    t = xb.shape[0]
    nj = D_MODEL // tn
    gate_spec = lambda n: pl.BlockSpec((None, D_MODEL, tn), lambda i, j: (layer, 0, n * nj + j))
    return pl.pallas_call(
        _merge_kernel,
        grid=(t // tm, nj),
        in_specs=[
            pl.BlockSpec((tm, D_MODEL), lambda i, j: (i, 0)),
            pl.BlockSpec((tm, BW), lambda i, j: (i, 0)),
            pl.BlockSpec((tm, BW), lambda i, j: (i, 0)),
            pl.BlockSpec((tm, BW), lambda i, j: (i, 0)),
            gate_spec(0), gate_spec(1), gate_spec(2),
            pl.BlockSpec((None, 3, BW, tn), lambda i, j: (layer, 0, 0, j)),
        ],
        out_specs=pl.BlockSpec((tm, tn), lambda i, j: (i, j)),
        out_shape=jax.ShapeDtypeStruct((t, D_MODEL), BF16),
        compiler_params=_cparams(2),
        name="merge",
    )(xb, ya, yb, yc, w_gate, w_gate, w_gate, w_br)


def _outproj_kernel(m_ref, x_ref, wo_ref, lng_ref, lnb_ref, xo_ref, xbo_ref):
    for s in range(D_MODEL // V7X_MXU_COLS):
        cols = slice(s * V7X_MXU_COLS, (s + 1) * V7X_MXU_COLS)
        xo_ref[:, cols] = DEEPNORM_ALPHA * x_ref[:, cols] + _dot(m_ref[...], wo_ref[:, cols])
    hres = xo_ref[...]
    mu = jnp.mean(hres, axis=-1, keepdims=True)
    d = hres - mu
    var = jnp.mean(d * d, axis=-1, keepdims=True)
    x_new = d * lax.rsqrt(var + LN_EPS) * lng_ref[...] + lnb_ref[...]
    xo_ref[...] = x_new
    xbo_ref[...] = x_new.astype(BF16)


def _outproj(merged, x, w_o, lng3, lnb3, layer, *, tm):
    t = x.shape[0]
    return pl.pallas_call(
        _outproj_kernel,
        grid=(t // tm,),
        in_specs=[
            pl.BlockSpec((tm, D_MODEL), lambda i: (i, 0)),
            pl.BlockSpec((tm, D_MODEL), lambda i: (i, 0)),
            pl.BlockSpec((None, D_MODEL, D_MODEL), lambda i: (layer, 0, 0)),
            pl.BlockSpec((None, 1, D_MODEL), lambda i: (layer, 0, 0)),
            pl.BlockSpec((None, 1, D_MODEL), lambda i: (layer, 0, 0)),
        ],
        out_specs=[pl.BlockSpec((tm, D_MODEL), lambda i: (i, 0)), pl.BlockSpec((tm, D_MODEL), lambda i: (i, 0))],
        out_shape=[jax.ShapeDtypeStruct((t, D_MODEL), F32), jax.ShapeDtypeStruct((t, D_MODEL), BF16)],
        compiler_params=_cparams(1),
        name="outproj",
    )(merged, x, w_o, lng3, lnb3)


DEC_NB = 8


def _columns(x):
    d = x.shape[1]
    eye = (_iota((d, d), 0) == _iota((d, d), 1)).astype(F32)
    return _dot_t(eye, x, precision=HIGHEST)


def _pad_rows_bf16(x):
    return jnp.concatenate([x, jnp.zeros_like(x)], axis=0).astype(BF16)


def _bf16_columns(x16):
    d = x16.shape[1]
    eye = jnp.where(_iota((d, d), 0) == _iota((d, d), 1), 1.0, 0.0).astype(BF16)
    return _dot_t(eye, x16).astype(BF16)


def _rank1_updates(s_ref, so_ref, o_scr, h, sl, decays, k_cols16, v16, q16):
    d = k_cols16.shape[0]
    lane = _iota(k_cols16.shape, 1)
    k_stack = jnp.concatenate(
        [jnp.where(lane == j, k_cols16, jnp.zeros_like(k_cols16)) for j in range(DEC_NB)], axis=0)
    upd = _dot(k_stack, v16)
    s_bf = []
    for j in range(DEC_NB):
        s_new = decays[j] * s_ref[0, j, h] + upd[j * d:(j + 1) * d]
        so_ref[0, j, h] = s_new
        s_bf.append(s_new.astype(BF16))
    for j in range(DEC_NB):
        o_scr[j:j + 1, sl] = _dot(q16, s_bf[j])[j:j + 1, :]


def _hgrn_dec_kernel(q_ref, g_ref, v_ref, z_ref, gain_ref, s_ref, *rest):
    y_ref, so_ref, o_scr = rest[-3:]
    for h in range(HG_HEADS):
        sl = slice(h * HG_D, (h + 1) * HG_D)
        f = jnp.exp(g_ref[:, sl])
        f_cols = _columns(f)
        k_cols16 = _bf16_columns(_pad_rows_bf16(1.0 - f))
        q16 = _pad_rows_bf16(q_ref[:, sl].astype(F32))
        v16 = _pad_rows_bf16(v_ref[:, sl].astype(F32))
        _rank1_updates(s_ref, so_ref, o_scr, h, sl, [f_cols[:, j:j + 1] for j in range(DEC_NB)], k_cols16, v16, q16)
        y_ref[:, sl] = _rms_gate(o_scr[:, sl], gain_ref[:, sl], z_ref[:, sl].astype(F32))


def _ret_dec_kernel(q_ref, k_ref, v_ref, z_ref, gain_ref, s_ref, *rest):
    y_ref, so_ref, o_scr = rest[-3:]
    for h in range(RET_HEADS):
        sl = slice(h * RET_D, (h + 1) * RET_D)
        gamma = 1.0 - 2.0 ** (-5.0 - h)
        k_cols16 = _bf16_columns(_pad_rows_bf16(k_ref[:, sl].astype(F32)))
        q16 = _pad_rows_bf16(q_ref[:, sl].astype(F32))
        v16 = _pad_rows_bf16(v_ref[:, sl].astype(F32))
        _rank1_updates(s_ref, so_ref, o_scr, h, sl, [gamma] * DEC_NB, k_cols16, v16, q16)
        y_ref[:, sl] = _ln_gate(o_scr[:, sl], gain_ref[:, sl], z_ref[:, sl].astype(F32))


def _mlstm_gate_dec_kernel(gt_ref, m_ref, mo_ref, w_ref, sc_ref, eps_ref):
    gi = gt_ref[0:ML_HEADS, :]
    gf = gt_ref[ML_HEADS:2 * ML_HEADS, :]
    log_inter = m_ref[...] + gf
    m_new = jnp.maximum(log_inter, gi)
    mo_ref[...] = m_new
    w_ref[...] = jnp.exp(gi - m_new)
    sc_ref[...] = jnp.exp(log_inter - m_new)
    eps_ref[...] = jnp.exp(-m_new)


def _mlstm_gate_dec(gt, m_t):
    shp = jax.ShapeDtypeStruct(m_t.shape, F32)
    return pl.pallas_call(_mlstm_gate_dec_kernel, out_shape=[shp, shp, shp, shp], name="mlstm_gate_dec")(gt, m_t)


def _mlstm_dec_kernel(w_sm, sc_sm, eps_sm, q_ref, k_ref, v_ref, z_ref, og_ref, gain_ref, n_ref, c_ref, *rest, bsz):
    y_ref, no_ref, co_ref, o_scr, kw_scr, sc_scr, eps_scr = rest[-7:]
    i = pl.program_id(0)
    for h in range(ML_HEADS):
        sl = slice(h * ML_D, (h + 1) * ML_D)
        base = h * bsz + i * DEC_NB
        qf = q_ref[:, sl].astype(F32)
        k_rows = k_ref[:, sl].astype(F32)
        for j in range(DEC_NB):
            kw_scr[j:j + 1, :] = k_rows[j:j + 1, :] * w_sm[base + j]
            sc_scr[j:j + 1, :] = jnp.full((1, 128), sc_sm[base + j], F32)
            eps_scr[j:j + 1, :] = jnp.full((1, 128), eps_sm[base + j], F32)
        kw_rows = kw_scr[...]
        kw_cols16 = _bf16_columns(_pad_rows_bf16(kw_rows))
        q16 = _pad_rows_bf16(qf)
        v16 = _pad_rows_bf16(v_ref[:, sl].astype(F32))
        _rank1_updates(c_ref, co_ref, o_scr, h, sl, [sc_sm[base + j] for j in range(DEC_NB)], kw_cols16, v16, q16)
        n_new = sc_scr[:, 0:1] * n_ref[:, sl] + kw_rows
        no_ref[:, sl] = n_new
        den = jnp.sum(qf * n_new, axis=1, keepdims=True)
        hid = o_scr[:, sl] / jnp.maximum(jnp.abs(den), eps_scr[:, 0:1])
        hid = og_ref[:, sl].astype(F32) * hid
        y_ref[:, sl] = _ln_gate(hid, gain_ref[:, sl], z_ref[:, sl].astype(F32))


def _hgrn_dec(ps, g, pp, gain3, state, layer, prev_out):
    bsz = ps.shape[0]
    in_specs = [
        pl.BlockSpec((DEC_NB, BW), lambda i: (i, SILU_BLOCKS.index(_HQ))),
        pl.BlockSpec((DEC_NB, BW), lambda i: (i, 0)),
        pl.BlockSpec((DEC_NB, BW), lambda i: (i, PLAIN_BLOCKS.index(_HI))),
        pl.BlockSpec((DEC_NB, BW), lambda i: (i, SILU_BLOCKS.index(_HZ))),
        pl.BlockSpec((None, 1, BW), lambda i: (layer, 0, 0)),
        pl.BlockSpec((1, DEC_NB, HG_HEADS, HG_D, HG_D), lambda i: (layer, i, 0, 0, 0)),
    ]
    args = [ps, g, pp, ps, gain3, state]
    s_spec, s_shape, x_specs, x_args, aliases = _state_chain(
        state.shape, (1, DEC_NB, HG_HEADS, HG_D, HG_D), lambda i: (layer, i, 0, 0, 0), prev_out, len(args), 1)
    return pl.pallas_call(
        _hgrn_dec_kernel,
        grid=(bsz // DEC_NB,),
        in_specs=in_specs + x_specs,
        out_specs=[pl.BlockSpec((DEC_NB, BW), lambda i: (i, 0)), s_spec],
        out_shape=[jax.ShapeDtypeStruct((bsz, BW), BF16), s_shape],
        scratch_shapes=[pltpu.VMEM((DEC_NB, BW), F32)],
        input_output_aliases=aliases,
        compiler_params=_cparams(1),
        name="hgrn_dec",
    )(*args, *x_args)


def _ret_dec(ps, pr, pp, gain3, state, layer, prev_out):
    bsz = ps.shape[0]
    d, nh = RET_D, RET_HEADS
    in_specs = [
        pl.BlockSpec((DEC_NB, BW), lambda i: (i, ROT_BLOCKS.index(_RQ))),
        pl.BlockSpec((DEC_NB, BW), lambda i: (i, ROT_BLOCKS.index(_RK))),
        pl.BlockSpec((DEC_NB, BW), lambda i: (i, PLAIN_BLOCKS.index(_RV))),
        pl.BlockSpec((DEC_NB, BW), lambda i: (i, SILU_BLOCKS.index(_RZ))),
        pl.BlockSpec((None, 1, BW), lambda i: (layer, 0, 0)),
        pl.BlockSpec((1, DEC_NB, nh, d, d), lambda i: (layer, i, 0, 0, 0)),
    ]
    args = [pr, pr, pp, ps, gain3, state]
    s_spec, s_shape, x_specs, x_args, aliases = _state_chain(
        state.shape, (1, DEC_NB, nh, d, d), lambda i: (layer, i, 0, 0, 0), prev_out, len(args), 1)
    return pl.pallas_call(
        _ret_dec_kernel,
        grid=(bsz // DEC_NB,),
        in_specs=in_specs + x_specs,
        out_specs=[pl.BlockSpec((DEC_NB, BW), lambda i: (i, 0)), s_spec],
        out_shape=[jax.ShapeDtypeStruct((bsz, BW), BF16), s_shape],
        scratch_shapes=[pltpu.VMEM((DEC_NB, BW), F32)],
        input_output_aliases=aliases,
        compiler_params=_cparams(1),
        name="ret_dec",
    )(*args, *x_args)


def _mlstm_dec(ps, pp, gain3, w_flat, sc_flat, eps_flat, n_state, c_state, layer, prev_out):
    bsz = ps.shape[0]
    d, nh = ML_D, ML_HEADS
    row = lambda blocks, blk: pl.BlockSpec((DEC_NB, BW), lambda i, *_: (i, blocks.index(blk)))
    in_specs = [
        row(PLAIN_BLOCKS, _MQ), row(PLAIN_BLOCKS, _MK), row(PLAIN_BLOCKS, _MV), row(SILU_BLOCKS, _MZ),
        row(SILU_BLOCKS, _MO),
        pl.BlockSpec((None, 1, BW), lambda i, *_: (layer, 0, 0)),
        pl.BlockSpec((DEC_NB, BW), lambda i, *_: (i, 0)),
        pl.BlockSpec((1, DEC_NB, nh, d, d), lambda i, *_: (layer, i, 0, 0, 0)),
    ]
    args = [pp, pp, pp, ps, ps, gain3, n_state, c_state]
    n_prefetch = 3
    c_spec, c_shape, x_specs, x_args, aliases = _state_chain(
        c_state.shape, (1, DEC_NB, nh, d, d), lambda i, *_: (layer, i, 0, 0, 0), prev_out,
        n_prefetch + len(args), 2)
    grid_spec = pltpu.PrefetchScalarGridSpec(
        num_scalar_prefetch=n_prefetch,
        grid=(bsz // DEC_NB,),
        in_specs=in_specs + x_specs,
        out_specs=[pl.BlockSpec((DEC_NB, BW), lambda i, *_: (i, 0)),
                   pl.BlockSpec((DEC_NB, BW), lambda i, *_: (i, 0)),
                   c_spec],
        scratch_shapes=[pltpu.VMEM((DEC_NB, BW), F32), pltpu.VMEM((DEC_NB, d), F32),
                        pltpu.VMEM((DEC_NB, 128), F32), pltpu.VMEM((DEC_NB, 128), F32)],
    )
    return pl.pallas_call(
        functools.partial(_mlstm_dec_kernel, bsz=bsz),
        grid_spec=grid_spec,
        out_shape=[jax.ShapeDtypeStruct((bsz, BW), BF16), jax.ShapeDtypeStruct((bsz, BW), F32), c_shape],
        input_output_aliases=aliases,
        compiler_params=_cparams(1),
        name="mlstm_dec",
    )(w_flat, sc_flat, eps_flat, *args, *x_args)


def _rotary_tables(pos):
    theta = 1.0 / (ROPE_BASE ** jnp.linspace(0.0, 1.0, RET_D // 2, dtype=F32))
    ang = pos.astype(F32)[:, None] * theta[None]
    cos = jnp.repeat(jnp.cos(ang), 2, axis=1)
    sin = jnp.repeat(jnp.sin(ang), 2, axis=1)
    sign = jnp.tile(jnp.asarray([-1.0, 1.0], F32), RET_D // 2)
    return cos, sin * sign[None]


def _rows3(a):
    return a.astype(F32)[:, None, :]


def _tile_sizes(n_tokens, seq):
    return min(1024, seq), min(1024, n_tokens), 512, min(512, n_tokens)


def kernel(x_prompt, x_sample, state_hgrn, state_ret, state_mlstm_c, state_mlstm_n, state_mlstm_m,
           w_in, hgrn_lb_logits, hgrn_norm, ret_norm, mlstm_norm, mlstm_b_i, mlstm_b_f,
           w_branch, w_out, ln_g, ln_b):
    bp, lp, _ = x_prompt.shape
    bs, ls, _ = x_sample.shape
    assert ls == 1 and w_in.dtype == F32
    lbs3 = _lower_bounds(hgrn_lb_logits.astype(F32))[:, None, :]
    assert w_in.shape[-1] == GATE_COL0 + 3 * D_MODEL
    w_in = jnp.swapaxes(w_in, 1, 2)
    w_gate, w_if = _gate_prep(w_in)
    b_if = jnp.concatenate([mlstm_b_i, mlstm_b_f], axis=-1).astype(F32)[:, :, None]
    w_br = w_branch.astype(BF16)
    w_o = w_out.astype(BF16)
    hg_gain, ret_gain, ml_gain = _rows3(hgrn_norm), _rows3(ret_norm), _rows3(mlstm_norm)
    lng3, lnb3 = _rows3(ln_g), _rows3(ln_b)
    cos_p, sin_p = _rotary_tables(jnp.arange(lp))
    cos_s, sin_s = _rotary_tables(jnp.full((bs,), PAST_LEN))

    tp, tmm_p, tnm, tmo_p = _tile_sizes(bp * lp, lp)
    _, tmm_s, _, tmo_s = _tile_sizes(bs, bs)
    x_p = x_prompt.reshape(bp * lp, D_MODEL)
    x_s = x_sample.reshape(bs, D_MODEL)
    xb_p, xb_s = x_p.astype(BF16), x_s.astype(BF16)
    hg_p = ret_p = mc_p = hg_s = ret_s = mc_s = None
    mn_l, mm_l, mn_sl, mm_sl = [], [], [], []
    for l in range(DEPTH):
        g, gt, ps, pr, pp = _project_all(xb_p, xb_s, w_in, l, lbs3, w_if, b_if, cos_p, sin_p, cos_s, sin_s, tm=tp)
        ya, hg_p, yb, ret_p, yc, mc_p, mn, mm = _scan_prompt(
            g[0], gt[0], ps[0], pr[0], pp[0], hg_gain, ret_gain, ml_gain, l,
            None if l == 0 else (hg_p, ret_p, mc_p), bsz=bp, seq=lp, cs=min(128, lp))
        merged = _merge(xb_p, ya, yb, yc, w_gate, w_br, l, tm=tmm_p, tn=tnm)
        x_p, xb_p = _outproj(merged, x_p, w_o, lng3, lnb3, l, tm=tmo_p)
        mn_l.append(mn.reshape(bp, ML_HEADS, ML_D))
        mm_l.append(mm.reshape(bp, ML_HEADS, 128)[:, :, 0])
        ya, hg_s = _hgrn_dec(ps[1], g[1], pp[1], hg_gain, state_hgrn, l, hg_s)
        yb, ret_s = _ret_dec(ps[1], pr[1], pp[1], ret_gain, state_ret, l, ret_s)
        m_new, w_t, sc_t, eps_t = _mlstm_gate_dec(gt[1], state_mlstm_m[l].T)
        yc, mn, mc_s = _mlstm_dec(ps[1], pp[1], ml_gain, w_t.reshape(-1), sc_t.reshape(-1), eps_t.reshape(-1),
                                  state_mlstm_n[l].reshape(bs, BW), state_mlstm_c, l, mc_s)
        merged = _merge(xb_s, ya, yb, yc, w_gate, w_br, l, tm=tmm_s, tn=tnm)
        x_s, xb_s = _outproj(merged, x_s, w_o, lng3, lnb3, l, tm=tmo_s)
        mn_sl.append(mn.reshape(bs, ML_HEADS, ML_D))
        mm_sl.append(m_new.T)
    y_prompt = x_p.reshape(bp, lp, D_MODEL)
    y_sample = x_s.reshape(bs, 1, D_MODEL)

    return (y_prompt, y_sample, hg_p, ret_p, mc_p, jnp.stack(mn_l), jnp.stack(mm_l),
            hg_s, ret_s, mc_s, jnp.stack(mn_sl), jnp.stack(mm_sl))
```

```python
import functools
import math

import jax
import jax.numpy as jnp
from jax import lax
from jax.experimental import pallas as pl
from jax.experimental.pallas import tpu as pltpu

F32 = jnp.float32
BF16 = jnp.bfloat16
HIGHEST = lax.Precision.HIGHEST

D_MODEL = 2048
DEPTH = 4
PAST_LEN = 16384
BW = D_MODEL // 2
HG_HEADS, HG_D = 8, 128
RET_HEADS, RET_D = 4, 256
ML_HEADS, ML_D = 4, 256
HG_LB_FLOOR = 1e-30
ROPE_BASE = 10000.0
NEG_LARGE = -1e30
NORM_EPS = 1e-6
LN_EPS = 1e-5
DEEPNORM_ALPHA = (2 * DEPTH) ** 0.25
LOG2_E = math.log2(math.e)

_HQ, _HF, _HI, _HZ, _RQ, _RK, _RV, _RZ, _MQ, _MK, _MV, _MZ, _MO = range(13)
SILU_BLOCKS = (_HQ, _HZ, _RZ, _MZ, _MO)
ROT_BLOCKS = (_RQ, _RK)
PLAIN_BLOCKS = (_HI, _RV, _MQ, _MK, _MV)
IF_COL0 = 13 * BW
GATE_COL0 = 13 * BW + 2 * ML_HEADS

V7X_VMEM_LIMIT = 56 * 1024 * 1024
V7X_MXU_COLS = 256


def _cparams(n_grid):
    return pltpu.CompilerParams(dimension_semantics=("arbitrary",) * n_grid, vmem_limit_bytes=V7X_VMEM_LIMIT)


def _sigmoid(x):
    return 1.0 / (1.0 + jnp.exp(-x))


def _round_robin(stages):
    while stages:
        stages = [s for s in stages if next(s, True) is None]


def _log_sigmoid(x):
    return jnp.minimum(x, 0.0) - jnp.log1p(jnp.exp(-jnp.abs(x)))


def _iota(shape, dim):
    return lax.broadcasted_iota(jnp.int32, shape, dim)


def _dot(a, b):
    return jnp.dot(a, b, preferred_element_type=F32)


def _dot_t(a, b, precision=None):
    return lax.dot_general(a, b, (((1,), (1,)), ((), ())), preferred_element_type=F32, precision=precision)


def _lookup(j, table):
    out = table[0]
    for k in range(1, len(table)):
        out = jnp.where(j == k, table[k], out)
    return out


def _lb_kernel(logit_ref, lb_ref):
    z = logit_ref[...]
    rows = [z[l:l + 1] for l in range(DEPTH)]
    mx = functools.reduce(jnp.maximum, rows)
    ex = [jnp.exp(r - mx) for r in rows]
    tot = functools.reduce(lambda a, b: a + b, ex)
    sm = [e / tot for e in ex]
    run = sm[0]
    lb_ref[0:1, :] = run - sm[0]
    for l in range(1, DEPTH):
        run = run + sm[l]
        lb_ref[l:l + 1, :] = run - sm[0]


def _lower_bounds(logits):
    return pl.pallas_call(_lb_kernel, out_shape=jax.ShapeDtypeStruct(logits.shape, F32), name="lower_bounds")(logits)


GATE_PREP_TN = 512
XPOSE_CHUNK = 256


def _transpose_cast(w_ref, wb_ref):
    for r in range(w_ref.shape[1] // XPOSE_CHUNK):
        ks = slice(r * XPOSE_CHUNK, (r + 1) * XPOSE_CHUNK)
        wb_ref[ks, :] = w_ref[:, ks].T.astype(BF16)


def _gate_prep_kernel(wt_ref, wif_rows_ref, wg_ref, wif_ref):
    _transpose_cast(wt_ref, wg_ref)

    @pl.when(pl.program_id(1) == 0)
    def _():
        rows = wif_rows_ref[...]
        padded = jnp.concatenate([rows, jnp.zeros((128 - rows.shape[0], D_MODEL), F32)], axis=0)
        wif_ref[...] = padded.T.astype(BF16)


def _gate_prep(w_t):
    tn = GATE_PREP_TN
    assert GATE_COL0 % 8 == 0 and IF_COL0 % 8 == 0
    return pl.pallas_call(
        _gate_prep_kernel,
        grid=(DEPTH, 3 * D_MODEL // tn),
        in_specs=[pl.BlockSpec((None, pl.Element(tn), pl.Element(D_MODEL)), lambda l, j: (l, pl.multiple_of(GATE_COL0 + j * tn, 8), 0)),
                  pl.BlockSpec((None, 2 * ML_HEADS, D_MODEL), lambda l, j: (l, IF_COL0 // (2 * ML_HEADS), 0))],
        out_specs=[pl.BlockSpec((None, D_MODEL, tn), lambda l, j: (l, 0, j)),
                   pl.BlockSpec((None, D_MODEL, 128), lambda l, j: (l, 0, 0))],
        out_shape=[jax.ShapeDtypeStruct((DEPTH, D_MODEL, 3 * D_MODEL), BF16),
                   jax.ShapeDtypeStruct((DEPTH, D_MODEL, 128), BF16)],
        compiler_params=_cparams(2),
        name="gate_prep",
    )(w_t, w_t)


def _proj_body(xb_ref, wb_ref, out_ref, epilogue):
    for s in range(wb_ref.shape[1] // V7X_MXU_COLS):
        cols = slice(s * V7X_MXU_COLS, (s + 1) * V7X_MXU_COLS)
        out_ref[:, cols] = epilogue(_dot(xb_ref[...], wb_ref[:, cols]), cols).astype(out_ref.dtype)


def _cast_weights(w_ref, wb_ref):
    @pl.when(pl.program_id(1) == 0)
    def _():
        _transpose_cast(w_ref, wb_ref)


def _proj_two_groups(xp_ref, xs_ref, wb_ref, op_ref, os_ref, epilogue_p, epilogue_s):
    last = pl.program_id(1) == pl.num_programs(1) - 1
    tm = xp_ref.shape[0]

    @pl.when(jnp.logical_not(last))
    def _():
        _proj_body(xp_ref, wb_ref, op_ref, epilogue_p)

    @pl.when(last)
    def _():
        x_all = jnp.concatenate([xp_ref[...], xs_ref[...]], axis=0)
        for s in range(wb_ref.shape[1] // V7X_MXU_COLS):
            cols = slice(s * V7X_MXU_COLS, (s + 1) * V7X_MXU_COLS)
            acc = _dot(x_all, wb_ref[:, cols])
            op_ref[:, cols] = epilogue_p(acc[:tm], cols).astype(op_ref.dtype)
            os_ref[:, cols] = epilogue_s(acc[tm:], cols).astype(os_ref.dtype)


def _proj_silu_kernel(xp_ref, xs_ref, w_ref, op_ref, os_ref, wb_ref, *, sigmoid_block):
    _cast_weights(w_ref, wb_ref)
    plain_sigmoid = pl.program_id(0) == sigmoid_block

    def epilogue(acc, cols):
        s = _sigmoid(acc)
        return jnp.where(plain_sigmoid, s, acc * s)

    _proj_two_groups(xp_ref, xs_ref, wb_ref, op_ref, os_ref, epilogue, epilogue)


def _proj_plain_kernel(xp_ref, xs_ref, w_ref, op_ref, os_ref, wb_ref, *, scaled_block, scale):
    _cast_weights(w_ref, wb_ref)
    sc = jnp.where(pl.program_id(0) == scaled_block, scale, 1.0).astype(F32)
    epilogue = lambda acc, cols: acc * sc
    _proj_two_groups(xp_ref, xs_ref, wb_ref, op_ref, os_ref, epilogue, epilogue)


def _proj_rotary_kernel(xp_ref, xs_ref, w_ref, cosp_ref, sinp_ref, coss_ref, sins_ref, op_ref, os_ref, wb_ref, *,
                        scaled_block, scale):
    _cast_weights(w_ref, wb_ref)
    sc = jnp.where(pl.program_id(0) == scaled_block, scale, 1.0).astype(F32)
    assert V7X_MXU_COLS == RET_D

    def rotate(cos_ref, sin_ref):
        def epilogue(acc, cols):
            even = (_iota(acc.shape, 1) % 2) == 0
            swapped = jnp.where(even, pltpu.roll(acc, RET_D - 1, 1), pltpu.roll(acc, 1, 1))
            return (acc * cos_ref[...] + swapped * sin_ref[...]) * sc
        return epilogue

    _proj_two_groups(xp_ref, xs_ref, wb_ref, op_ref, os_ref, rotate(cosp_ref, sinp_ref), rotate(coss_ref, sins_ref))


def _proj_logf_kernel(xp_ref, xs_ref, w_ref, lb_ref, wif_ref, bif_ref, gp_ref, gs_ref, gtp_ref, gts_ref, wb_ref):
    _cast_weights(w_ref, wb_ref)

    def gates(x_ref, gt_ref):
        pt = _dot(x_ref[...], wif_ref[...]).T[0:2 * ML_HEADS, :] + bif_ref[...]
        gt_ref[...] = jnp.where(_iota(pt.shape, 0) < ML_HEADS, pt, _log_sigmoid(pt))

    gates(xp_ref, gtp_ref)

    @pl.when(pl.program_id(1) == pl.num_programs(1) - 1)
    def _():
        gates(xs_ref, gts_ref)

    def epilogue(acc, cols):
        lb = lb_ref[:, cols]
        return jnp.log(jnp.maximum(lb, HG_LB_FLOOR) + (1.0 - lb) * _sigmoid(acc))

    _proj_two_groups(xp_ref, xs_ref, wb_ref, gp_ref, gs_ref, epilogue, epilogue)


def _proj(kern, xp, xs, w_t, layer, blocks, extra_in, extra_specs, out_dtype, *, tm, extra_out=(), extra_out_specs=()):
    tp, ts = xp.shape[0], xs.shape[0]
    n_prompt = tp // tm
    out_shape = [jax.ShapeDtypeStruct((tp, len(blocks) * BW), out_dtype),
                 jax.ShapeDtypeStruct((ts, len(blocks) * BW), out_dtype)] + list(extra_out)
    out_specs = [pl.BlockSpec((tm, BW), lambda j, i: (i, j)),
                 pl.BlockSpec((ts, BW), lambda j, i: (0, j))] + list(extra_out_specs)
    return pl.pallas_call(
        kern,
        grid=(len(blocks), n_prompt),
        in_specs=[pl.BlockSpec((tm, D_MODEL), lambda j, i: (i, 0)),
                  pl.BlockSpec((ts, D_MODEL), lambda j, i: (0, 0)),
                  pl.BlockSpec((None, BW, D_MODEL), lambda j, i: (layer, _lookup(j, blocks), 0))] + list(extra_specs),
        out_specs=out_specs,
        out_shape=out_shape,
        scratch_shapes=[pltpu.VMEM((D_MODEL, BW), BF16)],
        compiler_params=_cparams(2),
        name="proj",
    )(xp, xs, w_t, *extra_in)


def _project_all(xp, xs, w_t, layer, lbs3, w_if, b_if, cos_p, sin_p, cos_s, sin_s, *, tm):
    tp, ts = xp.shape[0], xs.shape[0]
    n_pos = cos_p.shape[0] // tm
    g_p, g_s, gt_p, gt_s = _proj(
        _proj_logf_kernel, xp, xs, w_t, layer, (_HF,),
        (lbs3, w_if, b_if),
        (pl.BlockSpec((None, 1, BW), lambda j, i: (layer, 0, 0)),
         pl.BlockSpec((None, D_MODEL, 128), lambda j, i: (layer, 0, 0)),
         pl.BlockSpec((None, 2 * ML_HEADS, 1), lambda j, i: (layer, 0, 0))),
        F32, tm=tm,
        extra_out=(jax.ShapeDtypeStruct((2 * ML_HEADS, tp), F32), jax.ShapeDtypeStruct((2 * ML_HEADS, ts), F32)),
        extra_out_specs=(pl.BlockSpec((2 * ML_HEADS, tm), lambda j, i: (0, i)),
                         pl.BlockSpec((2 * ML_HEADS, ts), lambda j, i: (0, 0))))
    ps = _proj(functools.partial(_proj_silu_kernel, sigmoid_block=SILU_BLOCKS.index(_MO)),
               xp, xs, w_t, layer, SILU_BLOCKS, (), (), BF16, tm=tm)
    pos_spec = pl.BlockSpec((tm, RET_D), lambda j, i: (i % n_pos, 0))
    one_spec = pl.BlockSpec((ts, RET_D), lambda j, i: (0, 0))
    pr = _proj(functools.partial(_proj_rotary_kernel, scaled_block=ROT_BLOCKS.index(_RK), scale=RET_D ** -0.5),
               xp, xs, w_t, layer, ROT_BLOCKS, (cos_p, sin_p, cos_s, sin_s),
               (pos_spec, pos_spec, one_spec, one_spec), BF16, tm=tm)
    pp = _proj(functools.partial(_proj_plain_kernel, scaled_block=PLAIN_BLOCKS.index(_MK), scale=ML_D ** -0.5),
               xp, xs, w_t, layer, PLAIN_BLOCKS, (), (), BF16, tm=tm)
    return (g_p, g_s), (gt_p, gt_s), ps, pr, pp


def _rms_gate(o, gain, zs):
    y = o * lax.rsqrt(jnp.mean(o * o, axis=-1, keepdims=True) + NORM_EPS)
    return (y * gain * zs).astype(BF16)


def _ln_gate(o, gain, zs):
    mu = jnp.mean(o, axis=-1, keepdims=True)
    d = o - mu
    var = jnp.mean(d * d, axis=-1, keepdims=True)
    return (d * lax.rsqrt(var + NORM_EPS) * gain * zs).astype(BF16)


def _state_chain(state_shape, block, index_map, prev_out, n_in, out_pos):
    spec = pl.BlockSpec(block, index_map)
    shape = jax.ShapeDtypeStruct(state_shape, F32)
    if prev_out is None:
        return spec, shape, [], [], {}
    return spec, shape, [pl.BlockSpec(memory_space=pl.ANY)], [prev_out], {n_in: out_pos}


def _hgrn_level_ref(bc, m):
    cs, d = bc.shape
    if m >= 8:
        parts = [jnp.broadcast_to(bc[b * 2 * m + m - 1:b * 2 * m + m], (2 * m, d)) for b in range(cs // (2 * m))]
        return parts[0] if len(parts) == 1 else jnp.concatenate(parts, axis=0)
    x3 = bc.reshape(cs // 8, 8, d)
    sub = _iota((cs // 8, 8, d), 1)
    pick = lambda r: jnp.broadcast_to(x3[:, r:r + 1, :], x3.shape)
    if m == 4:
        r3 = pick(3)
    elif m == 2:
        r3 = jnp.where(sub < 4, pick(1), pick(5))
    else:
        r3 = jnp.where(sub < 2, pick(0), jnp.where(sub < 4, pick(2), jnp.where(sub < 6, pick(4), pick(6))))
    return r3.reshape(cs, d)


def _hgrn_stages(q_ref, g_ref, v_ref, z_ref, gain_ref, y_ref, s_ref):
    c = pl.program_id(1)
    cs = q_ref.shape[0]

    @pl.when(c == 0)
    def _():
        s_ref[...] = jnp.zeros_like(s_ref)

    row = _iota((cs, cs), 0)
    col = _iota((cs, cs), 1)
    tril = (row >= col).astype(F32)
    level = jnp.where(row > col, 31 - lax.clz(row ^ col), -1)
    eye_c = row == col
    eye_d = _iota((HG_D, HG_D), 0) == _iota((HG_D, HG_D), 1)
    levels = [cs >> (k + 1) for k in range(cs.bit_length() - 1)]
    heads = [slice(h * HG_D, (h + 1) * HG_D) for h in range(HG_HEADS)]
    qb = q_ref[...]
    q = qb.astype(F32)
    g = g_ref[...] * LOG2_E
    kk = 1.0 - jnp.exp2(g)
    kkb = kk.astype(BF16)
    bc = jnp.dot(tril, g, preferred_element_type=F32, precision=HIGHEST)
    qk = q * kk
    a = [jnp.where(eye_c, jnp.sum(qk[:, sl], axis=1, keepdims=True), 0.0) for sl in heads]
    yield
    for m in levels:
        e = jnp.exp2(-jnp.abs(bc - _hgrn_level_ref(bc, m))).astype(BF16)
        qs = qb * e
        ks = kkb * e
        sel = level == m.bit_length() - 1
        a = [jnp.where(sel, _dot_t(qs[:, sl], ks[:, sl]), a_h) for sl, a_h in zip(heads, a)]
        yield
    qe = (q * jnp.exp2(bc)).astype(BF16)
    last = bc[cs - 1:cs]
    k_dec_t = (kk * jnp.exp2(last - bc)).T.astype(BF16)
    dec_row = jnp.exp2(last)
    yield
    for h, sl in enumerate(heads):
        v = v_ref[:, sl]
        s_old = s_ref[0, 0, h]
        o = _dot(qe[:, sl], s_old.astype(BF16)) + _dot(a[h].astype(BF16), v)
        y_ref[:, sl] = _rms_gate(o, gain_ref[:, sl], z_ref[:, sl].astype(F32))
        dec_col = jnp.sum(jnp.where(eye_d, dec_row[:, sl], 0.0), axis=1, keepdims=True)
        s_ref[0, 0, h] = dec_col * s_old + _dot(k_dec_t[sl, :], v)
        if h % 2 == 1:
            yield


def _ret_log_gamma(h):
    return jnp.log(jnp.full((1, 1), 1.0 - 2.0 ** (-5.0 - h), F32))


def _ret_stages(q_ref, k_ref, v_ref, z_ref, gain_ref, y_ref, s_ref):
    c = pl.program_id(1)
    cs = q_ref.shape[0]

    @pl.when(c == 0)
    def _():
        s_ref[...] = jnp.zeros_like(s_ref)

    t_col = _iota((cs, 1), 0).astype(F32)
    rel = (_iota((cs, cs), 0) - _iota((cs, cs), 1)).astype(F32)
    causal = rel >= 0.0
    heads = [slice(h * RET_D, (h + 1) * RET_D) for h in range(RET_HEADS)]
    hr = range(RET_HEADS)
    lg = [_ret_log_gamma(h) for h in hr]
    s_qk = [_dot_t(q_ref[:, sl], k_ref[:, sl]) for sl in heads]
    yield
    s_old = [s_ref[0, 0, h] for h in hr]
    q_s = [_dot(q_ref[:, sl], s_old[h].astype(BF16)) for h, sl in enumerate(heads)]
    yield
    k_dec_t = [(k_ref[:, sl].astype(F32) * jnp.exp((cs - 1.0 - t_col) * lg[h])).T.astype(BF16)
               for h, sl in enumerate(heads)]
    upd = [_dot(k_dec_t[h], v_ref[:, sl]) for h, sl in enumerate(heads)]
    yield
    a = [s_qk[h] * jnp.where(causal, jnp.exp(jnp.where(causal, rel * lg[h], 0.0)), 0.0) for h in hr]
    a_v = [_dot(a[h].astype(BF16), v_ref[:, sl]) for h, sl in enumerate(heads)]
    yield
    for h, sl in enumerate(heads):
        o = jnp.exp((t_col + 1.0) * lg[h]) * q_s[h] + a_v[h]
        y_ref[:, sl] = _ln_gate(o, gain_ref[:, sl], z_ref[:, sl].astype(F32))
        s_ref[0, 0, h] = jnp.exp(cs * lg[h]) * s_old[h] + upd[h]
        if h % 2 == 1:
            yield


def _mlstm_stages(q_ref, k_ref, v_ref, z_ref, og_ref, gt_ref, gain_ref, y_ref, c_ref, n_ref, m_ref):
    c = pl.program_id(1)
    cs = q_ref.shape[0]

    @pl.when(c == 0)
    def _():
        c_ref[...] = jnp.zeros_like(c_ref)
        n_ref[...] = jnp.zeros_like(n_ref)
        m_ref[...] = jnp.zeros_like(m_ref)

    row = _iota((cs, cs), 0)
    col = _iota((cs, cs), 1)
    causal = row >= col
    tril = causal.astype(F32)
    triu = (row <= col).astype(F32)
    eye = (row == col).astype(F32)
    gi = gt_ref[0:ML_HEADS, :]
    gf = gt_ref[ML_HEADS:2 * ML_HEADS, :]
    b_rows = jnp.dot(gf, triu, preferred_element_type=F32, precision=HIGHEST)
    b_cols = _dot_t(tril, gf, precision=HIGHEST)
    i_cols = _dot_t(eye, gi, precision=HIGHEST)
    heads = [slice(h * ML_D, (h + 1) * ML_D) for h in range(ML_HEADS)]
    hr = range(ML_HEADS)
    s_qk = [_dot_t(q_ref[:, sl], k_ref[:, sl]) for sl in heads]
    yield
    c_old = [c_ref[0, 0, h] for h in hr]
    q_c = [_dot(q_ref[:, sl], c_old[h].astype(BF16)) for h, sl in enumerate(heads)]
    yield
    m_old = [m_ref[0, :, h * 128:h * 128 + 1] for h in hr]
    b_col = [b_cols[:, h:h + 1] for h in hr]
    log_inter = [m_old[h] + b_col[h] for h in hr]
    log_w = [jnp.where(causal, b_col[h] - b_rows[h:h + 1, :] + gi[h:h + 1, :], NEG_LARGE) for h in hr]
    m = [jnp.maximum(log_inter[h], jnp.max(log_w[h], axis=1, keepdims=True)) for h in hr]
    yield
    qk = [s_qk[h] * jnp.where(causal, jnp.exp(jnp.where(causal, log_w[h] - m[h], 0.0)), 0.0) for h in hr]
    sc = [jnp.exp(log_inter[h] - m[h]) for h in hr]
    num = [_dot(qk[h].astype(BF16), v_ref[:, sl]) + sc[h] * q_c[h] for h, sl in enumerate(heads)]
    yield
    m_last = [m[h][cs - 1:cs] for h in hr]
    b_last = [b_col[h][cs - 1:cs] for h in hr]
    kw = [k_ref[:, sl].astype(F32) * jnp.exp(b_last[h] - b_col[h] + i_cols[:, h:h + 1] - m_last[h])
          for h, sl in enumerate(heads)]
    upd = [_dot(kw[h].T.astype(BF16), v_ref[:, sl]) for h, sl in enumerate(heads)]
    yield
    for h, sl in enumerate(heads):
        n_old = n_ref[0, :, sl]
        den = (jnp.sum(qk[h], axis=1, keepdims=True)
               + sc[h] * jnp.sum(q_ref[:, sl].astype(F32) * n_old, axis=1, keepdims=True))
        hid = num[h] / jnp.maximum(jnp.abs(den), jnp.exp(-m[h]))
        hid = og_ref[:, sl].astype(F32) * hid
        y_ref[:, sl] = _ln_gate(hid, gain_ref[:, sl], z_ref[:, sl].astype(F32))
        dec = jnp.exp(m_old[h] + b_last[h] - m_last[h])
        c_ref[0, 0, h] = dec * c_old[h] + upd[h]
        n_ref[0, :, sl] = dec * n_old + jnp.sum(kw[h], axis=0, keepdims=True)
        m_ref[0, :, h * 128:(h + 1) * 128] = jnp.broadcast_to(m_last[h], (1, 128))
        if h % 2 == 1:
            yield


N_SCAN_IN = 17


def _scan_kernel(*refs):
    ins, outs = refs[:N_SCAN_IN], refs[-8:]
    _round_robin([_hgrn_stages(*ins[0:5], *outs[0:2]), _ret_stages(*ins[5:10], *outs[2:4]),
                  _mlstm_stages(*ins[10:17], *outs[4:8])])


def _scan_prompt(g, gt, ps, pr, pp, hg_gain, ret_gain, ml_gain, layer, prev, *, bsz, seq, cs):
    nc = seq // cs
    tok = lambda b, c: b * nc + c
    col = lambda blocks, blk: pl.BlockSpec((cs, BW), lambda b, c: (tok(b, c), blocks.index(blk)))
    gain = pl.BlockSpec((None, 1, BW), lambda b, c: (layer, 0, 0))
    in_specs = [
        col(SILU_BLOCKS, _HQ), pl.BlockSpec((cs, BW), lambda b, c: (tok(b, c), 0)), col(PLAIN_BLOCKS, _HI),
        col(SILU_BLOCKS, _HZ), gain,
        col(ROT_BLOCKS, _RQ), col(ROT_BLOCKS, _RK), col(PLAIN_BLOCKS, _RV), col(SILU_BLOCKS, _RZ), gain,
        col(PLAIN_BLOCKS, _MQ), col(PLAIN_BLOCKS, _MK), col(PLAIN_BLOCKS, _MV), col(SILU_BLOCKS, _MZ),
        col(SILU_BLOCKS, _MO), pl.BlockSpec((2 * ML_HEADS, cs), lambda b, c: (0, tok(b, c))), gain,
    ]
    args = [ps, g, pp, ps, hg_gain, pr, pr, pp, ps, ret_gain, pp, pp, pp, ps, ps, gt, ml_gain]
    assert len(args) == N_SCAN_IN
    y_spec = pl.BlockSpec((cs, BW), lambda b, c: (tok(b, c), 0))
    y_shape = jax.ShapeDtypeStruct((bsz * seq, BW), BF16)
    state_specs, state_shapes, aliases = [], [], {}
    for k, (nh, d) in enumerate(((HG_HEADS, HG_D), (RET_HEADS, RET_D), (ML_HEADS, ML_D))):
        spec, shape, x_specs, x_args, alias = _state_chain(
            (DEPTH, bsz, nh, d, d), (1, 1, nh, d, d), lambda b, c: (layer, b, 0, 0, 0),
            None if prev is None else prev[k], len(args), 2 * k + 1)
        state_specs.append(spec)
        state_shapes.append(shape)
        in_specs = in_specs + x_specs
        args = args + x_args
        aliases.update(alias)
    return pl.pallas_call(
        _scan_kernel,
        grid=(bsz, nc),
        in_specs=in_specs,
        out_specs=[y_spec, state_specs[0], y_spec, state_specs[1], y_spec, state_specs[2],
                   pl.BlockSpec((1, 1, BW), lambda b, c: (b, 0, 0)),
                   pl.BlockSpec((1, 1, ML_HEADS * 128), lambda b, c: (b, 0, 0))],
        out_shape=[y_shape, state_shapes[0], y_shape, state_shapes[1], y_shape, state_shapes[2],
                   jax.ShapeDtypeStruct((bsz, 1, BW), F32),
                   jax.ShapeDtypeStruct((bsz, 1, ML_HEADS * 128), F32)],
        input_output_aliases=aliases,
        compiler_params=_cparams(2),
        name="scan_prompt",
    )(*args)


def _merge_kernel(xb_ref, ya_ref, yb_ref, yc_ref, wg0_ref, wg1_ref, wg2_ref, wbr_ref, out_ref):
    xb = xb_ref[...]
    for s in range(out_ref.shape[1] // V7X_MXU_COLS):
        cols = slice(s * V7X_MXU_COLS, (s + 1) * V7X_MXU_COLS)
        merged = None
        for n, (y_ref, wg_ref) in enumerate(((ya_ref, wg0_ref), (yb_ref, wg1_ref), (yc_ref, wg2_ref))):
            term = _sigmoid(_dot(xb, wg_ref[:, cols])) * _dot(y_ref[...], wbr_ref[n, :, cols])
            merged = term if merged is None else merged + term
        out_ref[:, cols] = merged.astype(BF16)


def _merge(xb, ya, yb, yc, w_gate, w_br, layer, *, tm, tn):
    t = xb.shape[0]
    nj = D_MODEL // tn
    gate_spec = lambda n: pl.BlockSpec((None, D_MODEL, tn), lambda i, j: (layer, 0, n * nj + j))
    return pl.pallas_call(
        _merge_kernel,
        grid=(t // tm, nj),
        in_specs=[
            pl.BlockSpec((tm, D_MODEL), lambda i, j: (i, 0)),
            pl.BlockSpec((tm, BW), lambda i, j: (i, 0)),
            pl.BlockSpec((tm, BW), lambda i, j: (i, 0)),
            pl.BlockSpec((tm, BW), lambda i, j: (i, 0)),
            gate_spec(0), gate_spec(1), gate_spec(2),
            pl.BlockSpec((None, 3, BW, tn), lambda i, j: (layer, 0, 0, j)),
        ],
        out_specs=pl.BlockSpec((tm, tn), lambda i, j: (i, j)),
        out_shape=jax.ShapeDtypeStruct((t, D_MODEL), BF16),
        compiler_params=_cparams(2),
        name="merge",
    )(xb, ya, yb, yc, w_gate, w_gate, w_gate, w_br)


def _outproj_kernel(m_ref, x_ref, wo_ref, lng_ref, lnb_ref, xo_ref, xbo_ref):
    for s in range(D_MODEL // V7X_MXU_COLS):
        cols = slice(s * V7X_MXU_COLS, (s + 1) * V7X_MXU_COLS)
        xo_ref[:, cols] = DEEPNORM_ALPHA * x_ref[:, cols] + _dot(m_ref[...], wo_ref[:, cols])
    hres = xo_ref[...]
    mu = jnp.mean(hres, axis=-1, keepdims=True)
    d = hres - mu
    var = jnp.mean(d * d, axis=-1, keepdims=True)
    x_new = d * lax.rsqrt(var + LN_EPS) * lng_ref[...] + lnb_ref[...]
    xo_ref[...] = x_new
    xbo_ref[...] = x_new.astype(BF16)


def _outproj(merged, x, w_o, lng3, lnb3, layer, *, tm):
    t = x.shape[0]
    return pl.pallas_call(
        _outproj_kernel,
        grid=(t // tm,),
        in_specs=[
            pl.BlockSpec((tm, D_MODEL), lambda i: (i, 0)),
            pl.BlockSpec((tm, D_MODEL), lambda i: (i, 0)),
            pl.BlockSpec((None, D_MODEL, D_MODEL), lambda i: (layer, 0, 0)),
            pl.BlockSpec((None, 1, D_MODEL), lambda i: (layer, 0, 0)),
            pl.BlockSpec((None, 1, D_MODEL), lambda i: (layer, 0, 0)),
        ],
        out_specs=[pl.BlockSpec((tm, D_MODEL), lambda i: (i, 0)), pl.BlockSpec((tm, D_MODEL), lambda i: (i, 0))],
        out_shape=[jax.ShapeDtypeStruct((t, D_MODEL), F32), jax.ShapeDtypeStruct((t, D_MODEL), BF16)],
        compiler_params=_cparams(1),
        name="outproj",
    )(merged, x, w_o, lng3, lnb3)


DEC_NB = 8


def _columns(x):
    d = x.shape[1]
    eye = (_iota((d, d), 0) == _iota((d, d), 1)).astype(F32)
    return _dot_t(eye, x, precision=HIGHEST)


def _pad_rows_bf16(x):
    return jnp.concatenate([x, jnp.zeros_like(x)], axis=0).astype(BF16)


def _bf16_columns(x16):
    d = x16.shape[1]
    eye = jnp.where(_iota((d, d), 0) == _iota((d, d), 1), 1.0, 0.0).astype(BF16)
    return _dot_t(eye, x16).astype(BF16)


def _rank1_stages(s_ref, so_ref, o_scr, h, sl, decays, k_cols16, v16, q16):
    d = k_cols16.shape[0]
    lane = _iota(k_cols16.shape, 1)
    k_stack = jnp.concatenate(
        [jnp.where(lane == j, k_cols16, jnp.zeros_like(k_cols16)) for j in range(DEC_NB)], axis=0)
    upd = _dot(k_stack, v16)
    yield
    s_bf = []
    for j in range(DEC_NB):
        s_new = decays[j] * s_ref[0, j, h] + upd[j * d:(j + 1) * d]
        so_ref[0, j, h] = s_new
        s_bf.append(s_new.astype(BF16))
    yield
    for j in range(DEC_NB):
        o_scr[j:j + 1, sl] = _dot(q16, s_bf[j])[j:j + 1, :]


def _rank1_updates(*args):
    for _ in _rank1_stages(*args):
        pass


def _hgrn_dec_kernel(q_ref, g_ref, v_ref, z_ref, gain_ref, s_ref, *rest):
    y_ref, so_ref, o_scr = rest[-3:]
    heads = [slice(h * HG_D, (h + 1) * HG_D) for h in range(HG_HEADS)]
    stages = []
    for h, sl in enumerate(heads):
        f = jnp.exp(g_ref[:, sl])
        f_cols = _columns(f)
        k_cols16 = _bf16_columns(_pad_rows_bf16(1.0 - f))
        q16 = _pad_rows_bf16(q_ref[:, sl].astype(F32))
        v16 = _pad_rows_bf16(v_ref[:, sl].astype(F32))
        stages.append(_rank1_stages(s_ref, so_ref, o_scr, h, sl, [f_cols[:, j:j + 1] for j in range(DEC_NB)],
                                    k_cols16, v16, q16))
    _round_robin(stages)
    for sl in heads:
        y_ref[:, sl] = _rms_gate(o_scr[:, sl], gain_ref[:, sl], z_ref[:, sl].astype(F32))


def _ret_dec_kernel(q_ref, k_ref, v_ref, z_ref, gain_ref, s_ref, *rest):
    y_ref, so_ref, o_scr = rest[-3:]
    for h in range(RET_HEADS):
        sl = slice(h * RET_D, (h + 1) * RET_D)
        gamma = 1.0 - 2.0 ** (-5.0 - h)
        k_cols16 = _bf16_columns(_pad_rows_bf16(k_ref[:, sl].astype(F32)))
        q16 = _pad_rows_bf16(q_ref[:, sl].astype(F32))
        v16 = _pad_rows_bf16(v_ref[:, sl].astype(F32))
        _rank1_updates(s_ref, so_ref, o_scr, h, sl, [gamma] * DEC_NB, k_cols16, v16, q16)
        y_ref[:, sl] = _ln_gate(o_scr[:, sl], gain_ref[:, sl], z_ref[:, sl].astype(F32))


def _mlstm_gate_dec_kernel(gt_ref, m_ref, mo_ref, w_ref, sc_ref, eps_ref):
    gi = gt_ref[0:ML_HEADS, :]
    gf = gt_ref[ML_HEADS:2 * ML_HEADS, :]
    log_inter = m_ref[...] + gf
    m_new = jnp.maximum(log_inter, gi)
    mo_ref[...] = m_new
    w_ref[...] = jnp.exp(gi - m_new)
    sc_ref[...] = jnp.exp(log_inter - m_new)
    eps_ref[...] = jnp.exp(-m_new)


def _mlstm_gate_dec(gt, m_t):
    shp = jax.ShapeDtypeStruct(m_t.shape, F32)
    return pl.pallas_call(_mlstm_gate_dec_kernel, out_shape=[shp, shp, shp, shp], name="mlstm_gate_dec")(gt, m_t)


def _mlstm_dec_kernel(w_sm, sc_sm, eps_sm, q_ref, k_ref, v_ref, z_ref, og_ref, gain_ref, n_ref, c_ref, *rest, bsz):
    y_ref, no_ref, co_ref, o_scr, kw_scr, sc_scr, eps_scr = rest[-7:]
    i = pl.program_id(0)
    for h in range(ML_HEADS):
        sl = slice(h * ML_D, (h + 1) * ML_D)
        base = h * bsz + i * DEC_NB
        qf = q_ref[:, sl].astype(F32)
        k_rows = k_ref[:, sl].astype(F32)
        for j in range(DEC_NB):
            kw_scr[j:j + 1, :] = k_rows[j:j + 1, :] * w_sm[base + j]
            sc_scr[j:j + 1, :] = jnp.full((1, 128), sc_sm[base + j], F32)
            eps_scr[j:j + 1, :] = jnp.full((1, 128), eps_sm[base + j], F32)
        kw_rows = kw_scr[...]
        kw_cols16 = _bf16_columns(_pad_rows_bf16(kw_rows))
        q16 = _pad_rows_bf16(qf)
        v16 = _pad_rows_bf16(v_ref[:, sl].astype(F32))
        _rank1_updates(c_ref, co_ref, o_scr, h, sl, [sc_sm[base + j] for j in range(DEC_NB)], kw_cols16, v16, q16)
        n_new = sc_scr[:, 0:1] * n_ref[:, sl] + kw_rows
        no_ref[:, sl] = n_new
        den = jnp.sum(qf * n_new, axis=1, keepdims=True)
        hid = o_scr[:, sl] / jnp.maximum(jnp.abs(den), eps_scr[:, 0:1])
        hid = og_ref[:, sl].astype(F32) * hid
        y_ref[:, sl] = _ln_gate(hid, gain_ref[:, sl], z_ref[:, sl].astype(F32))


def _hgrn_dec(ps, g, pp, gain3, state, layer, prev_out):
    bsz = ps.shape[0]
    in_specs = [
        pl.BlockSpec((DEC_NB, BW), lambda i: (i, SILU_BLOCKS.index(_HQ))),
        pl.BlockSpec((DEC_NB, BW), lambda i: (i, 0)),
        pl.BlockSpec((DEC_NB, BW), lambda i: (i, PLAIN_BLOCKS.index(_HI))),
        pl.BlockSpec((DEC_NB, BW), lambda i: (i, SILU_BLOCKS.index(_HZ))),
        pl.BlockSpec((None, 1, BW), lambda i: (layer, 0, 0)),
        pl.BlockSpec((1, DEC_NB, HG_HEADS, HG_D, HG_D), lambda i: (layer, i, 0, 0, 0)),
    ]
    args = [ps, g, pp, ps, gain3, state]
    s_spec, s_shape, x_specs, x_args, aliases = _state_chain(
        state.shape, (1, DEC_NB, HG_HEADS, HG_D, HG_D), lambda i: (layer, i, 0, 0, 0), prev_out, len(args), 1)
    return pl.pallas_call(
        _hgrn_dec_kernel,
        grid=(bsz // DEC_NB,),
        in_specs=in_specs + x_specs,
        out_specs=[pl.BlockSpec((DEC_NB, BW), lambda i: (i, 0)), s_spec],
        out_shape=[jax.ShapeDtypeStruct((bsz, BW), BF16), s_shape],
        scratch_shapes=[pltpu.VMEM((DEC_NB, BW), F32)],
        input_output_aliases=aliases,
        compiler_params=_cparams(1),
        name="hgrn_dec",
    )(*args, *x_args)


def _ret_dec(ps, pr, pp, gain3, state, layer, prev_out):
    bsz = ps.shape[0]
    d, nh = RET_D, RET_HEADS
    in_specs = [
        pl.BlockSpec((DEC_NB, BW), lambda i: (i, ROT_BLOCKS.index(_RQ))),
        pl.BlockSpec((DEC_NB, BW), lambda i: (i, ROT_BLOCKS.index(_RK))),
        pl.BlockSpec((DEC_NB, BW), lambda i: (i, PLAIN_BLOCKS.index(_RV))),
        pl.BlockSpec((DEC_NB, BW), lambda i: (i, SILU_BLOCKS.index(_RZ))),
        pl.BlockSpec((None, 1, BW), lambda i: (layer, 0, 0)),
        pl.BlockSpec((1, DEC_NB, nh, d, d), lambda i: (layer, i, 0, 0, 0)),
    ]
    args = [pr, pr, pp, ps, gain3, state]
    s_spec, s_shape, x_specs, x_args, aliases = _state_chain(
        state.shape, (1, DEC_NB, nh, d, d), lambda i: (layer, i, 0, 0, 0), prev_out, len(args), 1)
    return pl.pallas_call(
        _ret_dec_kernel,
        grid=(bsz // DEC_NB,),
        in_specs=in_specs + x_specs,
        out_specs=[pl.BlockSpec((DEC_NB, BW), lambda i: (i, 0)), s_spec],
        out_shape=[jax.ShapeDtypeStruct((bsz, BW), BF16), s_shape],
        scratch_shapes=[pltpu.VMEM((DEC_NB, BW), F32)],
        input_output_aliases=aliases,
        compiler_params=_cparams(1),
        name="ret_dec",
    )(*args, *x_args)


def _mlstm_dec(ps, pp, gain3, w_flat, sc_flat, eps_flat, n_state, c_state, layer, prev_out):
    bsz = ps.shape[0]
    d, nh = ML_D, ML_HEADS
    row = lambda blocks, blk: pl.BlockSpec((DEC_NB, BW), lambda i, *_: (i, blocks.index(blk)))
    in_specs = [
        row(PLAIN_BLOCKS, _MQ), row(PLAIN_BLOCKS, _MK), row(PLAIN_BLOCKS, _MV), row(SILU_BLOCKS, _MZ),
        row(SILU_BLOCKS, _MO),
        pl.BlockSpec((None, 1, BW), lambda i, *_: (layer, 0, 0)),
        pl.BlockSpec((DEC_NB, BW), lambda i, *_: (i, 0)),
        pl.BlockSpec((1, DEC_NB, nh, d, d), lambda i, *_: (layer, i, 0, 0, 0)),
    ]
    args = [pp, pp, pp, ps, ps, gain3, n_state, c_state]
    n_prefetch = 3
    c_spec, c_shape, x_specs, x_args, aliases = _state_chain(
        c_state.shape, (1, DEC_NB, nh, d, d), lambda i, *_: (layer, i, 0, 0, 0), prev_out,
        n_prefetch + len(args), 2)
    grid_spec = pltpu.PrefetchScalarGridSpec(
        num_scalar_prefetch=n_prefetch,
        grid=(bsz // DEC_NB,),
        in_specs=in_specs + x_specs,
        out_specs=[pl.BlockSpec((DEC_NB, BW), lambda i, *_: (i, 0)),
                   pl.BlockSpec((DEC_NB, BW), lambda i, *_: (i, 0)),
                   c_spec],
        scratch_shapes=[pltpu.VMEM((DEC_NB, BW), F32), pltpu.VMEM((DEC_NB, d), F32),
                        pltpu.VMEM((DEC_NB, 128), F32), pltpu.VMEM((DEC_NB, 128), F32)],
    )
    return pl.pallas_call(
        functools.partial(_mlstm_dec_kernel, bsz=bsz),
        grid_spec=grid_spec,
        out_shape=[jax.ShapeDtypeStruct((bsz, BW), BF16), jax.ShapeDtypeStruct((bsz, BW), F32), c_shape],
        input_output_aliases=aliases,
        compiler_params=_cparams(1),
        name="mlstm_dec",
    )(w_flat, sc_flat, eps_flat, *args, *x_args)


def _rotary_tables(pos):
    theta = 1.0 / (ROPE_BASE ** jnp.linspace(0.0, 1.0, RET_D // 2, dtype=F32))
    ang = pos.astype(F32)[:, None] * theta[None]
    cos = jnp.repeat(jnp.cos(ang), 2, axis=1)
    sin = jnp.repeat(jnp.sin(ang), 2, axis=1)
    sign = jnp.tile(jnp.asarray([-1.0, 1.0], F32), RET_D // 2)
    return cos, sin * sign[None]


def _rows3(a):
    return a.astype(F32)[:, None, :]


def _tile_sizes(n_tokens, seq):
    return min(1024, seq), min(1024, n_tokens), 512, min(512, n_tokens)


def kernel(x_prompt, x_sample, state_hgrn, state_ret, state_mlstm_c, state_mlstm_n, state_mlstm_m,
           w_in, hgrn_lb_logits, hgrn_norm, ret_norm, mlstm_norm, mlstm_b_i, mlstm_b_f,
           w_branch, w_out, ln_g, ln_b):
    bp, lp, _ = x_prompt.shape
    bs, ls, _ = x_sample.shape
    assert ls == 1 and w_in.dtype == F32
    lbs3 = _lower_bounds(hgrn_lb_logits.astype(F32))[:, None, :]
    assert w_in.shape[-1] == GATE_COL0 + 3 * D_MODEL
    w_in = jnp.swapaxes(w_in, 1, 2)
    w_gate, w_if = _gate_prep(w_in)
    b_if = jnp.concatenate([mlstm_b_i, mlstm_b_f], axis=-1).astype(F32)[:, :, None]
    w_br = w_branch.astype(BF16)
    w_o = w_out.astype(BF16)
    hg_gain, ret_gain, ml_gain = _rows3(hgrn_norm), _rows3(ret_norm), _rows3(mlstm_norm)
    lng3, lnb3 = _rows3(ln_g), _rows3(ln_b)
    cos_p, sin_p = _rotary_tables(jnp.arange(lp))
    cos_s, sin_s = _rotary_tables(jnp.full((bs,), PAST_LEN))

    tp, tmm_p, tnm, tmo_p = _tile_sizes(bp * lp, lp)
    _, tmm_s, _, tmo_s = _tile_sizes(bs, bs)
    x_p = x_prompt.reshape(bp * lp, D_MODEL)
    x_s = x_sample.reshape(bs, D_MODEL)
    xb_p, xb_s = x_p.astype(BF16), x_s.astype(BF16)
    hg_p = ret_p = mc_p = hg_s = ret_s = mc_s = None
    mn_l, mm_l, mn_sl, mm_sl = [], [], [], []
    for l in range(DEPTH):
        g, gt, ps, pr, pp = _project_all(xb_p, xb_s, w_in, l, lbs3, w_if, b_if, cos_p, sin_p, cos_s, sin_s, tm=tp)
        ya, hg_p, yb, ret_p, yc, mc_p, mn, mm = _scan_prompt(
            g[0], gt[0], ps[0], pr[0], pp[0], hg_gain, ret_gain, ml_gain, l,
            None if l == 0 else (hg_p, ret_p, mc_p), bsz=bp, seq=lp, cs=min(128, lp))
        merged = _merge(xb_p, ya, yb, yc, w_gate, w_br, l, tm=tmm_p, tn=tnm)
        x_p, xb_p = _outproj(merged, x_p, w_o, lng3, lnb3, l, tm=tmo_p)
        mn_l.append(mn.reshape(bp, ML_HEADS, ML_D))
        mm_l.append(mm.reshape(bp, ML_HEADS, 128)[:, :, 0])
        ya, hg_s = _hgrn_dec(ps[1], g[1], pp[1], hg_gain, state_hgrn, l, hg_s)
        yb, ret_s = _ret_dec(ps[1], pr[1], pp[1], ret_gain, state_ret, l, ret_s)
        m_new, w_t, sc_t, eps_t = _mlstm_gate_dec(gt[1], state_mlstm_m[l].T)
        yc, mn, mc_s = _mlstm_dec(ps[1], pp[1], ml_gain, w_t.reshape(-1), sc_t.reshape(-1), eps_t.reshape(-1),
                                  state_mlstm_n[l].reshape(bs, BW), state_mlstm_c, l, mc_s)
        merged = _merge(xb_s, ya, yb, yc, w_gate, w_br, l, tm=tmm_s, tn=tnm)
        x_s, xb_s = _outproj(merged, x_s, w_o, lng3, lnb3, l, tm=tmo_s)
        mn_sl.append(mn.reshape(bs, ML_HEADS, ML_D))
        mm_sl.append(m_new.T)
    y_prompt = x_p.reshape(bp, lp, D_MODEL)
    y_sample = x_s.reshape(bs, 1, D_MODEL)

    return (y_prompt, y_sample, hg_p, ret_p, mc_p, jnp.stack(mn_l), jnp.stack(mm_l),
            hg_s, ret_s, mc_s, jnp.stack(mn_sl), jnp.stack(mm_sl))
```

```python
import functools
import math

import jax
import jax.numpy as jnp
from jax import lax
from jax.experimental import pallas as pl
from jax.experimental.pallas import tpu as pltpu

F32 = jnp.float32
BF16 = jnp.bfloat16
HIGHEST = lax.Precision.HIGHEST

D_MODEL = 2048
DEPTH = 4
PAST_LEN = 16384
BW = D_MODEL // 2
HG_HEADS, HG_D = 8, 128
RET_HEADS, RET_D = 4, 256
ML_HEADS, ML_D = 4, 256
HG_LB_FLOOR = 1e-30
ROPE_BASE = 10000.0
NEG_LARGE = -1e30
NORM_EPS = 1e-6
LN_EPS = 1e-5
DEEPNORM_ALPHA = (2 * DEPTH) ** 0.25
LOG2_E = math.log2(math.e)

_HQ, _HF, _HI, _HZ, _RQ, _RK, _RV, _RZ, _MQ, _MK, _MV, _MZ, _MO = range(13)
SILU_BLOCKS = (_HQ, _HZ, _RZ, _MZ, _MO)
ROT_BLOCKS = (_RQ, _RK)
PLAIN_BLOCKS = (_HI, _RV, _MQ, _MK, _MV)
IF_COL0 = 13 * BW
GATE_COL0 = 13 * BW + 2 * ML_HEADS

V7X_VMEM_LIMIT = 56 * 1024 * 1024
V7X_MXU_COLS = 256


def _cparams(n_grid):
    return pltpu.CompilerParams(dimension_semantics=("arbitrary",) * n_grid, vmem_limit_bytes=V7X_VMEM_LIMIT)


def _sigmoid(x):
    return 1.0 / (1.0 + jnp.exp(-x))


def _round_robin(stages):
    while stages:
        stages = [s for s in stages if next(s, True) is None]


def _log_sigmoid(x):
    return jnp.minimum(x, 0.0) - jnp.log1p(jnp.exp(-jnp.abs(x)))


def _iota(shape, dim):
    return lax.broadcasted_iota(jnp.int32, shape, dim)


def _dot(a, b):
    return jnp.dot(a, b, preferred_element_type=F32)


def _dot_t(a, b, precision=None):
    return lax.dot_general(a, b, (((1,), (1,)), ((), ())), preferred_element_type=F32, precision=precision)


def _lookup(j, table):
    out = table[0]
    for k in range(1, len(table)):
        out = jnp.where(j == k, table[k], out)
    return out


def _lb_kernel(logit_ref, lb_ref):
    z = logit_ref[...]
    rows = [z[l:l + 1] for l in range(DEPTH)]
    mx = functools.reduce(jnp.maximum, rows)
    ex = [jnp.exp(r - mx) for r in rows]
    tot = functools.reduce(lambda a, b: a + b, ex)
    sm = [e / tot for e in ex]
    run = sm[0]
    lb_ref[0:1, :] = run - sm[0]
    for l in range(1, DEPTH):
        run = run + sm[l]
        lb_ref[l:l + 1, :] = run - sm[0]


def _lower_bounds(logits):
    return pl.pallas_call(_lb_kernel, out_shape=jax.ShapeDtypeStruct(logits.shape, F32), name="lower_bounds")(logits)


GATE_PREP_TN = 512
XPOSE_CHUNK = 256


def _transpose_cast(w_ref, wb_ref):
    for r in range(w_ref.shape[1] // XPOSE_CHUNK):
        ks = slice(r * XPOSE_CHUNK, (r + 1) * XPOSE_CHUNK)
        wb_ref[ks, :] = w_ref[:, ks].T.astype(BF16)


def _gate_prep_kernel(wt_ref, wif_rows_ref, wg_ref, wif_ref):
    _transpose_cast(wt_ref, wg_ref)

    @pl.when(pl.program_id(1) == 0)
    def _():
        rows = wif_rows_ref[...]
        padded = jnp.concatenate([rows, jnp.zeros((128 - rows.shape[0], D_MODEL), F32)], axis=0)
        wif_ref[...] = padded.T.astype(BF16)


def _gate_prep(w_t):
    tn = GATE_PREP_TN
    assert GATE_COL0 % 8 == 0 and IF_COL0 % 8 == 0
    return pl.pallas_call(
        _gate_prep_kernel,
        grid=(DEPTH, 3 * D_MODEL // tn),
        in_specs=[pl.BlockSpec((None, pl.Element(tn), pl.Element(D_MODEL)), lambda l, j: (l, pl.multiple_of(GATE_COL0 + j * tn, 8), 0)),
                  pl.BlockSpec((None, 2 * ML_HEADS, D_MODEL), lambda l, j: (l, IF_COL0 // (2 * ML_HEADS), 0))],
        out_specs=[pl.BlockSpec((None, D_MODEL, tn), lambda l, j: (l, 0, j)),
                   pl.BlockSpec((None, D_MODEL, 128), lambda l, j: (l, 0, 0))],
        out_shape=[jax.ShapeDtypeStruct((DEPTH, D_MODEL, 3 * D_MODEL), BF16),
                   jax.ShapeDtypeStruct((DEPTH, D_MODEL, 128), BF16)],
        compiler_params=_cparams(2),
        name="gate_prep",
    )(w_t, w_t)


def _proj_body(xb_ref, wb_ref, out_ref, epilogue):
    for s in range(wb_ref.shape[1] // V7X_MXU_COLS):
        cols = slice(s * V7X_MXU_COLS, (s + 1) * V7X_MXU_COLS)
        out_ref[:, cols] = epilogue(_dot(xb_ref[...], wb_ref[:, cols]), cols).astype(out_ref.dtype)


def _cast_weights(w_ref, wb_ref):
    @pl.when(pl.program_id(1) == 0)
    def _():
        _transpose_cast(w_ref, wb_ref)


def _proj_two_groups(xp_ref, xs_ref, wb_ref, op_ref, os_ref, epilogue_p, epilogue_s):
    last = pl.program_id(1) == pl.num_programs(1) - 1
    tm = xp_ref.shape[0]

    @pl.when(jnp.logical_not(last))
    def _():
        _proj_body(xp_ref, wb_ref, op_ref, epilogue_p)

    @pl.when(last)
    def _():
        x_all = jnp.concatenate([xp_ref[...], xs_ref[...]], axis=0)
        for s in range(wb_ref.shape[1] // V7X_MXU_COLS):
            cols = slice(s * V7X_MXU_COLS, (s + 1) * V7X_MXU_COLS)
            acc = _dot(x_all, wb_ref[:, cols])
            op_ref[:, cols] = epilogue_p(acc[:tm], cols).astype(op_ref.dtype)
            os_ref[:, cols] = epilogue_s(acc[tm:], cols).astype(os_ref.dtype)


def _proj_silu_kernel(xp_ref, xs_ref, w_ref, op_ref, os_ref, wb_ref, *, sigmoid_block):
    _cast_weights(w_ref, wb_ref)
    plain_sigmoid = pl.program_id(0) == sigmoid_block

    def epilogue(acc, cols):
        s = _sigmoid(acc)
        return jnp.where(plain_sigmoid, s, acc * s)

    _proj_two_groups(xp_ref, xs_ref, wb_ref, op_ref, os_ref, epilogue, epilogue)


def _proj_plain_kernel(xp_ref, xs_ref, w_ref, op_ref, os_ref, wb_ref, *, scaled_block, scale):
    _cast_weights(w_ref, wb_ref)
    sc = jnp.where(pl.program_id(0) == scaled_block, scale, 1.0).astype(F32)
    epilogue = lambda acc, cols: acc * sc
    _proj_two_groups(xp_ref, xs_ref, wb_ref, op_ref, os_ref, epilogue, epilogue)


def _proj_rotary_kernel(xp_ref, xs_ref, w_ref, cosp_ref, sinp_ref, coss_ref, sins_ref, op_ref, os_ref, wb_ref, *,
                        scaled_block, scale):
    _cast_weights(w_ref, wb_ref)
    sc = jnp.where(pl.program_id(0) == scaled_block, scale, 1.0).astype(F32)
    assert V7X_MXU_COLS == RET_D

    def rotate(cos_ref, sin_ref):
        def epilogue(acc, cols):
            even = (_iota(acc.shape, 1) % 2) == 0
            swapped = jnp.where(even, pltpu.roll(acc, RET_D - 1, 1), pltpu.roll(acc, 1, 1))
            return (acc * cos_ref[...] + swapped * sin_ref[...]) * sc
        return epilogue

    _proj_two_groups(xp_ref, xs_ref, wb_ref, op_ref, os_ref, rotate(cosp_ref, sinp_ref), rotate(coss_ref, sins_ref))


def _proj_logf_kernel(xp_ref, xs_ref, w_ref, lb_ref, wif_ref, bif_ref, gp_ref, gs_ref, gtp_ref, gts_ref, wb_ref):
    _cast_weights(w_ref, wb_ref)

    def gates(x_ref, gt_ref):
        pt = _dot(x_ref[...], wif_ref[...]).T[0:2 * ML_HEADS, :] + bif_ref[...]
        gt_ref[...] = jnp.where(_iota(pt.shape, 0) < ML_HEADS, pt, _log_sigmoid(pt))

    gates(xp_ref, gtp_ref)

    @pl.when(pl.program_id(1) == pl.num_programs(1) - 1)
    def _():
        gates(xs_ref, gts_ref)

    def epilogue(acc, cols):
        lb = lb_ref[:, cols]
        return jnp.log(jnp.maximum(lb, HG_LB_FLOOR) + (1.0 - lb) * _sigmoid(acc))

    _proj_two_groups(xp_ref, xs_ref, wb_ref, gp_ref, gs_ref, epilogue, epilogue)


def _proj(kern, xp, xs, w_t, layer, blocks, extra_in, extra_specs, out_dtype, *, tm, extra_out=(), extra_out_specs=()):
    tp, ts = xp.shape[0], xs.shape[0]
    n_prompt = tp // tm
    out_shape = [jax.ShapeDtypeStruct((tp, len(blocks) * BW), out_dtype),
                 jax.ShapeDtypeStruct((ts, len(blocks) * BW), out_dtype)] + list(extra_out)
    out_specs = [pl.BlockSpec((tm, BW), lambda j, i: (i, j)),
                 pl.BlockSpec((ts, BW), lambda j, i: (0, j))] + list(extra_out_specs)
    return pl.pallas_call(
        kern,
        grid=(len(blocks), n_prompt),
        in_specs=[pl.BlockSpec((tm, D_MODEL), lambda j, i: (i, 0)),
                  pl.BlockSpec((ts, D_MODEL), lambda j, i: (0, 0)),
                  pl.BlockSpec((None, BW, D_MODEL), lambda j, i: (layer, _lookup(j, blocks), 0))] + list(extra_specs),
        out_specs=out_specs,
        out_shape=out_shape,
        scratch_shapes=[pltpu.VMEM((D_MODEL, BW), BF16)],
        compiler_params=_cparams(2),
        name="proj",
    )(xp, xs, w_t, *extra_in)


def _project_all(xp, xs, w_t, layer, lbs3, w_if, b_if, cos_p, sin_p, cos_s, sin_s, *, tm):
    tp, ts = xp.shape[0], xs.shape[0]
    n_pos = cos_p.shape[0] // tm
    g_p, g_s, gt_p, gt_s = _proj(
        _proj_logf_kernel, xp, xs, w_t, layer, (_HF,),
        (lbs3, w_if, b_if),
        (pl.BlockSpec((None, 1, BW), lambda j, i: (layer, 0, 0)),
         pl.BlockSpec((None, D_MODEL, 128), lambda j, i: (layer, 0, 0)),
         pl.BlockSpec((None, 2 * ML_HEADS, 1), lambda j, i: (layer, 0, 0))),
        F32, tm=tm,
        extra_out=(jax.ShapeDtypeStruct((2 * ML_HEADS, tp), F32), jax.ShapeDtypeStruct((2 * ML_HEADS, ts), F32)),
        extra_out_specs=(pl.BlockSpec((2 * ML_HEADS, tm), lambda j, i: (0, i)),
                         pl.BlockSpec((2 * ML_HEADS, ts), lambda j, i: (0, 0))))
    ps = _proj(functools.partial(_proj_silu_kernel, sigmoid_block=SILU_BLOCKS.index(_MO)),
               xp, xs, w_t, layer, SILU_BLOCKS, (), (), BF16, tm=tm)
    pos_spec = pl.BlockSpec((tm, RET_D), lambda j, i: (i % n_pos, 0))
    one_spec = pl.BlockSpec((ts, RET_D), lambda j, i: (0, 0))
    pr = _proj(functools.partial(_proj_rotary_kernel, scaled_block=ROT_BLOCKS.index(_RK), scale=RET_D ** -0.5),
               xp, xs, w_t, layer, ROT_BLOCKS, (cos_p, sin_p, cos_s, sin_s),
               (pos_spec, pos_spec, one_spec, one_spec), BF16, tm=tm)
    pp = _proj(functools.partial(_proj_plain_kernel, scaled_block=PLAIN_BLOCKS.index(_MK), scale=ML_D ** -0.5),
               xp, xs, w_t, layer, PLAIN_BLOCKS, (), (), BF16, tm=tm)
    return (g_p, g_s), (gt_p, gt_s), ps, pr, pp


def _rms_gate(o, gain, zs):
    y = o * lax.rsqrt(jnp.mean(o * o, axis=-1, keepdims=True) + NORM_EPS)
    return (y * gain * zs).astype(BF16)


def _ln_gate(o, gain, zs):
    mu = jnp.mean(o, axis=-1, keepdims=True)
    d = o - mu
    var = jnp.mean(d * d, axis=-1, keepdims=True)
    return (d * lax.rsqrt(var + NORM_EPS) * gain * zs).astype(BF16)


def _state_chain(state_shape, block, index_map, prev_out, n_in, out_pos):
    spec = pl.BlockSpec(block, index_map)
    shape = jax.ShapeDtypeStruct(state_shape, F32)
    if prev_out is None:
        return spec, shape, [], [], {}
    return spec, shape, [pl.BlockSpec(memory_space=pl.ANY)], [prev_out], {n_in: out_pos}


def _hgrn_level_ref(bc, m):
    cs, d = bc.shape
    if m >= 8:
        parts = [jnp.broadcast_to(bc[b * 2 * m + m - 1:b * 2 * m + m], (2 * m, d)) for b in range(cs // (2 * m))]
        return parts[0] if len(parts) == 1 else jnp.concatenate(parts, axis=0)
    x3 = bc.reshape(cs // 8, 8, d)
    sub = _iota((cs // 8, 8, d), 1)
    pick = lambda r: jnp.broadcast_to(x3[:, r:r + 1, :], x3.shape)
    if m == 4:
        r3 = pick(3)
    elif m == 2:
        r3 = jnp.where(sub < 4, pick(1), pick(5))
    else:
        r3 = jnp.where(sub < 2, pick(0), jnp.where(sub < 4, pick(2), jnp.where(sub < 6, pick(4), pick(6))))
    return r3.reshape(cs, d)


def _hgrn_stages(q_ref, g_ref, v_ref, z_ref, gain_ref, y_ref, s_ref):
    c = pl.program_id(1)
    cs = q_ref.shape[0]

    @pl.when(c == 0)
    def _():
        s_ref[...] = jnp.zeros_like(s_ref)

    row = _iota((cs, cs), 0)
    col = _iota((cs, cs), 1)
    tril = (row >= col).astype(F32)
    level = jnp.where(row > col, 31 - lax.clz(row ^ col), -1)
    eye_c = row == col
    eye_d = _iota((HG_D, HG_D), 0) == _iota((HG_D, HG_D), 1)
    levels = [cs >> (k + 1) for k in range(cs.bit_length() - 1)]
    heads = [slice(h * HG_D, (h + 1) * HG_D) for h in range(HG_HEADS)]
    qb = q_ref[...]
    q = qb.astype(F32)
    g = g_ref[...] * LOG2_E
    kk = 1.0 - jnp.exp2(g)
    kkb = kk.astype(BF16)
    bc = jnp.dot(tril, g, preferred_element_type=F32, precision=HIGHEST)
    qk = q * kk
    a = [jnp.where(eye_c, jnp.sum(qk[:, sl], axis=1, keepdims=True), 0.0) for sl in heads]
    yield
    for m in levels:
        e = jnp.exp2(-jnp.abs(bc - _hgrn_level_ref(bc, m))).astype(BF16)
        qs = qb * e
        ks = kkb * e
        sel = level == m.bit_length() - 1
        a = [jnp.where(sel, _dot_t(qs[:, sl], ks[:, sl]), a_h) for sl, a_h in zip(heads, a)]
        yield
    qe = (q * jnp.exp2(bc)).astype(BF16)
    last = bc[cs - 1:cs]
    k_dec_t = (kk * jnp.exp2(last - bc)).T.astype(BF16)
    dec_row = jnp.exp2(last)
    yield
    for h, sl in enumerate(heads):
        v = v_ref[:, sl]
        s_old = s_ref[0, 0, h]
        o = _dot(qe[:, sl], s_old.astype(BF16)) + _dot(a[h].astype(BF16), v)
        y_ref[:, sl] = _rms_gate(o, gain_ref[:, sl], z_ref[:, sl].astype(F32))
        dec_col = jnp.sum(jnp.where(eye_d, dec_row[:, sl], 0.0), axis=1, keepdims=True)
        s_ref[0, 0, h] = dec_col * s_old + _dot(k_dec_t[sl, :], v)
        if h % 2 == 1:
            yield


def _ret_log_gamma(h):
    return jnp.log(jnp.full((1, 1), 1.0 - 2.0 ** (-5.0 - h), F32))


def _ret_stages(q_ref, k_ref, v_ref, z_ref, gain_ref, y_ref, s_ref):
    c = pl.program_id(1)
    cs = q_ref.shape[0]

    @pl.when(c == 0)
    def _():
        s_ref[...] = jnp.zeros_like(s_ref)

    t_col = _iota((cs, 1), 0).astype(F32)
    rel = (_iota((cs, cs), 0) - _iota((cs, cs), 1)).astype(F32)
    causal = rel >= 0.0
    heads = [slice(h * RET_D, (h + 1) * RET_D) for h in range(RET_HEADS)]
    hr = range(RET_HEADS)
    lg = [_ret_log_gamma(h) for h in hr]
    s_qk = [_dot_t(q_ref[:, sl], k_ref[:, sl]) for sl in heads]
    yield
    s_old = [s_ref[0, 0, h] for h in hr]
    q_s = [_dot(q_ref[:, sl], s_old[h].astype(BF16)) for h, sl in enumerate(heads)]
    yield
    k_dec_t = [(k_ref[:, sl].astype(F32) * jnp.exp((cs - 1.0 - t_col) * lg[h])).T.astype(BF16)
               for h, sl in enumerate(heads)]
    upd = [_dot(k_dec_t[h], v_ref[:, sl]) for h, sl in enumerate(heads)]
    yield
    a = [s_qk[h] * jnp.where(causal, jnp.exp(jnp.where(causal, rel * lg[h], 0.0)), 0.0) for h in hr]
    a_v = [_dot(a[h].astype(BF16), v_ref[:, sl]) for h, sl in enumerate(heads)]
    yield
    for h, sl in enumerate(heads):
        o = jnp.exp((t_col + 1.0) * lg[h]) * q_s[h] + a_v[h]
        y_ref[:, sl] = _ln_gate(o, gain_ref[:, sl], z_ref[:, sl].astype(F32))
        s_ref[0, 0, h] = jnp.exp(cs * lg[h]) * s_old[h] + upd[h]
        if h % 2 == 1:
            yield


def _mlstm_stages(q_ref, k_ref, v_ref, z_ref, og_ref, gt_ref, gain_ref, y_ref, c_ref, n_ref, m_ref):
    c = pl.program_id(1)
    cs = q_ref.shape[0]

    @pl.when(c == 0)
    def _():
        c_ref[...] = jnp.zeros_like(c_ref)
        n_ref[...] = jnp.zeros_like(n_ref)
        m_ref[...] = jnp.zeros_like(m_ref)

    row = _iota((cs, cs), 0)
    col = _iota((cs, cs), 1)
    causal = row >= col
    tril = causal.astype(F32)
    triu = (row <= col).astype(F32)
    eye = (row == col).astype(F32)
    gi = gt_ref[0:ML_HEADS, :]
    gf = gt_ref[ML_HEADS:2 * ML_HEADS, :]
    b_rows = jnp.dot(gf, triu, preferred_element_type=F32, precision=HIGHEST)
    b_cols = _dot_t(tril, gf, precision=HIGHEST)
    i_cols = _dot_t(eye, gi, precision=HIGHEST)
    heads = [slice(h * ML_D, (h + 1) * ML_D) for h in range(ML_HEADS)]
    hr = range(ML_HEADS)
    s_qk = [_dot_t(q_ref[:, sl], k_ref[:, sl]) for sl in heads]
    yield
    c_old = [c_ref[0, 0, h] for h in hr]
    q_c = [_dot(q_ref[:, sl], c_old[h].astype(BF16)) for h, sl in enumerate(heads)]
    yield
    m_old = [m_ref[0, :, h * 128:h * 128 + 1] for h in hr]
    b_col = [b_cols[:, h:h + 1] for h in hr]
    log_inter = [m_old[h] + b_col[h] for h in hr]
    log_w = [jnp.where(causal, b_col[h] - b_rows[h:h + 1, :] + gi[h:h + 1, :], NEG_LARGE) for h in hr]
    m = [jnp.maximum(log_inter[h], jnp.max(log_w[h], axis=1, keepdims=True)) for h in hr]
    yield
    qk = [s_qk[h] * jnp.where(causal, jnp.exp(jnp.where(causal, log_w[h] - m[h], 0.0)), 0.0) for h in hr]
    sc = [jnp.exp(log_inter[h] - m[h]) for h in hr]
    num = [_dot(qk[h].astype(BF16), v_ref[:, sl]) + sc[h] * q_c[h] for h, sl in enumerate(heads)]
    yield
    m_last = [m[h][cs - 1:cs] for h in hr]
    b_last = [b_col[h][cs - 1:cs] for h in hr]
    kw = [k_ref[:, sl].astype(F32) * jnp.exp(b_last[h] - b_col[h] + i_cols[:, h:h + 1] - m_last[h])
          for h, sl in enumerate(heads)]
    upd = [_dot(kw[h].T.astype(BF16), v_ref[:, sl]) for h, sl in enumerate(heads)]
    yield
    for h, sl in enumerate(heads):
        n_old = n_ref[0, :, sl]
        den = (jnp.sum(qk[h], axis=1, keepdims=True)
               + sc[h] * jnp.sum(q_ref[:, sl].astype(F32) * n_old, axis=1, keepdims=True))
        hid = num[h] / jnp.maximum(jnp.abs(den), jnp.exp(-m[h]))
        hid = og_ref[:, sl].astype(F32) * hid
        y_ref[:, sl] = _ln_gate(hid, gain_ref[:, sl], z_ref[:, sl].astype(F32))
        dec = jnp.exp(m_old[h] + b_last[h] - m_last[h])
        c_ref[0, 0, h] = dec * c_old[h] + upd[h]
        n_ref[0, :, sl] = dec * n_old + jnp.sum(kw[h], axis=0, keepdims=True)
        m_ref[0, :, h * 128:(h + 1) * 128] = jnp.broadcast_to(m_last[h], (1, 128))
        if h % 2 == 1:
            yield


N_SCAN_IN = 17


def _scan_kernel(*refs):
    ins, outs = refs[:N_SCAN_IN], refs[-8:]
    _round_robin([_hgrn_stages(*ins[0:5], *outs[0:2]), _ret_stages(*ins[5:10], *outs[2:4]),
                  _mlstm_stages(*ins[10:17], *outs[4:8])])


def _scan_prompt(g, gt, ps, pr, pp, hg_gain, ret_gain, ml_gain, layer, prev, *, bsz, seq, cs):
    nc = seq // cs
    tok = lambda b, c: b * nc + c
    col = lambda blocks, blk: pl.BlockSpec((cs, BW), lambda b, c: (tok(b, c), blocks.index(blk)))
    gain = pl.BlockSpec((None, 1, BW), lambda b, c: (layer, 0, 0))
    in_specs = [
        col(SILU_BLOCKS, _HQ), pl.BlockSpec((cs, BW), lambda b, c: (tok(b, c), 0)), col(PLAIN_BLOCKS, _HI),
        col(SILU_BLOCKS, _HZ), gain,
        col(ROT_BLOCKS, _RQ), col(ROT_BLOCKS, _RK), col(PLAIN_BLOCKS, _RV), col(SILU_BLOCKS, _RZ), gain,
        col(PLAIN_BLOCKS, _MQ), col(PLAIN_BLOCKS, _MK), col(PLAIN_BLOCKS, _MV), col(SILU_BLOCKS, _MZ),
        col(SILU_BLOCKS, _MO), pl.BlockSpec((2 * ML_HEADS, cs), lambda b, c: (0, tok(b, c))), gain,
    ]
    args = [ps, g, pp, ps, hg_gain, pr, pr, pp, ps, ret_gain, pp, pp, pp, ps, ps, gt, ml_gain]
    assert len(args) == N_SCAN_IN
    y_spec = pl.BlockSpec((cs, BW), lambda b, c: (tok(b, c), 0))
    y_shape = jax.ShapeDtypeStruct((bsz * seq, BW), BF16)
    state_specs, state_shapes, aliases = [], [], {}
    for k, (nh, d) in enumerate(((HG_HEADS, HG_D), (RET_HEADS, RET_D), (ML_HEADS, ML_D))):
        spec, shape, x_specs, x_args, alias = _state_chain(
            (DEPTH, bsz, nh, d, d), (1, 1, nh, d, d), lambda b, c: (layer, b, 0, 0, 0),
            None if prev is None else prev[k], len(args), 2 * k + 1)
        state_specs.append(spec)
        state_shapes.append(shape)
        in_specs = in_specs + x_specs
        args = args + x_args
        aliases.update(alias)
    return pl.pallas_call(
        _scan_kernel,
        grid=(bsz, nc),
        in_specs=in_specs,
        out_specs=[y_spec, state_specs[0], y_spec, state_specs[1], y_spec, state_specs[2],
                   pl.BlockSpec((1, 1, BW), lambda b, c: (b, 0, 0)),
                   pl.BlockSpec((1, 1, ML_HEADS * 128), lambda b, c: (b, 0, 0))],
        out_shape=[y_shape, state_shapes[0], y_shape, state_shapes[1], y_shape, state_shapes[2],
                   jax.ShapeDtypeStruct((bsz, 1, BW), F32),
                   jax.ShapeDtypeStruct((bsz, 1, ML_HEADS * 128), F32)],
        input_output_aliases=aliases,
        compiler_params=_cparams(2),
        name="scan_prompt",
    )(*args)


def _merge_kernel(xb_ref, ya_ref, yb_ref, yc_ref, wg0_ref, wg1_ref, wg2_ref, wbr_ref, out_ref):
    xb = xb_ref[...]
    for s in range(out_ref.shape[1] // V7X_MXU_COLS):
        cols = slice(s * V7X_MXU_COLS, (s + 1) * V7X_MXU_COLS)
        merged = None
        for n, (y_ref, wg_ref) in enumerate(((ya_ref, wg0_ref), (yb_ref, wg1_ref), (yc_ref, wg2_ref))):
            term = _sigmoid(_dot(xb, wg_ref[:, cols])) * _dot(y_ref[...], wbr_ref[n, :, cols])
            merged = term if merged is None else merged + term
        out_ref[:, cols] = merged.astype(BF16)


def _merge(xb, ya, yb, yc, w_gate, w_br, layer, *, tm, tn):
    t = xb.shape[0]
    nj = D_MODEL // tn
    gate_spec = lambda n: pl.BlockSpec((None, D_MODEL, tn), lambda i, j: (layer, 0, n * nj + j))
    return pl.pallas_call(
        _merge_kernel,
        grid=(t // tm, nj),
        in_specs=[
            pl.BlockSpec((tm, D_MODEL), lambda i, j: (i, 0)),
            pl.BlockSpec((tm, BW), lambda i, j: (i, 0)),
            pl.BlockSpec((tm, BW), lambda i, j: (i, 0)),
            pl.BlockSpec((tm, BW), lambda i, j: (i, 0)),
            gate_spec(0), gate_spec(1), gate_spec(2),
            pl.BlockSpec((None, 3, BW, tn), lambda i, j: (layer, 0, 0, j)),
        ],
        out_specs=pl.BlockSpec((tm, tn), lambda i, j: (i, j)),
        out_shape=jax.ShapeDtypeStruct((t, D_MODEL), BF16),
        compiler_params=_cparams(2),
        name="merge",
    )(xb, ya, yb, yc, w_gate, w_gate, w_gate, w_br)


def _outproj_kernel(m_ref, x_ref, wo_ref, lng_ref, lnb_ref, xo_ref, xbo_ref):
    for s in range(D_MODEL // V7X_MXU_COLS):
        cols = slice(s * V7X_MXU_COLS, (s + 1) * V7X_MXU_COLS)
        xo_ref[:, cols] = DEEPNORM_ALPHA * x_ref[:, cols] + _dot(m_ref[...], wo_ref[:, cols])
    hres = xo_ref[...]
    mu = jnp.mean(hres, axis=-1, keepdims=True)
    d = hres - mu
    var = jnp.mean(d * d, axis=-1, keepdims=True)
    x_new = d * lax.rsqrt(var + LN_EPS) * lng_ref[...] + lnb_ref[...]
    xo_ref[...] = x_new
    xbo_ref[...] = x_new.astype(BF16)


def _outproj(merged, x, w_o, lng3, lnb3, layer, *, tm):
    t = x.shape[0]
    return pl.pallas_call(
        _outproj_kernel,
        grid=(t // tm,),
        in_specs=[
            pl.BlockSpec((tm, D_MODEL), lambda i: (i, 0)),
            pl.BlockSpec((tm, D_MODEL), lambda i: (i, 0)),
            pl.BlockSpec((None, D_MODEL, D_MODEL), lambda i: (layer, 0, 0)),
            pl.BlockSpec((None, 1, D_MODEL), lambda i: (layer, 0, 0)),
            pl.BlockSpec((None, 1, D_MODEL), lambda i: (layer, 0, 0)),
        ],
        out_specs=[pl.BlockSpec((tm, D_MODEL), lambda i: (i, 0)), pl.BlockSpec((tm, D_MODEL), lambda i: (i, 0))],
        out_shape=[jax.ShapeDtypeStruct((t, D_MODEL), F32), jax.ShapeDtypeStruct((t, D_MODEL), BF16)],
        compiler_params=_cparams(1),
        name="outproj",
    )(merged, x, w_o, lng3, lnb3)


DEC_NB = 8


def _columns(x):
    d = x.shape[1]
    eye = (_iota((d, d), 0) == _iota((d, d), 1)).astype(F32)
    return _dot_t(eye, x, precision=HIGHEST)


def _pad_rows_bf16(x):
    return jnp.concatenate([x, jnp.zeros_like(x)], axis=0).astype(BF16)


def _bf16_columns(x16):
    d = x16.shape[1]
    eye = jnp.where(_iota((d, d), 0) == _iota((d, d), 1), 1.0, 0.0).astype(BF16)
    return _dot_t(eye, x16).astype(BF16)


def _rank1_stages(s_ref, so_ref, o_scr, h, sl, decays, k_cols16, v16, q16):
    d = k_cols16.shape[0]
    lane = _iota(k_cols16.shape, 1)
    k_stack = jnp.concatenate(
        [jnp.where(lane == j, k_cols16, jnp.zeros_like(k_cols16)) for j in range(DEC_NB)], axis=0)
    upd = _dot(k_stack, v16)
    yield
    s_bf = []
    for j in range(DEC_NB):
        s_new = decays[j] * s_ref[0, j, h] + upd[j * d:(j + 1) * d]
        so_ref[0, j, h] = s_new
        s_bf.append(s_new.astype(BF16))
    yield
    for j in range(DEC_NB):
        o_scr[j:j + 1, sl] = _dot(q16, s_bf[j])[j:j + 1, :]


def _hgrn_dec_kernel(q_ref, g_ref, v_ref, z_ref, gain_ref, s_ref, *rest):
    y_ref, so_ref, o_scr = rest[-3:]
    heads = [slice(h * HG_D, (h + 1) * HG_D) for h in range(HG_HEADS)]
    stages = []
    for h, sl in enumerate(heads):
        f = jnp.exp(g_ref[:, sl])
        f_cols = _columns(f)
        k_cols16 = _bf16_columns(_pad_rows_bf16(1.0 - f))
        q16 = _pad_rows_bf16(q_ref[:, sl].astype(F32))
        v16 = _pad_rows_bf16(v_ref[:, sl].astype(F32))
        stages.append(_rank1_stages(s_ref, so_ref, o_scr, h, sl, [f_cols[:, j:j + 1] for j in range(DEC_NB)],
                                    k_cols16, v16, q16))
    _round_robin(stages)
    for sl in heads:
        y_ref[:, sl] = _rms_gate(o_scr[:, sl], gain_ref[:, sl], z_ref[:, sl].astype(F32))


def _ret_dec_kernel(q_ref, k_ref, v_ref, z_ref, gain_ref, s_ref, *rest):
    y_ref, so_ref, o_scr = rest[-3:]
    heads = [slice(h * RET_D, (h + 1) * RET_D) for h in range(RET_HEADS)]
    stages = []
    for h, sl in enumerate(heads):
        gamma = 1.0 - 2.0 ** (-5.0 - h)
        k_cols16 = _bf16_columns(_pad_rows_bf16(k_ref[:, sl].astype(F32)))
        q16 = _pad_rows_bf16(q_ref[:, sl].astype(F32))
        v16 = _pad_rows_bf16(v_ref[:, sl].astype(F32))
        stages.append(_rank1_stages(s_ref, so_ref, o_scr, h, sl, [gamma] * DEC_NB, k_cols16, v16, q16))
    _round_robin(stages)
    for sl in heads:
        y_ref[:, sl] = _ln_gate(o_scr[:, sl], gain_ref[:, sl], z_ref[:, sl].astype(F32))


def _mlstm_gate_dec_kernel(gt_ref, m_ref, mo_ref, w_ref, sc_ref, eps_ref):
    gi = gt_ref[0:ML_HEADS, :]
    gf = gt_ref[ML_HEADS:2 * ML_HEADS, :]
    log_inter = m_ref[...] + gf
    m_new = jnp.maximum(log_inter, gi)
    mo_ref[...] = m_new
    w_ref[...] = jnp.exp(gi - m_new)
    sc_ref[...] = jnp.exp(log_inter - m_new)
    eps_ref[...] = jnp.exp(-m_new)


def _mlstm_gate_dec(gt, m_t):
    shp = jax.ShapeDtypeStruct(m_t.shape, F32)
    return pl.pallas_call(_mlstm_gate_dec_kernel, out_shape=[shp, shp, shp, shp], name="mlstm_gate_dec")(gt, m_t)


def _mlstm_dec_kernel(w_sm, sc_sm, eps_sm, q_ref, k_ref, v_ref, z_ref, og_ref, gain_ref, n_ref, c_ref, *rest, bsz):
    y_ref, no_ref, co_ref, o_scr, kw_scr, sc_scr, eps_scr = rest[-7:]
    i = pl.program_id(0)
    heads = [slice(h * ML_D, (h + 1) * ML_D) for h in range(ML_HEADS)]
    stages = []
    for h, sl in enumerate(heads):
        base = h * bsz + i * DEC_NB
        lane = slice(h * 128, (h + 1) * 128)
        k_rows = k_ref[:, sl].astype(F32)
        for j in range(DEC_NB):
            kw_scr[j:j + 1, sl] = k_rows[j:j + 1, :] * w_sm[base + j]
            sc_scr[j:j + 1, lane] = jnp.full((1, 128), sc_sm[base + j], F32)
            eps_scr[j:j + 1, lane] = jnp.full((1, 128), eps_sm[base + j], F32)
        kw_cols16 = _bf16_columns(_pad_rows_bf16(kw_scr[:, sl]))
        q16 = _pad_rows_bf16(q_ref[:, sl].astype(F32))
        v16 = _pad_rows_bf16(v_ref[:, sl].astype(F32))
        stages.append(_rank1_stages(c_ref, co_ref, o_scr, h, sl, [sc_sm[base + j] for j in range(DEC_NB)],
                                    kw_cols16, v16, q16))
    _round_robin(stages)
    for h, sl in enumerate(heads):
        n_new = sc_scr[:, h * 128:h * 128 + 1] * n_ref[:, sl] + kw_scr[:, sl]
        no_ref[:, sl] = n_new
        den = jnp.sum(q_ref[:, sl].astype(F32) * n_new, axis=1, keepdims=True)
        hid = o_scr[:, sl] / jnp.maximum(jnp.abs(den), eps_scr[:, h * 128:h * 128 + 1])
        hid = og_ref[:, sl].astype(F32) * hid
        y_ref[:, sl] = _ln_gate(hid, gain_ref[:, sl], z_ref[:, sl].astype(F32))


def _hgrn_dec(ps, g, pp, gain3, state, layer, prev_out):
    bsz = ps.shape[0]
    in_specs = [
        pl.BlockSpec((DEC_NB, BW), lambda i: (i, SILU_BLOCKS.index(_HQ))),
        pl.BlockSpec((DEC_NB, BW), lambda i: (i, 0)),
        pl.BlockSpec((DEC_NB, BW), lambda i: (i, PLAIN_BLOCKS.index(_HI))),
        pl.BlockSpec((DEC_NB, BW), lambda i: (i, SILU_BLOCKS.index(_HZ))),
        pl.BlockSpec((None, 1, BW), lambda i: (layer, 0, 0)),
        pl.BlockSpec((1, DEC_NB, HG_HEADS, HG_D, HG_D), lambda i: (layer, i, 0, 0, 0)),
    ]
    args = [ps, g, pp, ps, gain3, state]
    s_spec, s_shape, x_specs, x_args, aliases = _state_chain(
        state.shape, (1, DEC_NB, HG_HEADS, HG_D, HG_D), lambda i: (layer, i, 0, 0, 0), prev_out, len(args), 1)
    return pl.pallas_call(
        _hgrn_dec_kernel,
        grid=(bsz // DEC_NB,),
        in_specs=in_specs + x_specs,
        out_specs=[pl.BlockSpec((DEC_NB, BW), lambda i: (i, 0)), s_spec],
        out_shape=[jax.ShapeDtypeStruct((bsz, BW), BF16), s_shape],
        scratch_shapes=[pltpu.VMEM((DEC_NB, BW), F32)],
        input_output_aliases=aliases,
        compiler_params=_cparams(1),
        name="hgrn_dec",
    )(*args, *x_args)


def _ret_dec(ps, pr, pp, gain3, state, layer, prev_out):
    bsz = ps.shape[0]
    d, nh = RET_D, RET_HEADS
    in_specs = [
        pl.BlockSpec((DEC_NB, BW), lambda i: (i, ROT_BLOCKS.index(_RQ))),
        pl.BlockSpec((DEC_NB, BW), lambda i: (i, ROT_BLOCKS.index(_RK))),
        pl.BlockSpec((DEC_NB, BW), lambda i: (i, PLAIN_BLOCKS.index(_RV))),
        pl.BlockSpec((DEC_NB, BW), lambda i: (i, SILU_BLOCKS.index(_RZ))),
        pl.BlockSpec((None, 1, BW), lambda i: (layer, 0, 0)),
        pl.BlockSpec((1, DEC_NB, nh, d, d), lambda i: (layer, i, 0, 0, 0)),
    ]
    args = [pr, pr, pp, ps, gain3, state]
    s_spec, s_shape, x_specs, x_args, aliases = _state_chain(
        state.shape, (1, DEC_NB, nh, d, d), lambda i: (layer, i, 0, 0, 0), prev_out, len(args), 1)
    return pl.pallas_call(
        _ret_dec_kernel,
        grid=(bsz // DEC_NB,),
        in_specs=in_specs + x_specs,
        out_specs=[pl.BlockSpec((DEC_NB, BW), lambda i: (i, 0)), s_spec],
        out_shape=[jax.ShapeDtypeStruct((bsz, BW), BF16), s_shape],
        scratch_shapes=[pltpu.VMEM((DEC_NB, BW), F32)],
        input_output_aliases=aliases,
        compiler_params=_cparams(1),
        name="ret_dec",
    )(*args, *x_args)


def _mlstm_dec(ps, pp, gain3, w_flat, sc_flat, eps_flat, n_state, c_state, layer, prev_out):
    bsz = ps.shape[0]
    d, nh = ML_D, ML_HEADS
    row = lambda blocks, blk: pl.BlockSpec((DEC_NB, BW), lambda i, *_: (i, blocks.index(blk)))
    in_specs = [
        row(PLAIN_BLOCKS, _MQ), row(PLAIN_BLOCKS, _MK), row(PLAIN_BLOCKS, _MV), row(SILU_BLOCKS, _MZ),
        row(SILU_BLOCKS, _MO),
        pl.BlockSpec((None, 1, BW), lambda i, *_: (layer, 0, 0)),
        pl.BlockSpec((DEC_NB, BW), lambda i, *_: (i, 0)),
        pl.BlockSpec((1, DEC_NB, nh, d, d), lambda i, *_: (layer, i, 0, 0, 0)),
    ]
    args = [pp, pp, pp, ps, ps, gain3, n_state, c_state]
    n_prefetch = 3
    c_spec, c_shape, x_specs, x_args, aliases = _state_chain(
        c_state.shape, (1, DEC_NB, nh, d, d), lambda i, *_: (layer, i, 0, 0, 0), prev_out,
        n_prefetch + len(args), 2)
    grid_spec = pltpu.PrefetchScalarGridSpec(
        num_scalar_prefetch=n_prefetch,
        grid=(bsz // DEC_NB,),
        in_specs=in_specs + x_specs,
        out_specs=[pl.BlockSpec((DEC_NB, BW), lambda i, *_: (i, 0)),
                   pl.BlockSpec((DEC_NB, BW), lambda i, *_: (i, 0)),
                   c_spec],
        scratch_shapes=[pltpu.VMEM((DEC_NB, BW), F32), pltpu.VMEM((DEC_NB, BW), F32),
                        pltpu.VMEM((DEC_NB, nh * 128), F32), pltpu.VMEM((DEC_NB, nh * 128), F32)],
    )
    return pl.pallas_call(
        functools.partial(_mlstm_dec_kernel, bsz=bsz),
        grid_spec=grid_spec,
        out_shape=[jax.ShapeDtypeStruct((bsz, BW), BF16), jax.ShapeDtypeStruct((bsz, BW), F32), c_shape],
        input_output_aliases=aliases,
        compiler_params=_cparams(1),
        name="mlstm_dec",
    )(w_flat, sc_flat, eps_flat, *args, *x_args)


def _rotary_tables(pos):
    theta = 1.0 / (ROPE_BASE ** jnp.linspace(0.0, 1.0, RET_D // 2, dtype=F32))
    ang = pos.astype(F32)[:, None] * theta[None]
    cos = jnp.repeat(jnp.cos(ang), 2, axis=1)
    sin = jnp.repeat(jnp.sin(ang), 2, axis=1)
    sign = jnp.tile(jnp.asarray([-1.0, 1.0], F32), RET_D // 2)
    return cos, sin * sign[None]


def _rows3(a):
    return a.astype(F32)[:, None, :]


def _tile_sizes(n_tokens, seq):
    return min(1024, seq), min(1024, n_tokens), 512, min(512, n_tokens)


def kernel(x_prompt, x_sample, state_hgrn, state_ret, state_mlstm_c, state_mlstm_n, state_mlstm_m,
           w_in, hgrn_lb_logits, hgrn_norm, ret_norm, mlstm_norm, mlstm_b_i, mlstm_b_f,
           w_branch, w_out, ln_g, ln_b):
    bp, lp, _ = x_prompt.shape
    bs, ls, _ = x_sample.shape
    assert ls == 1 and w_in.dtype == F32
    lbs3 = _lower_bounds(hgrn_lb_logits.astype(F32))[:, None, :]
    assert w_in.shape[-1] == GATE_COL0 + 3 * D_MODEL
    w_in = jnp.swapaxes(w_in, 1, 2)
    w_gate, w_if = _gate_prep(w_in)
    b_if = jnp.concatenate([mlstm_b_i, mlstm_b_f], axis=-1).astype(F32)[:, :, None]
    w_br = w_branch.astype(BF16)
    w_o = w_out.astype(BF16)
    hg_gain, ret_gain, ml_gain = _rows3(hgrn_norm), _rows3(ret_norm), _rows3(mlstm_norm)
    lng3, lnb3 = _rows3(ln_g), _rows3(ln_b)
    cos_p, sin_p = _rotary_tables(jnp.arange(lp))
    cos_s, sin_s = _rotary_tables(jnp.full((bs,), PAST_LEN))

    tp, tmm_p, tnm, tmo_p = _tile_sizes(bp * lp, lp)
    _, tmm_s, _, tmo_s = _tile_sizes(bs, bs)
    x_p = x_prompt.reshape(bp * lp, D_MODEL)
    x_s = x_sample.reshape(bs, D_MODEL)
    xb_p, xb_s = x_p.astype(BF16), x_s.astype(BF16)
    hg_p = ret_p = mc_p = hg_s = ret_s = mc_s = None
    mn_l, mm_l, mn_sl, mm_sl = [], [], [], []
    for l in range(DEPTH):
        g, gt, ps, pr, pp = _project_all(xb_p, xb_s, w_in, l, lbs3, w_if, b_if, cos_p, sin_p, cos_s, sin_s, tm=tp)
        ya, hg_p, yb, ret_p, yc, mc_p, mn, mm = _scan_prompt(
            g[0], gt[0], ps[0], pr[0], pp[0], hg_gain, ret_gain, ml_gain, l,
            None if l == 0 else (hg_p, ret_p, mc_p), bsz=bp, seq=lp, cs=min(128, lp))
        merged = _merge(xb_p, ya, yb, yc, w_gate, w_br, l, tm=tmm_p, tn=tnm)
        x_p, xb_p = _outproj(merged, x_p, w_o, lng3, lnb3, l, tm=tmo_p)
        mn_l.append(mn.reshape(bp, ML_HEADS, ML_D))
        mm_l.append(mm.reshape(bp, ML_HEADS, 128)[:, :, 0])
        ya, hg_s = _hgrn_dec(ps[1], g[1], pp[1], hg_gain, state_hgrn, l, hg_s)
        yb, ret_s = _ret_dec(ps[1], pr[1], pp[1], ret_gain, state_ret, l, ret_s)
        m_new, w_t, sc_t, eps_t = _mlstm_gate_dec(gt[1], state_mlstm_m[l].T)
        yc, mn, mc_s = _mlstm_dec(ps[1], pp[1], ml_gain, w_t.reshape(-1), sc_t.reshape(-1), eps_t.reshape(-1),
                                  state_mlstm_n[l].reshape(bs, BW), state_mlstm_c, l, mc_s)
        merged = _merge(xb_s, ya, yb, yc, w_gate, w_br, l, tm=tmm_s, tn=tnm)
        x_s, xb_s = _outproj(merged, x_s, w_o, lng3, lnb3, l, tm=tmo_s)
        mn_sl.append(mn.reshape(bs, ML_HEADS, ML_D))
        mm_sl.append(m_new.T)
    y_prompt = x_p.reshape(bp, lp, D_MODEL)
    y_sample = x_s.reshape(bs, 1, D_MODEL)

    return (y_prompt, y_sample, hg_p, ret_p, mc_p, jnp.stack(mn_l), jnp.stack(mm_l),
            hg_s, ret_s, mc_s, jnp.stack(mn_sl), jnp.stack(mm_sl))
```
